```python
import math
import jax, jax.numpy as jnp
from jax import lax
import numpy as np

D_MODEL = 1024
BATCH = 16
SEQ = 2048
DEPTH = 4

N_A_LAYERS = DEPTH // 2
N_B_LAYERS = DEPTH - N_A_LAYERS
RWKV_HEAD_DIM = 64
RWKV_HEADS = D_MODEL // RWKV_HEAD_DIM
DECAY_LORA = 64
ICLR_LORA = 64
VRES_LORA = 32
GATE_LORA = 160
RWKV_GN_EPS = 64e-5
L2_EPS = 1e-12
DA_HEAD_DIM = 64
DA_HEADS = D_MODEL // (2 * DA_HEAD_DIM)
DA_QK_WIDTH = DA_HEADS * 2 * DA_HEAD_DIM
DA_V_WIDTH = DA_HEADS * 2 * DA_HEAD_DIM
BLOCK_Q = 128
SUBLN_EPS = 1e-5
N_EXPERTS = 32
TOP_K = 4
EXPERT_FF = D_MODEL
SWIGLU_LIMIT = 7.0
SWIGLU_ALPHA = 1.702
EXPERT_BLOCK = 512
DEEPNORM_ALPHA = (2 * DEPTH) ** 0.25
DEEPNORM_BETA = (8 * DEPTH) ** -0.25
LN_EPS = 1e-5

kernel_name = "yoco_rwkv7_diffattn_moe_deepnorm"

F32 = jnp.float32


def layer_norm(x, g, b):
    xf = x.astype(F32)
    mu = jnp.mean(xf, axis=-1, keepdims=True)
    var = jnp.mean(jnp.square(xf - mu), axis=-1, keepdims=True)
    return ((xf - mu) * lax.rsqrt(var + LN_EPS) * g + b).astype(x.dtype)


def wkv7_scan(r, w, k, v, a, b):
    B_, S_, H, N = r.shape

    def step(state, inp):
        r_t, w_t, k_t, v_t, a_t, b_t = inp
        sa = jnp.einsum('bhvk,bhk->bhv', state, a_t)
        state = (state * w_t[:, :, None, :] + sa[..., None] * b_t[:, :, None, :]
                 + v_t[..., None] * k_t[:, :, None, :])
        y_t = jnp.einsum('bhvk,bhk->bhv', state, r_t)
        return state, y_t

    xs = tuple(jnp.moveaxis(t, 1, 0) for t in (r, w, k, v, a, b))
    state0 = jnp.zeros((B_, H, N, N), F32)
    _, y = lax.scan(step, state0, xs)
    return jnp.moveaxis(y, 0, 1)


def rwkv7_time_mix(x, v_first, mix, w_rkv, w_o, w0, w1, w2, a0, a1, a2, g1, g2,
                   k_k, k_a, r_k, lnx_g, lnx_b, v0=None, v1=None, v2=None):
    B_, S_, D_ = x.shape
    H, N = RWKV_HEADS, RWKV_HEAD_DIM
    xx = jnp.pad(x, ((0, 0), (1, 0), (0, 0)))[:, :-1] - x
    xm = x[:, :, None, :] + xx[:, :, None, :] * mix
    rkv = jnp.einsum('bsjd,jde->bsje', xm[:, :, :3], w_rkv)
    r, k, v = rkv[:, :, 0], rkv[:, :, 1], rkv[:, :, 2]
    xv, xw, xa, xg = xm[:, :, 2], xm[:, :, 3], xm[:, :, 4], xm[:, :, 5]
    w_log = -jax.nn.softplus(-(w0 + jnp.tanh(xw @ w1) @ w2)) - 0.5
    decay = jnp.exp(-jnp.exp(w_log.astype(F32)))
    if v0 is None:
        v_first = v
    else:
        v = v + (v_first - v) * jax.nn.sigmoid(v0 + (xv @ v1) @ v2)
    a = jax.nn.sigmoid(a0 + (xa @ a1) @ a2)
    g = jax.nn.sigmoid(xg @ g1) @ g2

    def heads(t):
        return t.astype(F32).reshape(B_, S_, H, N)

    r_h, k_h, v_h, a_h, w_h = heads(r), heads(k), heads(v), heads(a), heads(decay)
    kk = k_h * k_k.astype(F32).reshape(H, N)
    kk = kk / jnp.maximum(jnp.sqrt(jnp.sum(kk * kk, axis=-1, keepdims=True)), L2_EPS)
    k_h = k_h * (1.0 + (a_h - 1.0) * k_a.astype(F32).reshape(H, N))
    y = wkv7_scan(r_h, w_h, k_h, v_h, -kk, kk * a_h)
    mu = jnp.mean(y, axis=-1, keepdims=True)
    var = jnp.mean(jnp.square(y - mu), axis=-1, keepdims=True)
    y = ((y - mu) * lax.rsqrt(var + RWKV_GN_EPS)).reshape(B_, S_, D_) * lnx_g + lnx_b
    bonus = jnp.sum(r_h * k_h * r_k.astype(F32), axis=-1, keepdims=True) * v_h
    y = y + bonus.reshape(B_, S_, D_)
    out = (y * g).astype(x.dtype) @ w_o
    return out, v_first


def shared_kv(x, kv_w):
    B_, S_, _ = x.shape
    kv = x @ kv_w
    k = kv[..., :DA_QK_WIDTH].reshape(B_, S_, DA_HEADS, 2, DA_HEAD_DIM).transpose(0, 2, 3, 1, 4)
    v = kv[..., DA_QK_WIDTH:].reshape(B_, S_, DA_HEADS, 2 * DA_HEAD_DIM).transpose(0, 2, 1, 3)
    return k, v


def alibi_slopes(n_heads):
    return 2.0 ** (-8.0 * jnp.arange(1, n_heads + 1, dtype=F32) / n_heads)


def diff_attention(x, k_sh, v_sh, w_q, w_o, lam, subln_g, lambda_init):
    B_, S_, _ = x.shape
    scale = DA_HEAD_DIM ** -0.5
    q = (x @ w_q).reshape(B_, S_, DA_HEADS, 2, DA_HEAD_DIM).transpose(0, 2, 3, 1, 4)
    lam_f = lam.astype(F32)
    lam_full = (jnp.exp(jnp.sum(lam_f[0] * lam_f[1])) - jnp.exp(jnp.sum(lam_f[2] * lam_f[3]))
                + lambda_init)
    slopes = alibi_slopes(DA_HEADS)
    outs = []
    for i in range(S_ // BLOCK_Q):
        q0, q1 = i * BLOCK_Q, (i + 1) * BLOCK_Q
        qb = q[:, :, :, q0:q1]
        kb = k_sh[:, :, :, :q1]
        vb = v_sh[:, :, :q1]
        s = jnp.einsum('bhcqd,bhckd->bhcqk', qb, kb).astype(F32) * scale
        dist = (q0 + jnp.arange(BLOCK_Q))[:, None] - jnp.arange(q1)[None, :]
        bias = jnp.where(dist >= 0, -slopes[:, None, None] * dist.astype(F32), -jnp.inf)
        p = jax.nn.softmax(s + bias[None, :, None], axis=-1)
        attn = p[:, :, 0] - lam_full * p[:, :, 1]
        outs.append(jnp.einsum('bhqk,bhkd->bhqd', attn.astype(vb.dtype), vb))
    o = jnp.concatenate(outs, axis=2).astype(F32)
    o = o * lax.rsqrt(jnp.mean(o * o, axis=-1, keepdims=True) + SUBLN_EPS) * subln_g
    o = o * (1.0 - lambda_init)
    o = o.transpose(0, 2, 1, 3).reshape(B_, S_, DA_V_WIDTH).astype(x.dtype)
    return o @ w_o


def clamped_swiglu(h):
    gate, up = h[..., :EXPERT_FF], h[..., EXPERT_FF:]
    gate = jnp.minimum(gate, SWIGLU_LIMIT)
    up = jnp.clip(up, -SWIGLU_LIMIT, SWIGLU_LIMIT)
    return (up + 1.0) * (gate * jax.nn.sigmoid(SWIGLU_ALPHA * gate))


def moe_ffn(x, router_w, router_b, w_gu, b_gu, w_dn, b_dn):
    B_, S_, D_ = x.shape
    T = B_ * S_
    TK = T * TOP_K
    xt = x.reshape(T, D_)
    logits = (xt @ router_w + router_b).astype(F32)
    top_logit, top_idx = lax.top_k(logits, TOP_K)
    gate = jax.nn.softmax(top_logit, axis=-1)
    flat_e = top_idx.reshape(TK)
    flat_tok = jnp.arange(TK, dtype=jnp.int32) // TOP_K
    flat_gate = gate.reshape(TK)
    order = jnp.argsort(flat_e)
    e_sorted = flat_e[order]
    counts = jnp.bincount(flat_e, length=N_EXPERTS)
    padded = (counts + EXPERT_BLOCK - 1) // EXPERT_BLOCK * EXPERT_BLOCK
    pad_end = jnp.cumsum(padded)
    pad_start = pad_end - padded
    grp_start = jnp.cumsum(counts) - counts
    dest = pad_start[e_sorted] + jnp.arange(TK, dtype=jnp.int32) - grp_start[e_sorted]
    n_blocks = -(-TK // EXPERT_BLOCK) + N_EXPERTS
    P = n_blocks * EXPERT_BLOCK
    row_tok = jnp.full((P,), T, jnp.int32).at[dest].set(flat_tok[order])
    row_gate = jnp.zeros((P,), F32).at[dest].set(flat_gate[order])
    block_start = jnp.arange(n_blocks, dtype=jnp.int32) * EXPERT_BLOCK
    block_exp = jnp.minimum(jnp.searchsorted(pad_end, block_start, side='right'), N_EXPERTS - 1)
    x_pad = jnp.concatenate([xt, jnp.zeros((1, D_), xt.dtype)], axis=0)

    def expert_block(args):
        toks, e = args
        h = x_pad[toks] @ w_gu[e] + b_gu[e]
        return clamped_swiglu(h) @ w_dn[e] + b_dn[e]

    y_rows = lax.map(expert_block, (row_tok.reshape(n_blocks, EXPERT_BLOCK), block_exp))
    y_rows = y_rows.reshape(P, D_).astype(F32) * row_gate[:, None]
    out = jnp.zeros((T + 1, D_), F32).at[row_tok].add(y_rows)[:T]
    return out.reshape(B_, S_, D_).astype(x.dtype)


def setup_inputs(seed: int = 0) -> dict:
    key = jax.random.key(seed)
    ks = iter(jax.random.split(key, 48))

    def nrm(shape, scale):
        return jax.random.normal(next(ks), shape, F32) * scale

    D, NA, NB, E, FF = D_MODEL, N_A_LAYERS, N_B_LAYERS, N_EXPERTS, EXPERT_FF
    H, N = RWKV_HEADS, RWKV_HEAD_DIM
    x = nrm((BATCH, SEQ, D), 1.0)
    ln_g = 1.0 + nrm((DEPTH, 2, D), 0.05)
    ln_b = nrm((DEPTH, 2, D), 0.02)
    rwkv_mix = jax.random.uniform(next(ks), (NA, 6, D), F32)
    rkv_scale = jnp.array([1.0, 1.0, DEEPNORM_BETA], F32)[None, :, None, None] * D ** -0.5
    rwkv_w_rkv = nrm((NA, 3, D, D), 1.0) * rkv_scale
    rwkv_w_o = nrm((NA, D, D), D ** -0.5 * DEEPNORM_BETA)
    rwkv_w0 = jnp.linspace(-6.0, -1.0, D, dtype=F32)[None, :] + nrm((NA, D), 0.1)
    rwkv_w1 = nrm((NA, D, DECAY_LORA), D ** -0.5)
    rwkv_w2 = nrm((NA, DECAY_LORA, D), 0.1 * DECAY_LORA ** -0.5)
    rwkv_a0 = nrm((NA, D), 0.1)
    rwkv_a1 = nrm((NA, D, ICLR_LORA), D ** -0.5)
    rwkv_a2 = nrm((NA, ICLR_LORA, D), 0.5 * ICLR_LORA ** -0.5)
    rwkv_g1 = nrm((NA, D, GATE_LORA), D ** -0.5)
    rwkv_g2 = nrm((NA, GATE_LORA, D), GATE_LORA ** -0.5)
    rwkv_k_k = 0.85 + nrm((NA, D), 0.05)
    rwkv_k_a = 1.0 + nrm((NA, D), 0.05)
    rwkv_r_k = nrm((NA, H, N), 0.1)
    rwkv_lnx_g = 1.0 + nrm((NA, D), 0.05)
    rwkv_lnx_b = nrm((NA, D), 0.02)
    rwkv_v0 = 1.0 + nrm((NA - 1, D), 0.1)
    rwkv_v1 = nrm((NA - 1, D, VRES_LORA), D ** -0.5)
    rwkv_v2 = nrm((NA - 1, VRES_LORA, D), 0.5 * VRES_LORA ** -0.5)
    kv_scale = jnp.concatenate([jnp.ones((DA_QK_WIDTH,), F32),
                                jnp.full((DA_V_WIDTH,), DEEPNORM_BETA, F32)]) * D ** -0.5
    kv_w = nrm((D, DA_QK_WIDTH + DA_V_WIDTH), 1.0) * kv_scale
    da_w_q = nrm((NB, D, DA_QK_WIDTH), D ** -0.5)
    da_w_o = nrm((NB, DA_V_WIDTH, D), DA_V_WIDTH ** -0.5 * DEEPNORM_BETA)
    da_lambda = nrm((NB, 4, DA_HEAD_DIM), 0.1)
    da_subln_g = 1.0 + nrm((NB, 2 * DA_HEAD_DIM), 0.05)
    moe_router_w = nrm((DEPTH, D, E), D ** -0.5)
    moe_router_b = nrm((DEPTH, E), 0.01)
    moe_w_gu = nrm((DEPTH, E, D, 2 * FF), D ** -0.5)
    moe_b_gu = nrm((DEPTH, E, 2 * FF), 0.02)
    moe_w_dn = nrm((DEPTH, E, FF, D), FF ** -0.5 * DEEPNORM_BETA)
    moe_b_dn = nrm((DEPTH, E, D), 0.02)
    return {"x": x, "ln_g": ln_g, "ln_b": ln_b,
            "rwkv_mix": rwkv_mix, "rwkv_w_rkv": rwkv_w_rkv, "rwkv_w_o": rwkv_w_o,
            "rwkv_w0": rwkv_w0, "rwkv_w1": rwkv_w1, "rwkv_w2": rwkv_w2,
            "rwkv_a0": rwkv_a0, "rwkv_a1": rwkv_a1, "rwkv_a2": rwkv_a2,
            "rwkv_g1": rwkv_g1, "rwkv_g2": rwkv_g2, "rwkv_k_k": rwkv_k_k, "rwkv_k_a": rwkv_k_a,
            "rwkv_r_k": rwkv_r_k, "rwkv_lnx_g": rwkv_lnx_g, "rwkv_lnx_b": rwkv_lnx_b,
            "rwkv_v0": rwkv_v0, "rwkv_v1": rwkv_v1, "rwkv_v2": rwkv_v2,
            "kv_w": kv_w, "da_w_q": da_w_q, "da_w_o": da_w_o, "da_lambda": da_lambda,
            "da_subln_g": da_subln_g,
            "moe_router_w": moe_router_w, "moe_router_b": moe_router_b,
            "moe_w_gu": moe_w_gu, "moe_b_gu": moe_b_gu, "moe_w_dn": moe_w_dn, "moe_b_dn": moe_b_dn}


def reference(x, ln_g, ln_b, rwkv_mix, rwkv_w_rkv, rwkv_w_o, rwkv_w0, rwkv_w1, rwkv_w2,
              rwkv_a0, rwkv_a1, rwkv_a2, rwkv_g1, rwkv_g2, rwkv_k_k, rwkv_k_a, rwkv_r_k,
              rwkv_lnx_g, rwkv_lnx_b, rwkv_v0, rwkv_v1, rwkv_v2, kv_w, da_w_q, da_w_o,
              da_lambda, da_subln_g, moe_router_w, moe_router_b, moe_w_gu, moe_b_gu,
              moe_w_dn, moe_b_dn):
    v_first = None
    k_sh = v_sh = None
    for l in range(DEPTH):
        if l < N_A_LAYERS:
            vres = (None, None, None) if l == 0 else (rwkv_v0[l - 1], rwkv_v1[l - 1], rwkv_v2[l - 1])
            mix_out, v_first = rwkv7_time_mix(
                x, v_first, rwkv_mix[l], rwkv_w_rkv[l], rwkv_w_o[l], rwkv_w0[l], rwkv_w1[l],
                rwkv_w2[l], rwkv_a0[l], rwkv_a1[l], rwkv_a2[l], rwkv_g1[l], rwkv_g2[l],
                rwkv_k_k[l], rwkv_k_a[l], rwkv_r_k[l], rwkv_lnx_g[l], rwkv_lnx_b[l], *vres)
        else:
            if l == N_A_LAYERS:
                k_sh, v_sh = shared_kv(x, kv_w)
            j = l - N_A_LAYERS
            lambda_init = 0.8 - 0.6 * math.exp(-0.3 * l)
            mix_out = diff_attention(x, k_sh, v_sh, da_w_q[j], da_w_o[j], da_lambda[j],
                                     da_subln_g[j], lambda_init)
        x = layer_norm(DEEPNORM_ALPHA * x + mix_out, ln_g[l, 0], ln_b[l, 0])
        ffn_out = moe_ffn(x, moe_router_w[l], moe_router_b[l], moe_w_gu[l], moe_b_gu[l],
                          moe_w_dn[l], moe_b_dn[l])
        x = layer_norm(DEEPNORM_ALPHA * x + ffn_out, ln_g[l, 1], ln_b[l, 1])
    return x
```

```python
import functools
import math

import jax
import jax.numpy as jnp
from jax import lax
from jax.experimental import pallas as pl
from jax.experimental.pallas import tpu as pltpu

F32 = jnp.float32
BF16 = jnp.bfloat16

D_MODEL = 1024
DEPTH = 4
N_A_LAYERS = DEPTH // 2
RWKV_HEAD_DIM = 64
RWKV_GN_EPS = 64e-5
L2_EPS = 1e-12
DA_HEAD_DIM = 64
DA_HEADS = D_MODEL // (2 * DA_HEAD_DIM)
SUBLN_EPS = 1e-5
N_EXPERTS = 32
TOP_K = 4
EXPERT_FF = D_MODEL
SWIGLU_LIMIT = 7.0
SWIGLU_ALPHA = 1.702
EXPERT_BLOCK = 512
DEEPNORM_ALPHA = (2 * DEPTH) ** 0.25
LN_EPS = 1e-5

LANES = 128
SUBLANES = 8
VMEM_LIMIT_BYTES = 56 * 1024 * 1024

WKV_CHUNK = 64
HEADS_PER_TILE = LANES // RWKV_HEAD_DIM
NEG_BIG = -1e30


def _params(*sem):
    return pltpu.CompilerParams(dimension_semantics=sem, vmem_limit_bytes=VMEM_LIMIT_BYTES)


def _dot(a, b):
    return jnp.dot(a.astype(BF16), b.astype(BF16), preferred_element_type=F32)


def _dot_nt(a, b):
    return lax.dot_general(a.astype(BF16), b.astype(BF16), (((1,), (1,)), ((), ())),
                           preferred_element_type=F32)


def _split3(x):
    hi = x.astype(BF16)
    r1 = x - hi.astype(F32)
    mid = r1.astype(BF16)
    lo = (r1 - mid.astype(F32)).astype(BF16)
    return hi, mid, lo


def _dot_exact_rhs(x, m_bf16):
    hi, mid, lo = _split3(x)
    return (jnp.dot(hi, m_bf16, preferred_element_type=F32)
            + jnp.dot(mid, m_bf16, preferred_element_type=F32)
            + jnp.dot(lo, m_bf16, preferred_element_type=F32))


def _dot_exact_lhs(m_bf16, x):
    hi, mid, lo = _split3(x)
    return (jnp.dot(m_bf16, hi, preferred_element_type=F32)
            + jnp.dot(m_bf16, mid, preferred_element_type=F32)
            + jnp.dot(m_bf16, lo, preferred_element_type=F32))


def _layer_norm(z, g, b):
    mu = jnp.mean(z, axis=-1, keepdims=True)
    d = z - mu
    var = jnp.mean(d * d, axis=-1, keepdims=True)
    return d * lax.rsqrt(var + LN_EPS) * g + b


def _row_tile(n, want):
    t = min(want, n)
    assert n % t == 0 and t % SUBLANES == 0
    return t


def _rwkv_proj_kernel(has_vres, *refs):
    if has_vres:
        (x_ref, xp_ref, mix_ref, wrkv_ref, w1_ref, w2_ref, a1_ref, a2_ref, g1_ref, g2_ref,
         vec_ref, v1_ref, v2_ref, vfirst_ref,
         r_out, lw_out, k_out, v_out, a_out, g_out) = refs
    else:
        (x_ref, xp_ref, mix_ref, wrkv_ref, w1_ref, w2_ref, a1_ref, a2_ref, g1_ref, g2_ref,
         vec_ref, r_out, lw_out, k_out, v_out, a_out, g_out) = refs
    i = pl.program_id(1)
    x = x_ref[0]
    prev = xp_ref[0][SUBLANES - 1:SUBLANES, :]
    prev = jnp.where(i == 0, 0.0, prev)
    row = lax.broadcasted_iota(jnp.int32, x.shape, 0)
    x_prev = jnp.where(row == 0, prev, pltpu.roll(x, 1, axis=0))
    xx = x_prev - x

    def xm(j):
        return x + xx * mix_ref[j:j + 1, :]

    xv = xm(2)
    r_out[0] = _dot(xm(0), wrkv_ref[0])
    k_out[0] = _dot(xm(1), wrkv_ref[1])
    v = _dot(xv, wrkv_ref[2])
    wl = vec_ref[0:1, :] + _dot(jnp.tanh(_dot(xm(3), w1_ref[...])), w2_ref[...])
    lw_out[0] = -jax.nn.sigmoid(wl) * math.exp(-0.5)
    a_out[0] = jax.nn.sigmoid(vec_ref[1:2, :] + _dot(_dot(xm(4), a1_ref[...]), a2_ref[...]))
    g_out[0] = _dot(jax.nn.sigmoid(_dot(xm(5), g1_ref[...])), g2_ref[...])
    if has_vres:
        mixv = jax.nn.sigmoid(vec_ref[2:3, :] + _dot(_dot(xv, v1_ref[...]), v2_ref[...]))
        v = v + (vfirst_ref[0] - v) * mixv
    v_out[0] = v


def _rwkv_proj(x, mix, w_rkv, w1, w2, a1, a2, g1, g2, vecs, vres):
    B, S, D = x.shape
    tm = _row_tile(S, 512)
    has_vres = vres is not None
    full = lambda a: pl.BlockSpec(a.shape, lambda b, i: (0,) * a.ndim)
    tile = pl.BlockSpec((1, tm, D), lambda b, i: (b, i, 0))
    prev = pl.BlockSpec((1, SUBLANES, D),
                        lambda b, i: (b, jnp.maximum(i * (tm // SUBLANES) - 1, 0), 0))
    ins = [x, x, mix, w_rkv, w1, w2, a1, a2, g1, g2, vecs]
    specs = [tile, prev] + [full(a) for a in ins[2:]]
    if has_vres:
        v1, v2, v_first = vres
        ins += [v1, v2, v_first]
        specs += [full(v1), full(v2), tile]
    out = jax.ShapeDtypeStruct((B, S, D), F32)
    return pl.pallas_call(
        functools.partial(_rwkv_proj_kernel, has_vres),
        grid=(B, S // tm),
        in_specs=specs,
        out_specs=[tile] * 6,
        out_shape=[out] * 6,
        compiler_params=_params("parallel", "arbitrary"),
    )(*ins)


def _wkv_kernel(r_ref, lw_ref, k_ref, v_ref, a_ref, hp_ref, y_out,
                state, avec_s, bvec_s, kmod_s, y_s):
    C = WKV_CHUNK
    ts = r_ref.shape[1]
    n_chunks = ts // C
    R2 = HEADS_PER_TILE * C

    @pl.when(pl.program_id(2) == 0)
    def _():
        state[...] = jnp.zeros_like(state)

    li = lax.broadcasted_iota(jnp.int32, (LANES, LANES), 0)
    lj = lax.broadcasted_iota(jnp.int32, (LANES, LANES), 1)
    head_ones = (li // RWKV_HEAD_DIM == lj // RWKV_HEAD_DIM).astype(BF16)

    k_k, k_a, r_k = hp_ref[0:1, :], hp_ref[1:2, :], hp_ref[2:3, :]
    lnx_g, lnx_b = hp_ref[3:4, :], hp_ref[4:5, :]

    k = k_ref[0]
    a = a_ref[0]
    kk = k * k_k
    n2 = _dot_exact_rhs(kk * kk, head_ones)
    kk = kk / jnp.maximum(jnp.sqrt(n2), L2_EPS)
    kmod = k * (1.0 + (a - 1.0) * k_a)
    avec_s[...] = -kk
    bvec_s[...] = kk * a
    kmod_s[...] = kmod

    ri = lax.broadcasted_iota(jnp.int32, (R2, R2), 0)
    rj = lax.broadcasted_iota(jnp.int32, (R2, R2), 1)
    same_head = (ri // C) == (rj // C)
    strict = same_head & ((ri % C) > (rj % C))
    incl = same_head & ((ri % C) >= (rj % C))
    ci = lax.broadcasted_iota(jnp.int32, (C, C), 0)
    cj = lax.broadcasted_iota(jnp.int32, (C, C), 1)
    tri = (ci >= cj).astype(BF16)
    lane = lax.broadcasted_iota(jnp.int32, (C, LANES), 1)
    head0 = lane < RWKV_HEAD_DIM

    def stack(t):
        return jnp.concatenate([jnp.where(head0, t, 0.0), jnp.where(head0, 0.0, t)], axis=0)

    def chunk(c, carry):
        sl = pl.ds(pl.multiple_of(c * C, C), C)
        lw = lw_ref[0, sl, :]
        cum = _dot_exact_lhs(tri, lw)
        cum_ex = cum - lw
        tot = cum[C - 1:C, :]
        e_in = jnp.exp(cum)
        e_neg = jnp.exp(-cum)
        e_end = jnp.exp(tot - cum)
        av, bv, km = avec_s[sl, :], bvec_s[sl, :], kmod_s[sl, :]
        vv = v_ref[0, sl, :]
        lhs = jnp.concatenate([stack(av * jnp.exp(cum_ex)), stack(r_ref[0, sl, :] * e_in)], axis=0)
        rhs = jnp.concatenate([stack(bv * e_neg), stack(km * e_neg)], axis=0)
        g = _dot_nt(lhs, rhs)
        s0 = state[...]
        a_s = _dot_nt(lhs, s0)
        vm = stack(vv)
        low = jnp.where(strict, g[:R2, :R2], 0.0)
        u = a_s[:R2] + _dot(jnp.where(strict, g[:R2, R2:], 0.0), vm)
        p = low
        n_doublings = int(math.log2(C))
        for it in range(n_doublings):
            u = u + _dot(p, u)
            if it + 1 < n_doublings:
                p = _dot(p, p)
        uv = jnp.concatenate([u, vm], axis=0)
        w_y = jnp.concatenate([jnp.where(incl, g[R2:, :R2], 0.0),
                               jnp.where(incl, g[R2:, R2:], 0.0)], axis=1)
        y_sm = a_s[R2:] + _dot(w_y, uv)
        y_s[sl, :] = y_sm[:C] + y_sm[C:]
        bk_end = jnp.concatenate([stack(bv * e_end), stack(km * e_end)], axis=0)
        state[...] = s0 * jnp.exp(tot) + _dot(uv.T, bk_end)
        return carry

    lax.fori_loop(0, n_chunks, chunk, 0)

    y = y_s[...]
    inv_n = 1.0 / RWKV_HEAD_DIM
    mu = _dot_exact_rhs(y, head_ones) * inv_n
    d = y - mu
    var = _dot_exact_rhs(d * d, head_ones) * inv_n
    yn = d * lax.rsqrt(var + RWKV_GN_EPS) * lnx_g + lnx_b
    bonus = _dot_exact_rhs(r_ref[0] * kmod * r_k, head_ones) * v_ref[0]
    y_out[0] = yn + bonus


def _wkv(r, lw, k, v, a, head_params):
    B, S, D = r.shape
    ts = _row_tile(S, 512)
    assert ts % WKV_CHUNK == 0
    tile = pl.BlockSpec((1, ts, LANES), lambda b, p, i: (b, i, p))
    hp = pl.BlockSpec((SUBLANES, LANES), lambda b, p, i: (0, p))
    return pl.pallas_call(
        _wkv_kernel,
        grid=(B, D // LANES, S // ts),
        in_specs=[tile] * 5 + [hp],
        out_specs=tile,
        out_shape=jax.ShapeDtypeStruct((B, S, D), F32),
        scratch_shapes=[pltpu.VMEM((LANES, LANES), F32)] + [pltpu.VMEM((ts, LANES), F32)] * 4,
        compiler_params=_params("parallel", "parallel", "arbitrary"),
    )(r, lw, k, v, a, head_params)


def _matmul_kernel(a_ref, w_ref, o_ref):
    o_ref[...] = _dot(a_ref[...], w_ref[...])


def _matmul(a, w):
    T, K = a.shape
    N = w.shape[1]
    tm = _row_tile(T, 512)
    return pl.pallas_call(
        _matmul_kernel,
        grid=(T // tm,),
        in_specs=[pl.BlockSpec((tm, K), lambda i: (i, 0)), pl.BlockSpec((K, N), lambda i: (0, 0))],
        out_specs=pl.BlockSpec((tm, N), lambda i: (i, 0)),
        out_shape=jax.ShapeDtypeStruct((T, N), F32),
        compiler_params=_params("parallel"),
    )(a, w)


def _proj_ln_kernel(has_gate, *refs):
    if has_gate:
        a_ref, g_ref, w_ref, x_ref, ln_ref, o_ref = refs
        a = a_ref[...] * g_ref[...]
    else:
        a_ref, w_ref, x_ref, ln_ref, o_ref = refs
        a = a_ref[...]
    z = DEEPNORM_ALPHA * x_ref[...] + _dot(a, w_ref[...])
    o_ref[...] = _layer_norm(z, ln_ref[0:1, :], ln_ref[1:2, :])


def _proj_ln(a, gate, w, x, ln):
    T, D = x.shape
    tm = _row_tile(T, 512)
    tile = pl.BlockSpec((tm, D), lambda i: (i, 0))
    full = lambda t: pl.BlockSpec(t.shape, lambda i: (0, 0))
    has_gate = gate is not None
    ins = [a] + ([gate] if has_gate else []) + [w, x, ln]
    specs = [tile] + ([tile] if has_gate else []) + [full(w), tile, full(ln)]
    return pl.pallas_call(
        functools.partial(_proj_ln_kernel, has_gate),
        grid=(T // tm,),
        in_specs=specs,
        out_specs=tile,
        out_shape=jax.ShapeDtypeStruct((T, D), F32),
        compiler_params=_params("parallel"),
    )(*ins)


def _attn_kernel(out_scale, tk, q_ref, k_ref, v_ref, slope_ref, lam_ref, g_ref, o_ref):
    i = pl.program_id(2)
    tq = q_ref.shape[1]
    q = q_ref[0] * (DA_HEAD_DIM ** -0.5)
    lane = lax.broadcasted_iota(jnp.int32, q.shape, 1)
    q0 = jnp.where(lane < DA_HEAD_DIM, q, 0.0).astype(BF16)
    q1 = jnp.where(lane < DA_HEAD_DIM, 0.0, q).astype(BF16)
    slope = slope_ref[0][:, 0:1]
    row = lax.broadcasted_iota(jnp.int32, (tq, tk), 0)
    col = lax.broadcasted_iota(jnp.int32, (tq, tk), 1)
    rel = (row - col).astype(F32)

    def body(j, carry):
        sl = pl.ds(pl.multiple_of(j * tk, tk), tk)
        kb = k_ref[0, sl, :].astype(BF16)
        vb = v_ref[0, sl, :].astype(BF16)
        dist = rel + (i * tq - j * tk).astype(F32)
        bias = jnp.where(dist >= 0, -slope * dist, NEG_BIG)
        new = []
        for qc, (m, l, acc) in zip((q0, q1), carry):
            s = lax.dot_general(qc, kb, (((1,), (1,)), ((), ())), preferred_element_type=F32) + bias
            m_new = jnp.maximum(m, jnp.max(s, axis=-1, keepdims=True))
            p = jnp.exp(s - m_new)
            alpha = jnp.exp(m - m_new)
            l = alpha * l + jnp.sum(p, axis=-1, keepdims=True)
            acc = alpha * acc + jnp.dot(p.astype(BF16), vb, preferred_element_type=F32)
            new.append((m_new, l, acc))
        return tuple(new)

    init = tuple((jnp.full((tq, 1), NEG_BIG, F32), jnp.zeros((tq, 1), F32),
                  jnp.zeros((tq, LANES), F32)) for _ in range(2))
    n_kv = (i * tq + tq + tk - 1) // tk
    (_, l0, acc0), (_, l1, acc1) = lax.fori_loop(0, n_kv, body, init)
    o = acc0 / l0 - lam_ref[...] * (acc1 / l1)
    o = o * lax.rsqrt(jnp.mean(o * o, axis=-1, keepdims=True) + SUBLN_EPS) * g_ref[...]
    o_ref[0] = o * out_scale


def _diff_attention(q, kv, slopes, lam_row, subln_g, lambda_init):
    B, S, D = q.shape
    tq = _row_tile(S, 256)
    H = DA_HEADS
    return pl.pallas_call(
        functools.partial(_attn_kernel, 1.0 - lambda_init, tq),
        grid=(B, H, S // tq),
        in_specs=[pl.BlockSpec((1, tq, LANES), lambda b, h, i: (b, i, h)),
                  pl.BlockSpec((1, S, LANES), lambda b, h, i: (b, 0, h)),
                  pl.BlockSpec((1, S, LANES), lambda b, h, i: (b, 0, H + h)),
                  pl.BlockSpec((1, 1, LANES), lambda b, h, i: (h, 0, 0)),
                  pl.BlockSpec((1, LANES), lambda b, h, i: (0, 0)),
                  pl.BlockSpec((1, LANES), lambda b, h, i: (0, 0))],
        out_specs=pl.BlockSpec((1, tq, LANES), lambda b, h, i: (b, i, h)),
        out_shape=jax.ShapeDtypeStruct((B, S, D), F32),
        compiler_params=_params("parallel", "parallel", "arbitrary"),
    )(q, kv, kv, slopes, lam_row, subln_g)


def _router_kernel(x_ref, w_ref, b_ref, idx_out, gate_out):
    x = x_ref[...]
    xh, xm_, xl = _split3(x)
    w = w_ref[...]
    f = lambda p, q: jnp.dot(p, q, preferred_element_type=F32)
    logits = (f(xh, w[0]) + (f(xh, w[1]) + f(xm_, w[0]))
              + (f(xh, w[2]) + f(xm_, w[1]) + f(xl, w[0]))) + b_ref[...]
    lane = lax.broadcasted_iota(jnp.int32, logits.shape, 1)
    vals, idxs = [], []
    cur = logits
    for _ in range(TOP_K):
        m = jnp.max(cur, axis=-1, keepdims=True)
        idx = jnp.min(jnp.where(cur == m, lane, LANES), axis=-1, keepdims=True)
        cur = jnp.where(lane == idx, -jnp.inf, cur)
        vals.append(m)
        idxs.append(idx)
    es = [jnp.exp(v - vals[0]) for v in vals]
    denom = es[0] + es[1] + es[2] + es[3]
    idx_row = jnp.zeros(logits.shape, jnp.int32)
    gate_row = jnp.zeros(logits.shape, F32)
    for kx in range(TOP_K):
        idx_row = jnp.where(lane == kx, idxs[kx], idx_row)
        gate_row = jnp.where(lane == kx, es[kx] / denom, gate_row)
    idx_out[...] = idx_row
    gate_out[...] = gate_row


def _router(x, w3, b_row):
    T, D = x.shape
    tm = _row_tile(T, 512)
    tile = pl.BlockSpec((tm, D), lambda i: (i, 0))
    out_tile = pl.BlockSpec((tm, LANES), lambda i: (i, 0))
    return pl.pallas_call(
        _router_kernel,
        grid=(T // tm,),
        in_specs=[tile, pl.BlockSpec(w3.shape, lambda i: (0, 0, 0)),
                  pl.BlockSpec(b_row.shape, lambda i: (0, 0))],
        out_specs=[out_tile, out_tile],
        out_shape=[jax.ShapeDtypeStruct((T, LANES), jnp.int32),
                   jax.ShapeDtypeStruct((T, LANES), F32)],
        compiler_params=_params("parallel"),
    )(x, w3, b_row)


def _clamped_swiglu(h):
    gate, up = h[:, :EXPERT_FF], h[:, EXPERT_FF:]
    gate = jnp.minimum(gate, SWIGLU_LIMIT)
    up = jnp.clip(up, -SWIGLU_LIMIT, SWIGLU_LIMIT)
    return (up + 1.0) * (gate * jax.nn.sigmoid(SWIGLU_ALPHA * gate))


def _moe_kernel(n_tokens, bexp_ref, nused_ref, idx_ref, x_hbm, wgu_ref, bgu_ref, wdn_ref, bdn_ref,
                y_hbm, xbuf, ybuf, sem_in, sem_out):
    del bexp_ref
    i = pl.program_id(0)
    n_pairs = n_tokens * TOP_K

    def row_in(j):
        tok = jnp.minimum(idx_ref[0, 0, j] // TOP_K, n_tokens - 1)
        return pltpu.make_async_copy(x_hbm.at[tok], xbuf.at[j], sem_in)

    def row_out(j):
        return pltpu.make_async_copy(ybuf.at[j], y_hbm.at[idx_ref[0, 0, j]], sem_out)

    @pl.when(i < nused_ref[0])
    def _():
        def issue(j, c):
            row_in(j).start()
            return c
        lax.fori_loop(0, EXPERT_BLOCK, issue, 0)

        def drain(j, c):
            row_in(j).wait()
            return c
        lax.fori_loop(0, EXPERT_BLOCK, drain, 0)

        h = _dot(xbuf[...], wgu_ref[0]) + bgu_ref[0]
        ybuf[...] = _dot(_clamped_swiglu(h), wdn_ref[0]) + bdn_ref[0]

        def scatter(j, c):
            @pl.when(idx_ref[0, 0, j] < n_pairs)
            def _():
                row_out(j).start()
            return c
        lax.fori_loop(0, EXPERT_BLOCK, scatter, 0)

        def drain_out(j, c):
            @pl.when(idx_ref[0, 0, j] < n_pairs)
            def _():
                row_out(j).wait()
            return c
        lax.fori_loop(0, EXPERT_BLOCK, drain_out, 0)


def _moe_experts(x, row_flat, block_exp, n_used, w_gu, b_gu, w_dn, b_dn):
    T, D = x.shape
    n_blocks = block_exp.shape[0]
    FF2 = w_gu.shape[2]
    grid_spec = pltpu.PrefetchScalarGridSpec(
        num_scalar_prefetch=2,
        grid=(n_blocks,),
        in_specs=[
            pl.BlockSpec((1, 1, EXPERT_BLOCK), lambda i, be, nu: (i, 0, 0), memory_space=pltpu.SMEM),
            pl.BlockSpec(memory_space=pl.ANY),
            pl.BlockSpec((1, D, FF2), lambda i, be, nu: (be[i], 0, 0)),
            pl.BlockSpec((1, 1, FF2), lambda i, be, nu: (be[i], 0, 0)),
            pl.BlockSpec((1, EXPERT_FF, D), lambda i, be, nu: (be[i], 0, 0)),
            pl.BlockSpec((1, 1, D), lambda i, be, nu: (be[i], 0, 0)),
        ],
        out_specs=pl.BlockSpec(memory_space=pl.ANY),
        scratch_shapes=[pltpu.VMEM((EXPERT_BLOCK, D), F32), pltpu.VMEM((EXPERT_BLOCK, D), F32),
                        pltpu.SemaphoreType.DMA(()), pltpu.SemaphoreType.DMA(())],
    )
    return pl.pallas_call(
        functools.partial(_moe_kernel, T),
        grid_spec=grid_spec,
        out_shape=jax.ShapeDtypeStruct((T * TOP_K, D), F32),
        compiler_params=_params("arbitrary"),
    )(block_exp, n_used, row_flat.reshape(n_blocks, 1, EXPERT_BLOCK), x, w_gu,
      b_gu.reshape(N_EXPERTS, 1, FF2), w_dn, b_dn.reshape(N_EXPERTS, 1, D))


def _combine_ln_kernel(y_ref, gate_ref, x_ref, ln_ref, o_ref):
    D = x_ref.shape[1]
    gate = gate_ref[...]
    ffn = jnp.zeros(x_ref.shape, F32)
    for kx in range(TOP_K):
        ffn = ffn + y_ref[:, kx * D:(kx + 1) * D] * gate[:, kx:kx + 1]
    z = DEEPNORM_ALPHA * x_ref[...] + ffn
    o_ref[...] = _layer_norm(z, ln_ref[0:1, :], ln_ref[1:2, :])


def _combine_ln(y_pairs, gate_rows, x, ln):
    T, D = x.shape
    tm = _row_tile(T, 512)
    tile = pl.BlockSpec((tm, D), lambda i: (i, 0))
    return pl.pallas_call(
        _combine_ln_kernel,
        grid=(T // tm,),
        in_specs=[pl.BlockSpec((tm, TOP_K * D), lambda i: (i, 0)),
                  pl.BlockSpec((tm, LANES), lambda i: (i, 0)), tile,
                  pl.BlockSpec(ln.shape, lambda i: (0, 0))],
        out_specs=tile,
        out_shape=jax.ShapeDtypeStruct((T, D), F32),
        compiler_params=_params("parallel"),
    )(y_pairs.reshape(T, TOP_K * D), gate_rows, x, ln)


def _routing_plan(top_idx):
    T = top_idx.shape[0]
    TK = T * TOP_K
    flat_e = top_idx.reshape(TK)
    order = jnp.argsort(flat_e).astype(jnp.int32)
    counts = jnp.bincount(flat_e, length=N_EXPERTS).astype(jnp.int32)
    padded = (counts + EXPERT_BLOCK - 1) // EXPERT_BLOCK * EXPERT_BLOCK
    pad_end = jnp.cumsum(padded)
    pad_start = pad_end - padded
    grp_start = jnp.cumsum(counts) - counts
    n_blocks = -(-TK // EXPERT_BLOCK) + N_EXPERTS
    block_start = jnp.arange(n_blocks, dtype=jnp.int32) * EXPERT_BLOCK
    block_exp = jnp.minimum(jnp.searchsorted(pad_end, block_start, side='right'),
                            N_EXPERTS - 1).astype(jnp.int32)
    pos = jnp.arange(n_blocks * EXPERT_BLOCK, dtype=jnp.int32)
    e_of = jnp.repeat(block_exp, EXPERT_BLOCK)
    off = pos - pad_start[e_of]
    valid = (off >= 0) & (off < counts[e_of])
    src = jnp.clip(grp_start[e_of] + off, 0, TK - 1)
    row_flat = jnp.where(valid, order[src], TK).astype(jnp.int32)
    n_used = (pad_end[-1] // EXPERT_BLOCK).astype(jnp.int32).reshape(1)
    return row_flat, block_exp, n_used


def _moe_ffn_ln(x, router_w, router_b, w_gu, b_gu, w_dn, b_dn, ln):
    T, D = x.shape
    w_pad = jnp.zeros((D, LANES), F32).at[:, :N_EXPERTS].set(router_w)
    w3 = jnp.stack(_split3(w_pad))
    b_row = jnp.full((1, LANES), -jnp.inf, F32).at[0, :N_EXPERTS].set(router_b)
    idx_rows, gate_rows = _router(x, w3, b_row)
    row_flat, block_exp, n_used = _routing_plan(idx_rows[:, :TOP_K])
    y_pairs = _moe_experts(x, row_flat, block_exp, n_used,
                           w_gu.astype(BF16), b_gu, w_dn.astype(BF16), b_dn)
    return _combine_ln(y_pairs, gate_rows, x, ln)


def _pad_rows(rows, n=SUBLANES):
    out = jnp.zeros((n, rows[0].shape[-1]), F32)
    for j, r in enumerate(rows):
        out = out.at[j].set(r.reshape(-1))
    return out


def kernel(x, ln_g, ln_b, rwkv_mix, rwkv_w_rkv, rwkv_w_o, rwkv_w0, rwkv_w1, rwkv_w2, rwkv_a0, rwkv_a1, rwkv_a2, rwkv_g1, rwkv_g2, rwkv_k_k, rwkv_k_a, rwkv_r_k, rwkv_lnx_g, rwkv_lnx_b, rwkv_v0, rwkv_v1, rwkv_v2, kv_w, da_w_q, da_w_o, da_lambda, da_subln_g, moe_router_w, moe_router_b, moe_w_gu, moe_b_gu, moe_w_dn, moe_b_dn):
    B, S, D = x.shape
    T = B * S
    v_first = None
    kv = None
    slopes = 2.0 ** (-8.0 * jnp.arange(1, DA_HEADS + 1, dtype=F32) / DA_HEADS)
    slopes = jnp.broadcast_to(slopes[:, None, None], (DA_HEADS, 1, LANES))
    for l in range(DEPTH):
        ln_mix = jnp.stack([ln_g[l, 0], ln_b[l, 0]])
        ln_ffn = jnp.stack([ln_g[l, 1], ln_b[l, 1]])
        xt = x.reshape(T, D)
        if l < N_A_LAYERS:
            zero = jnp.zeros((D,), F32)
            vecs = _pad_rows([rwkv_w0[l], rwkv_a0[l], rwkv_v0[l - 1] if l > 0 else zero])
            vres = None if l == 0 else (rwkv_v1[l - 1].astype(BF16), rwkv_v2[l - 1].astype(BF16), v_first)
            r, lw, k, v, a, g = _rwkv_proj(
                x, _pad_rows(list(rwkv_mix[l])), rwkv_w_rkv[l].astype(BF16),
                rwkv_w1[l].astype(BF16), rwkv_w2[l].astype(BF16), rwkv_a1[l].astype(BF16),
                rwkv_a2[l].astype(BF16), rwkv_g1[l].astype(BF16), rwkv_g2[l].astype(BF16), vecs, vres)
            if l == 0:
                v_first = v
            head_params = _pad_rows([rwkv_k_k[l], rwkv_k_a[l],
                                     jnp.tile(rwkv_r_k[l].reshape(-1), 1), rwkv_lnx_g[l], rwkv_lnx_b[l]])
            y = _wkv(r, lw, k, v, a, head_params)
            xt = _proj_ln(y.reshape(T, D), g.reshape(T, D), rwkv_w_o[l].astype(BF16), xt, ln_mix)
        else:
            if l == N_A_LAYERS:
                kv = _matmul(xt, kv_w.astype(BF16)).reshape(B, S, 2 * D)
            j = l - N_A_LAYERS
            lambda_init = 0.8 - 0.6 * math.exp(-0.3 * l)
            lam = da_lambda[j].astype(F32)
            lam_full = (jnp.exp(jnp.sum(lam[0] * lam[1])) - jnp.exp(jnp.sum(lam[2] * lam[3]))
                        + lambda_init)
            q = _matmul(xt, da_w_q[j].astype(BF16)).reshape(B, S, D)
            o = _diff_attention(q, kv, slopes, jnp.full((1, LANES), lam_full, F32),
                                da_subln_g[j].reshape(1, LANES), lambda_init)
            xt = _proj_ln(o.reshape(T, D), None, da_w_o[j].astype(BF16), xt, ln_mix)
        xt = _moe_ffn_ln(xt, moe_router_w[l], moe_router_b[l], moe_w_gu[l], moe_b_gu[l],
                         moe_w_dn[l], moe_b_dn[l], ln_ffn)
        x = xt.reshape(B, S, D)
    return x
```

```python
import functools
import math

import jax
import jax.numpy as jnp
from jax import lax
from jax.experimental import pallas as pl
from jax.experimental.pallas import tpu as pltpu

F32 = jnp.float32
BF16 = jnp.bfloat16

D_MODEL = 1024
DEPTH = 4
N_A_LAYERS = DEPTH // 2
RWKV_HEAD_DIM = 64
RWKV_GN_EPS = 64e-5
L2_EPS = 1e-12
DA_HEAD_DIM = 64
DA_HEADS = D_MODEL // (2 * DA_HEAD_DIM)
SUBLN_EPS = 1e-5
N_EXPERTS = 32
TOP_K = 4
EXPERT_FF = D_MODEL
SWIGLU_LIMIT = 7.0
SWIGLU_ALPHA = 1.702
EXPERT_BLOCK = 512
DEEPNORM_ALPHA = (2 * DEPTH) ** 0.25
LN_EPS = 1e-5

LANES = 128
SUBLANES = 8
VMEM_LIMIT_BYTES = 56 * 1024 * 1024

WKV_CHUNK = 64
WKV_TILES_PER_STEP = 8
HEADS_PER_TILE = LANES // RWKV_HEAD_DIM
NEG_BIG = -1e30


def _params(*sem):
    return pltpu.CompilerParams(dimension_semantics=sem, vmem_limit_bytes=VMEM_LIMIT_BYTES)


def _dot(a, b):
    return jnp.dot(a.astype(BF16), b.astype(BF16), preferred_element_type=F32)


def _dot_nt(a, b):
    return lax.dot_general(a.astype(BF16), b.astype(BF16), (((1,), (1,)), ((), ())),
                           preferred_element_type=F32)


def _split3(x):
    hi = x.astype(BF16)
    r1 = x - hi.astype(F32)
    mid = r1.astype(BF16)
    lo = (r1 - mid.astype(F32)).astype(BF16)
    return hi, mid, lo


def _split2(x):
    hi = x.astype(BF16)
    return hi, (x - hi.astype(F32)).astype(BF16)


def _dot_exact_rhs(x, m_bf16):
    hi, lo = _split2(x)
    return (jnp.dot(hi, m_bf16, preferred_element_type=F32)
            + jnp.dot(lo, m_bf16, preferred_element_type=F32))


def _dot_exact_lhs(m_bf16, x):
    hi, lo = _split2(x)
    return (jnp.dot(m_bf16, hi, preferred_element_type=F32)
            + jnp.dot(m_bf16, lo, preferred_element_type=F32))


def _layer_norm(z, g, b):
    mu = jnp.mean(z, axis=-1, keepdims=True)
    d = z - mu
    var = jnp.mean(d * d, axis=-1, keepdims=True)
    return d * lax.rsqrt(var + LN_EPS) * g + b


def _row_tile(n, want):
    t = min(want, n)
    assert n % t == 0 and t % SUBLANES == 0
    return t


def _rwkv_proj_kernel(has_vres, *refs):
    if has_vres:
        (x_ref, xp_ref, mix_ref, wrkv_ref, w1_ref, w2_ref, a1_ref, a2_ref, g1_ref, g2_ref,
         vec_ref, v1_ref, v2_ref, vfirst_ref,
         r_out, lw_out, k_out, v_out, a_out, g_out) = refs
    else:
        (x_ref, xp_ref, mix_ref, wrkv_ref, w1_ref, w2_ref, a1_ref, a2_ref, g1_ref, g2_ref,
         vec_ref, r_out, lw_out, k_out, v_out, a_out, g_out) = refs
    i = pl.program_id(1)
    x = x_ref[0]
    prev = xp_ref[0][SUBLANES - 1:SUBLANES, :]
    prev = jnp.where(i == 0, 0.0, prev)
    row = lax.broadcasted_iota(jnp.int32, x.shape, 0)
    x_prev = jnp.where(row == 0, prev, pltpu.roll(x, 1, axis=0))
    xx = x_prev - x

    def xm(j):
        return x + xx * mix_ref[j:j + 1, :]

    xv = xm(2)
    r_out[0] = _dot(xm(0), wrkv_ref[0])
    k_out[0] = _dot(xm(1), wrkv_ref[1])
    v = _dot(xv, wrkv_ref[2])
    wl = vec_ref[0:1, :] + _dot(jnp.tanh(_dot(xm(3), w1_ref[...])), w2_ref[...])
    lw_out[0] = -jax.nn.sigmoid(wl) * math.exp(-0.5)
    a_out[0] = jax.nn.sigmoid(vec_ref[1:2, :] + _dot(_dot(xm(4), a1_ref[...]), a2_ref[...]))
    g_out[0] = _dot(jax.nn.sigmoid(_dot(xm(5), g1_ref[...])), g2_ref[...])
    if has_vres:
        mixv = jax.nn.sigmoid(vec_ref[2:3, :] + _dot(_dot(xv, v1_ref[...]), v2_ref[...]))
        v = v + (vfirst_ref[0] - v) * mixv
    v_out[0] = v


def _rwkv_proj(x, mix, w_rkv, w1, w2, a1, a2, g1, g2, vecs, vres):
    B, S, D = x.shape
    tm = _row_tile(S, 512)
    has_vres = vres is not None
    full = lambda a: pl.BlockSpec(a.shape, lambda b, i: (0,) * a.ndim)
    tile = pl.BlockSpec((1, tm, D), lambda b, i: (b, i, 0))
    prev = pl.BlockSpec((1, SUBLANES, D),
                        lambda b, i: (b, jnp.maximum(i * (tm // SUBLANES) - 1, 0), 0))
    ins = [x, x, mix, w_rkv, w1, w2, a1, a2, g1, g2, vecs]
    specs = [tile, prev] + [full(a) for a in ins[2:]]
    if has_vres:
        v1, v2, v_first = vres
        ins += [v1, v2, v_first]
        specs += [full(v1), full(v2), tile]
    out = jax.ShapeDtypeStruct((B, S, D), F32)
    return pl.pallas_call(
        functools.partial(_rwkv_proj_kernel, has_vres),
        grid=(B, S // tm),
        in_specs=specs,
        out_specs=[tile] * 6,
        out_shape=[out] * 6,
        compiler_params=_params("parallel", "arbitrary"),
    )(*ins)


def _wkv_kernel(r_ref, lw_ref, k_ref, v_ref, a_ref, hp_ref, y_out,
                state, avec_s, bvec_s, kmod_s, y_s):
    C = WKV_CHUNK
    ts = r_ref.shape[1]
    n_tiles = r_ref.shape[2] // LANES
    n_chunks = ts // C
    R2 = HEADS_PER_TILE * C
    inv_n = 1.0 / RWKV_HEAD_DIM

    @pl.when(pl.program_id(2) == 0)
    def _():
        state[...] = jnp.zeros_like(state)

    li = lax.broadcasted_iota(jnp.int32, (LANES, LANES), 0)
    lj = lax.broadcasted_iota(jnp.int32, (LANES, LANES), 1)
    head_ones = (li // RWKV_HEAD_DIM == lj // RWKV_HEAD_DIM).astype(BF16)

    for t in range(n_tiles):
        ls = slice(t * LANES, (t + 1) * LANES)
        k = k_ref[0, :, ls]
        a = a_ref[0, :, ls]
        kk = k * hp_ref[0:1, ls]
        n2 = _dot_exact_rhs(kk * kk, head_ones)
        kk = kk / jnp.maximum(jnp.sqrt(n2), L2_EPS)
        avec_s[:, ls] = -kk
        bvec_s[:, ls] = kk * a
        kmod_s[:, ls] = k * (1.0 + (a - 1.0) * hp_ref[1:2, ls])

    ri = lax.broadcasted_iota(jnp.int32, (R2, R2), 0)
    rj = lax.broadcasted_iota(jnp.int32, (R2, R2), 1)
    same_head = (ri // C) == (rj // C)
    strict = same_head & ((ri % C) > (rj % C))
    incl = same_head & ((ri % C) >= (rj % C))
    ci = lax.broadcasted_iota(jnp.int32, (C, C), 0)
    cj = lax.broadcasted_iota(jnp.int32, (C, C), 1)
    tri = (ci >= cj).astype(BF16)
    lane = lax.broadcasted_iota(jnp.int32, (C, LANES), 1)
    head0 = lane < RWKV_HEAD_DIM
    n_doublings = int(math.log2(C))

    def stack(t):
        return jnp.concatenate([jnp.where(head0, t, 0.0), jnp.where(head0, 0.0, t)], axis=0)

    tiles = range(n_tiles)
    lanes_of = lambda t: slice(t * LANES, (t + 1) * LANES)

    def chunk(c, carry):
        sl = pl.ds(pl.multiple_of(c * C, C), C)
        each = lambda f: [f(t) for t in tiles]
        s0 = each(lambda t: state[t])
        lw = each(lambda t: lw_ref[0, sl, lanes_of(t)])
        cum = each(lambda t: _dot_exact_lhs(tri, lw[t]))
        tot = each(lambda t: cum[t][C - 1:C, :])
        e_neg = each(lambda t: jnp.exp(-cum[t]))
        av = each(lambda t: avec_s[sl, lanes_of(t)])
        bv = each(lambda t: bvec_s[sl, lanes_of(t)])
        km = each(lambda t: kmod_s[sl, lanes_of(t)])
        lhs = each(lambda t: jnp.concatenate(
            [stack(av[t] * jnp.exp(cum[t] - lw[t])),
             stack(r_ref[0, sl, lanes_of(t)] * jnp.exp(cum[t]))], axis=0))
        rhs = each(lambda t: jnp.concatenate([stack(bv[t] * e_neg[t]), stack(km[t] * e_neg[t])], axis=0))
        g = each(lambda t: _dot_nt(lhs[t], rhs[t]))
        a_s = each(lambda t: _dot_nt(lhs[t], s0[t]))
        vm = each(lambda t: stack(v_ref[0, sl, lanes_of(t)]))
        akv = each(lambda t: _dot(jnp.where(strict, g[t][:R2, R2:], 0.0), vm[t]))
        u = each(lambda t: a_s[t][:R2] + akv[t])
        p = each(lambda t: jnp.where(strict, g[t][:R2, :R2], 0.0))
        for it in range(n_doublings):
            pu = each(lambda t: _dot(p[t], u[t]))
            if it + 1 < n_doublings:
                p = each(lambda t: _dot(p[t], p[t]))
            u = each(lambda t: u[t] + pu[t])
        uv = each(lambda t: jnp.concatenate([u[t], vm[t]], axis=0))
        w_y = each(lambda t: jnp.concatenate([jnp.where(incl, g[t][R2:, :R2], 0.0),
                                              jnp.where(incl, g[t][R2:, R2:], 0.0)], axis=1))
        y_sm = each(lambda t: a_s[t][R2:] + _dot(w_y[t], uv[t]))
        bk_end = each(lambda t: jnp.concatenate(
            [stack(bv[t] * jnp.exp(tot[t] - cum[t])), stack(km[t] * jnp.exp(tot[t] - cum[t]))], axis=0))
        s1 = each(lambda t: s0[t] * jnp.exp(tot[t]) + _dot(uv[t].T, bk_end[t]))
        for t in tiles:
            y_s[sl, lanes_of(t)] = y_sm[t][:C] + y_sm[t][C:]
            state[t] = s1[t]
        return carry

    lax.fori_loop(0, n_chunks, chunk, 0)

    for t in range(n_tiles):
        ls = slice(t * LANES, (t + 1) * LANES)
        y = y_s[:, ls]
        mu = _dot_exact_rhs(y, head_ones) * inv_n
        d = y - mu
        var = _dot_exact_rhs(d * d, head_ones) * inv_n
        yn = d * lax.rsqrt(var + RWKV_GN_EPS) * hp_ref[3:4, ls] + hp_ref[4:5, ls]
        bonus = _dot_exact_rhs(r_ref[0, :, ls] * kmod_s[:, ls] * hp_ref[2:3, ls], head_ones)
        y_out[0, :, ls] = yn + bonus * v_ref[0, :, ls]


def _wkv(r, lw, k, v, a, head_params):
    B, S, D = r.shape
    ts = _row_tile(S, 512)
    assert ts % WKV_CHUNK == 0
    width = WKV_TILES_PER_STEP * LANES
    tile = pl.BlockSpec((1, ts, width), lambda b, p, i: (b, i, p))
    hp = pl.BlockSpec((SUBLANES, width), lambda b, p, i: (0, p))
    return pl.pallas_call(
        _wkv_kernel,
        grid=(B, D // width, S // ts),
        in_specs=[tile] * 5 + [hp],
        out_specs=tile,
        out_shape=jax.ShapeDtypeStruct((B, S, D), F32),
        scratch_shapes=[pltpu.VMEM((WKV_TILES_PER_STEP, LANES, LANES), F32)]
        + [pltpu.VMEM((ts, width), F32)] * 4,
        compiler_params=_params("parallel", "parallel", "arbitrary"),
    )(r, lw, k, v, a, head_params)


def _matmul_kernel(a_ref, w_ref, o_ref):
    o_ref[...] = _dot(a_ref[...], w_ref[...])


def _matmul(a, w):
    T, K = a.shape
    N = w.shape[1]
    tm = _row_tile(T, 512)
    return pl.pallas_call(
        _matmul_kernel,
        grid=(T // tm,),
        in_specs=[pl.BlockSpec((tm, K), lambda i: (i, 0)), pl.BlockSpec((K, N), lambda i: (0, 0))],
        out_specs=pl.BlockSpec((tm, N), lambda i: (i, 0)),
        out_shape=jax.ShapeDtypeStruct((T, N), F32),
        compiler_params=_params("parallel"),
    )(a, w)


def _proj_ln_kernel(has_gate, *refs):
    if has_gate:
        a_ref, g_ref, w_ref, x_ref, ln_ref, o_ref = refs
        a = a_ref[...] * g_ref[...]
    else:
        a_ref, w_ref, x_ref, ln_ref, o_ref = refs
        a = a_ref[...]
    z = DEEPNORM_ALPHA * x_ref[...] + _dot(a, w_ref[...])
    o_ref[...] = _layer_norm(z, ln_ref[0:1, :], ln_ref[1:2, :])


def _proj_ln(a, gate, w, x, ln):
    T, D = x.shape
    tm = _row_tile(T, 512)
    tile = pl.BlockSpec((tm, D), lambda i: (i, 0))
    full = lambda t: pl.BlockSpec(t.shape, lambda i: (0, 0))
    has_gate = gate is not None
    ins = [a] + ([gate] if has_gate else []) + [w, x, ln]
    specs = [tile] + ([tile] if has_gate else []) + [full(w), tile, full(ln)]
    return pl.pallas_call(
        functools.partial(_proj_ln_kernel, has_gate),
        grid=(T // tm,),
        in_specs=specs,
        out_specs=tile,
        out_shape=jax.ShapeDtypeStruct((T, D), F32),
        compiler_params=_params("parallel"),
    )(*ins)


def _attn_kernel(out_scale, tk, q_ref, k_ref, v_ref, slope_ref, lam_ref, g_ref, o_ref):
    i = pl.program_id(2)
    tq = q_ref.shape[1]
    q = q_ref[0] * (DA_HEAD_DIM ** -0.5)
    lane = lax.broadcasted_iota(jnp.int32, q.shape, 1)
    q0 = jnp.where(lane < DA_HEAD_DIM, q, 0.0).astype(BF16)
    q1 = jnp.where(lane < DA_HEAD_DIM, 0.0, q).astype(BF16)
    slope = slope_ref[0][:, 0:1]
    row = lax.broadcasted_iota(jnp.int32, (tq, tk), 0)
    col = lax.broadcasted_iota(jnp.int32, (tq, tk), 1)
    rel = (row - col).astype(F32)

    def body(j, carry):
        sl = pl.ds(pl.multiple_of(j * tk, tk), tk)
        kb = k_ref[0, sl, :].astype(BF16)
        vb = v_ref[0, sl, :].astype(BF16)
        dist = rel + (i * tq - j * tk).astype(F32)
        bias = jnp.where(dist >= 0, -slope * dist, NEG_BIG)
        scores = [lax.dot_general(qc, kb, (((1,), (1,)), ((), ())), preferred_element_type=F32)
                  for qc in (q0, q1)]
        new = []
        for s, (m, l, acc) in zip(scores, carry):
            s = s + bias
            m_new = jnp.maximum(m, jnp.max(s, axis=-1, keepdims=True))
            p = jnp.exp(s - m_new)
            alpha = jnp.exp(m - m_new)
            l = alpha * l + jnp.sum(p, axis=-1, keepdims=True)
            acc = alpha * acc + jnp.dot(p.astype(BF16), vb, preferred_element_type=F32)
            new.append((m_new, l, acc))
        return tuple(new)

    init = tuple((jnp.full((tq, 1), NEG_BIG, F32), jnp.zeros((tq, 1), F32),
                  jnp.zeros((tq, LANES), F32)) for _ in range(2))
    n_kv = (i * tq + tq + tk - 1) // tk
    (_, l0, acc0), (_, l1, acc1) = lax.fori_loop(0, n_kv, body, init)
    o = acc0 / l0 - lam_ref[...] * (acc1 / l1)
    o = o * lax.rsqrt(jnp.mean(o * o, axis=-1, keepdims=True) + SUBLN_EPS) * g_ref[...]
    o_ref[0] = o * out_scale


def _diff_attention(q, kv, slopes, lam_row, subln_g, lambda_init):
    B, S, D = q.shape
    tq = _row_tile(S, 256)
    H = DA_HEADS
    return pl.pallas_call(
        functools.partial(_attn_kernel, 1.0 - lambda_init, tq),
        grid=(B, H, S // tq),
        in_specs=[pl.BlockSpec((1, tq, LANES), lambda b, h, i: (b, i, h)),
                  pl.BlockSpec((1, S, LANES), lambda b, h, i: (b, 0, h)),
                  pl.BlockSpec((1, S, LANES), lambda b, h, i: (b, 0, H + h)),
                  pl.BlockSpec((1, 1, LANES), lambda b, h, i: (h, 0, 0)),
                  pl.BlockSpec((1, LANES), lambda b, h, i: (0, 0)),
                  pl.BlockSpec((1, LANES), lambda b, h, i: (0, 0))],
        out_specs=pl.BlockSpec((1, tq, LANES), lambda b, h, i: (b, i, h)),
        out_shape=jax.ShapeDtypeStruct((B, S, D), F32),
        compiler_params=_params("parallel", "parallel", "arbitrary"),
    )(q, kv, kv, slopes, lam_row, subln_g)


def _router_kernel(x_ref, w_ref, b_ref, idx_out, gate_out):
    x = x_ref[...]
    xh, xm_, xl = _split3(x)
    w = w_ref[...]
    f = lambda p, q: jnp.dot(p, q, preferred_element_type=F32)
    logits = (f(xh, w[0]) + (f(xh, w[1]) + f(xm_, w[0]))
              + (f(xh, w[2]) + f(xm_, w[1]) + f(xl, w[0]))) + b_ref[...]
    lane = lax.broadcasted_iota(jnp.int32, logits.shape, 1)
    vals, idxs = [], []
    cur = logits
    for _ in range(TOP_K):
        m = jnp.max(cur, axis=-1, keepdims=True)
        idx = jnp.min(jnp.where(cur == m, lane, LANES), axis=-1, keepdims=True)
        cur = jnp.where(lane == idx, -jnp.inf, cur)
        vals.append(m)
        idxs.append(idx)
    es = [jnp.exp(v - vals[0]) for v in vals]
    denom = es[0] + es[1] + es[2] + es[3]
    idx_row = jnp.zeros(logits.shape, jnp.int32)
    gate_row = jnp.zeros(logits.shape, F32)
    for kx in range(TOP_K):
        idx_row = jnp.where(lane == kx, idxs[kx], idx_row)
        gate_row = jnp.where(lane == kx, es[kx] / denom, gate_row)
    idx_out[...] = idx_row
    gate_out[...] = gate_row


def _router(x, w3, b_row):
    T, D = x.shape
    tm = _row_tile(T, 512)
    tile = pl.BlockSpec((tm, D), lambda i: (i, 0))
    out_tile = pl.BlockSpec((tm, LANES), lambda i: (i, 0))
    return pl.pallas_call(
        _router_kernel,
        grid=(T // tm,),
        in_specs=[tile, pl.BlockSpec(w3.shape, lambda i: (0, 0, 0)),
                  pl.BlockSpec(b_row.shape, lambda i: (0, 0))],
        out_specs=[out_tile, out_tile],
        out_shape=[jax.ShapeDtypeStruct((T, LANES), jnp.int32),
                   jax.ShapeDtypeStruct((T, LANES), F32)],
        compiler_params=_params("parallel"),
    )(x, w3, b_row)


def _clamped_swiglu(h):
    gate, up = h[:, :EXPERT_FF], h[:, EXPERT_FF:]
    gate = jnp.minimum(gate, SWIGLU_LIMIT)
    up = jnp.clip(up, -SWIGLU_LIMIT, SWIGLU_LIMIT)
    return (up + 1.0) * (gate * jax.nn.sigmoid(SWIGLU_ALPHA * gate))


def _moe_kernel(bexp_ref, g_first, g_b, g_next, s_prev, s_a, s_b, x_hbm,
                wgu_a, bgu_a, wdn_a, bdn_a, wgu_b, bgu_b, wdn_b, bdn_b, y_hbm,
                xbuf_a, xbuf_b, ybuf_a, ybuf_b, sem_ga, sem_gb, sem_sa, sem_sb):
    del bexp_ref
    g = pl.program_id(0)
    last = pl.num_programs(0) - 1

    def gather(idx_ref, xbuf, sem, j):
        return pltpu.make_async_copy(x_hbm.at[idx_ref[0, 0, j]], xbuf.at[j], sem)

    def scatter(idx_ref, ybuf, sem, j):
        return pltpu.make_async_copy(ybuf.at[j], y_hbm.at[idx_ref[0, 0, j]], sem)

    def wait_gather(xbuf, sem):
        pltpu.make_async_copy(x_hbm.at[pl.ds(0, EXPERT_BLOCK)], xbuf, sem).wait()

    def wait_scatter(ybuf, sem):
        pltpu.make_async_copy(ybuf, y_hbm.at[pl.ds(0, EXPERT_BLOCK)], sem).wait()

    def expert_mlp(xbuf, ybuf, wgu, bgu, wdn, bdn):
        h = _dot(xbuf[...], wgu[0]) + bgu[0]
        ybuf[...] = _dot(_clamped_swiglu(h), wdn[0]) + bdn[0]

    @pl.when(g == 0)
    def _():
        ybuf_b[...] = jnp.zeros_like(ybuf_b)

        def first(j, c):
            gather(g_first, xbuf_a, sem_ga, j).start()
            return c
        lax.fori_loop(0, EXPERT_BLOCK, first, 0)

    wait_gather(xbuf_a, sem_ga)

    @pl.when(g > 0)
    def _():
        wait_scatter(ybuf_a, sem_sa)

    for j in range(EXPERT_BLOCK):
        gather(g_b, xbuf_b, sem_gb, j).start()
        scatter(s_prev, ybuf_b, sem_sb, j).start()
    expert_mlp(xbuf_a, ybuf_a, wgu_a, bgu_a, wdn_a, bdn_a)

    wait_gather(xbuf_b, sem_gb)
    wait_scatter(ybuf_b, sem_sb)

    for j in range(EXPERT_BLOCK):
        gather(g_next, xbuf_a, sem_ga, j).start()
        scatter(s_a, ybuf_a, sem_sa, j).start()
    expert_mlp(xbuf_b, ybuf_b, wgu_b, bgu_b, wdn_b, bdn_b)

    @pl.when(g == last)
    def _():
        def final(j, c):
            scatter(s_b, ybuf_b, sem_sb, j).start()
            return c
        lax.fori_loop(0, EXPERT_BLOCK, final, 0)
        wait_gather(xbuf_a, sem_ga)
        wait_scatter(ybuf_a, sem_sa)
        wait_scatter(ybuf_b, sem_sb)


def _moe_experts(x, gather_tok, scatter_row, block_exp, n_out_rows, w_gu, b_gu, w_dn, b_dn):
    T, D = x.shape
    n_blocks = block_exp.shape[0]
    assert n_blocks % 2 == 0
    FF2 = w_gu.shape[2]
    idx_spec = lambda fn: pl.BlockSpec((1, 1, EXPERT_BLOCK), fn, memory_space=pltpu.SMEM)
    weights = lambda off: [
        pl.BlockSpec((1, D, FF2), lambda g, be: (be[2 * g + off], 0, 0)),
        pl.BlockSpec((1, 1, FF2), lambda g, be: (be[2 * g + off], 0, 0)),
        pl.BlockSpec((1, EXPERT_FF, D), lambda g, be: (be[2 * g + off], 0, 0)),
        pl.BlockSpec((1, 1, D), lambda g, be: (be[2 * g + off], 0, 0)),
    ]
    grid_spec = pltpu.PrefetchScalarGridSpec(
        num_scalar_prefetch=1,
        grid=(n_blocks // 2,),
        in_specs=[
            idx_spec(lambda g, be: (0, 0, 0)),
            idx_spec(lambda g, be: (2 * g + 1, 0, 0)),
            idx_spec(lambda g, be: (2 * g + 2, 0, 0)),
            idx_spec(lambda g, be: (2 * g, 0, 0)),
            idx_spec(lambda g, be: (2 * g + 1, 0, 0)),
            idx_spec(lambda g, be: (2 * g + 2, 0, 0)),
            pl.BlockSpec(memory_space=pl.ANY),
        ] + weights(0) + weights(1),
        out_specs=pl.BlockSpec(memory_space=pl.ANY),
        scratch_shapes=[pltpu.VMEM((EXPERT_BLOCK, D), F32)] * 4 + [pltpu.SemaphoreType.DMA(())] * 4,
    )
    b_gu = b_gu.reshape(N_EXPERTS, 1, FF2)
    b_dn = b_dn.reshape(N_EXPERTS, 1, D)
    return pl.pallas_call(
        _moe_kernel,
        grid_spec=grid_spec,
        out_shape=jax.ShapeDtypeStruct((n_out_rows, D), F32),
        compiler_params=_params("arbitrary"),
    )(block_exp, gather_tok, gather_tok, gather_tok, scatter_row, scatter_row, scatter_row, x,
      w_gu, b_gu, w_dn, b_dn, w_gu, b_gu, w_dn, b_dn)


def _combine_ln_kernel(y0_ref, y1_ref, y2_ref, y3_ref, gate_ref, x_ref, ln_ref, o_ref):
    gate = gate_ref[...]
    ffn = jnp.zeros(x_ref.shape, F32)
    for kx, y_ref in enumerate((y0_ref, y1_ref, y2_ref, y3_ref)):
        ffn = ffn + y_ref[...] * gate[:, kx:kx + 1]
    z = DEEPNORM_ALPHA * x_ref[...] + ffn
    o_ref[...] = _layer_norm(z, ln_ref[0:1, :], ln_ref[1:2, :])


def _combine_ln(y_rows, gate_rows, x, ln):
    T, D = x.shape
    tm = _row_tile(T, 512)
    tile = pl.BlockSpec((tm, D), lambda i: (i, 0))
    slot = lambda kx: pl.BlockSpec((tm, D), lambda i: (kx * (T // tm) + i, 0))
    return pl.pallas_call(
        _combine_ln_kernel,
        grid=(T // tm,),
        in_specs=[slot(kx) for kx in range(TOP_K)]
        + [pl.BlockSpec((tm, LANES), lambda i: (i, 0)), tile, pl.BlockSpec(ln.shape, lambda i: (0, 0))],
        out_specs=tile,
        out_shape=jax.ShapeDtypeStruct((T, D), F32),
        compiler_params=_params("parallel"),
    )(y_rows, y_rows, y_rows, y_rows, gate_rows, x, ln)


def _routing_plan(top_idx):
    T = top_idx.shape[0]
    TK = T * TOP_K
    flat_e = top_idx.reshape(TK)
    order = jnp.argsort(flat_e).astype(jnp.int32)
    counts = jnp.bincount(flat_e, length=N_EXPERTS).astype(jnp.int32)
    padded = (counts + EXPERT_BLOCK - 1) // EXPERT_BLOCK * EXPERT_BLOCK
    pad_end = jnp.cumsum(padded)
    pad_start = pad_end - padded
    grp_start = jnp.cumsum(counts) - counts
    n_blocks = -(-TK // EXPERT_BLOCK) + N_EXPERTS
    n_rows = n_blocks * EXPERT_BLOCK
    block_start = jnp.arange(n_blocks, dtype=jnp.int32) * EXPERT_BLOCK
    block_exp = jnp.minimum(jnp.searchsorted(pad_end, block_start, side='right'),
                            N_EXPERTS - 1).astype(jnp.int32)
    pos = jnp.arange(n_rows, dtype=jnp.int32)
    e_of = jnp.repeat(block_exp, EXPERT_BLOCK)
    off = pos - pad_start[e_of]
    valid = (off >= 0) & (off < counts[e_of])
    pair = order[jnp.clip(grp_start[e_of] + off, 0, TK - 1)]
    tok, slot = pair // TOP_K, pair % TOP_K
    spare = TK + jnp.cumsum(jnp.logical_not(valid).astype(jnp.int32)) - 1
    gather_tok = jnp.where(valid, tok, 0).astype(jnp.int32)
    scatter_row = jnp.where(valid, slot * T + tok, spare).astype(jnp.int32)
    fill = jnp.arange(EXPERT_BLOCK, dtype=jnp.int32)
    gather_tok = jnp.concatenate([gather_tok, jnp.zeros_like(fill)])
    scatter_row = jnp.concatenate([n_rows + fill, scatter_row])
    shape = (n_blocks + 1, 1, EXPERT_BLOCK)
    return gather_tok.reshape(shape), scatter_row.reshape(shape), block_exp, n_rows + EXPERT_BLOCK


def _moe_ffn_ln(x, router_w, router_b, w_gu, b_gu, w_dn, b_dn, ln):
    T, D = x.shape
    w_pad = jnp.zeros((D, LANES), F32).at[:, :N_EXPERTS].set(router_w)
    w3 = jnp.stack(_split3(w_pad))
    b_row = jnp.full((1, LANES), NEG_BIG, F32).at[0, :N_EXPERTS].set(router_b)
    idx_rows, gate_rows = _router(x, w3, b_row)
    gather_tok, scatter_row, block_exp, n_out_rows = _routing_plan(idx_rows[:, :TOP_K])
    y_rows = _moe_experts(x, gather_tok, scatter_row, block_exp, n_out_rows,
                          w_gu.astype(BF16), b_gu, w_dn.astype(BF16), b_dn)
    return _combine_ln(y_rows, gate_rows, x, ln)


def _pad_rows(rows, n=SUBLANES):
    out = jnp.zeros((n, rows[0].shape[-1]), F32)
    for j, r in enumerate(rows):
        out = out.at[j].set(r.reshape(-1))
    return out


def kernel(x, ln_g, ln_b, rwkv_mix, rwkv_w_rkv, rwkv_w_o, rwkv_w0, rwkv_w1, rwkv_w2, rwkv_a0, rwkv_a1, rwkv_a2, rwkv_g1, rwkv_g2, rwkv_k_k, rwkv_k_a, rwkv_r_k, rwkv_lnx_g, rwkv_lnx_b, rwkv_v0, rwkv_v1, rwkv_v2, kv_w, da_w_q, da_w_o, da_lambda, da_subln_g, moe_router_w, moe_router_b, moe_w_gu, moe_b_gu, moe_w_dn, moe_b_dn):
    B, S, D = x.shape
    T = B * S
    v_first = None
    kv = None
    slopes = 2.0 ** (-8.0 * jnp.arange(1, DA_HEADS + 1, dtype=F32) / DA_HEADS)
    slopes = jnp.broadcast_to(slopes[:, None, None], (DA_HEADS, 1, LANES))
    for l in range(DEPTH):
        ln_mix = jnp.stack([ln_g[l, 0], ln_b[l, 0]])
        ln_ffn = jnp.stack([ln_g[l, 1], ln_b[l, 1]])
        xt = x.reshape(T, D)
        if l < N_A_LAYERS:
            zero = jnp.zeros((D,), F32)
            vecs = _pad_rows([rwkv_w0[l], rwkv_a0[l], rwkv_v0[l - 1] if l > 0 else zero])
            vres = None if l == 0 else (rwkv_v1[l - 1].astype(BF16), rwkv_v2[l - 1].astype(BF16), v_first)
            r, lw, k, v, a, g = _rwkv_proj(
                x, _pad_rows(list(rwkv_mix[l])), rwkv_w_rkv[l].astype(BF16),
                rwkv_w1[l].astype(BF16), rwkv_w2[l].astype(BF16), rwkv_a1[l].astype(BF16),
                rwkv_a2[l].astype(BF16), rwkv_g1[l].astype(BF16), rwkv_g2[l].astype(BF16), vecs, vres)
            if l == 0:
                v_first = v
            head_params = _pad_rows([rwkv_k_k[l], rwkv_k_a[l],
                                     jnp.tile(rwkv_r_k[l].reshape(-1), 1), rwkv_lnx_g[l], rwkv_lnx_b[l]])
            y = _wkv(r, lw, k, v, a, head_params)
            xt = _proj_ln(y.reshape(T, D), g.reshape(T, D), rwkv_w_o[l].astype(BF16), xt, ln_mix)
        else:
            if l == N_A_LAYERS:
                kv = _matmul(xt, kv_w.astype(BF16)).reshape(B, S, 2 * D)
            j = l - N_A_LAYERS
            lambda_init = 0.8 - 0.6 * math.exp(-0.3 * l)
            lam = da_lambda[j].astype(F32)
            lam_full = (jnp.exp(jnp.sum(lam[0] * lam[1])) - jnp.exp(jnp.sum(lam[2] * lam[3]))
                        + lambda_init)
            q = _matmul(xt, da_w_q[j].astype(BF16)).reshape(B, S, D)
            o = _diff_attention(q, kv, slopes, jnp.full((1, LANES), lam_full, F32),
                                da_subln_g[j].reshape(1, LANES), lambda_init)
            xt = _proj_ln(o.reshape(T, D), None, da_w_o[j].astype(BF16), xt, ln_mix)
        xt = _moe_ffn_ln(xt, moe_router_w[l], moe_router_b[l], moe_w_gu[l], moe_b_gu[l],
                         moe_w_dn[l], moe_b_dn[l], ln_ffn)
        x = xt.reshape(B, S, D)
    return x
```

```python
import functools
import math

import jax
import jax.numpy as jnp
from jax import lax
from jax.experimental import pallas as pl
from jax.experimental.pallas import tpu as pltpu

F32 = jnp.float32
BF16 = jnp.bfloat16

D_MODEL = 1024
DEPTH = 4
N_A_LAYERS = DEPTH // 2
RWKV_HEAD_DIM = 64
RWKV_GN_EPS = 64e-5
L2_EPS = 1e-12
DA_HEAD_DIM = 64
DA_HEADS = D_MODEL // (2 * DA_HEAD_DIM)
SUBLN_EPS = 1e-5
N_EXPERTS = 32
TOP_K = 4
EXPERT_FF = D_MODEL
SWIGLU_LIMIT = 7.0
SWIGLU_ALPHA = 1.702
EXPERT_BLOCK = 512
DEEPNORM_ALPHA = (2 * DEPTH) ** 0.25
LN_EPS = 1e-5

LANES = 128
SUBLANES = 8
VMEM_LIMIT_BYTES = 56 * 1024 * 1024

WKV_CHUNK = 64
WKV_TILES_PER_STEP = 8
HEADS_PER_TILE = LANES // RWKV_HEAD_DIM
NEG_BIG = -1e30
ATTN_ONES_ROWS = 16
ATTN_HEADS_PER_STEP = 4


def _params(*sem):
    return pltpu.CompilerParams(dimension_semantics=sem, vmem_limit_bytes=VMEM_LIMIT_BYTES)


def _dot(a, b):
    return jnp.dot(a.astype(BF16), b.astype(BF16), preferred_element_type=F32)


def _dot_nt(a, b):
    return lax.dot_general(a.astype(BF16), b.astype(BF16), (((1,), (1,)), ((), ())),
                           preferred_element_type=F32)


def _split3(x):
    hi = x.astype(BF16)
    r1 = x - hi.astype(F32)
    mid = r1.astype(BF16)
    lo = (r1 - mid.astype(F32)).astype(BF16)
    return hi, mid, lo


def _split2(x):
    hi = x.astype(BF16)
    return hi, (x - hi.astype(F32)).astype(BF16)


def _dot_exact_rhs(x, m_bf16):
    hi, lo = _split2(x)
    return (jnp.dot(hi, m_bf16, preferred_element_type=F32)
            + jnp.dot(lo, m_bf16, preferred_element_type=F32))


def _dot_exact_lhs(m_bf16, x):
    hi, lo = _split2(x)
    return (jnp.dot(m_bf16, hi, preferred_element_type=F32)
            + jnp.dot(m_bf16, lo, preferred_element_type=F32))


def _layer_norm(z, g, b):
    mu = jnp.mean(z, axis=-1, keepdims=True)
    d = z - mu
    var = jnp.mean(d * d, axis=-1, keepdims=True)
    return d * lax.rsqrt(var + LN_EPS) * g + b


def _row_tile(n, want):
    t = min(want, n)
    assert n % t == 0 and t % SUBLANES == 0
    return t


def _rwkv_proj_kernel(has_vres, *refs):
    if has_vres:
        (x_ref, xp_ref, mix_ref, wrkv_ref, w1_ref, w2_ref, a1_ref, a2_ref, g1_ref, g2_ref,
         vec_ref, v1_ref, v2_ref, vfirst_ref,
         r_out, lw_out, k_out, v_out, a_out, g_out) = refs
    else:
        (x_ref, xp_ref, mix_ref, wrkv_ref, w1_ref, w2_ref, a1_ref, a2_ref, g1_ref, g2_ref,
         vec_ref, r_out, lw_out, k_out, v_out, a_out, g_out) = refs
    i = pl.program_id(1)
    x = x_ref[0]
    prev = xp_ref[0][SUBLANES - 1:SUBLANES, :]
    prev = jnp.where(i == 0, 0.0, prev)
    row = lax.broadcasted_iota(jnp.int32, x.shape, 0)
    x_prev = jnp.where(row == 0, prev, pltpu.roll(x, 1, axis=0))
    xx = x_prev - x

    def xm(j):
        return x + xx * mix_ref[j:j + 1, :]

    xv = xm(2)
    r_out[0] = _dot(xm(0), wrkv_ref[0])
    k_out[0] = _dot(xm(1), wrkv_ref[1])
    v = _dot(xv, wrkv_ref[2])
    wl = vec_ref[0:1, :] + _dot(jnp.tanh(_dot(xm(3), w1_ref[...])), w2_ref[...])
    lw_out[0] = -jax.nn.sigmoid(wl) * math.exp(-0.5)
    a_out[0] = jax.nn.sigmoid(vec_ref[1:2, :] + _dot(_dot(xm(4), a1_ref[...]), a2_ref[...]))
    g_out[0] = _dot(jax.nn.sigmoid(_dot(xm(5), g1_ref[...])), g2_ref[...])
    if has_vres:
        mixv = jax.nn.sigmoid(vec_ref[2:3, :] + _dot(_dot(xv, v1_ref[...]), v2_ref[...]))
        v = v + (vfirst_ref[0] - v) * mixv
    v_out[0] = v


def _rwkv_proj(x, mix, w_rkv, w1, w2, a1, a2, g1, g2, vecs, vres):
    B, S, D = x.shape
    tm = _row_tile(S, 512)
    has_vres = vres is not None
    full = lambda a: pl.BlockSpec(a.shape, lambda b, i: (0,) * a.ndim)
    tile = pl.BlockSpec((1, tm, D), lambda b, i: (b, i, 0))
    prev = pl.BlockSpec((1, SUBLANES, D),
                        lambda b, i: (b, jnp.maximum(i * (tm // SUBLANES) - 1, 0), 0))
    ins = [x, x, mix, w_rkv, w1, w2, a1, a2, g1, g2, vecs]
    specs = [tile, prev] + [full(a) for a in ins[2:]]
    if has_vres:
        v1, v2, v_first = vres
        ins += [v1, v2, v_first]
        specs += [full(v1), full(v2), tile]
    out = jax.ShapeDtypeStruct((B, S, D), F32)
    return pl.pallas_call(
        functools.partial(_rwkv_proj_kernel, has_vres),
        grid=(B, S // tm),
        in_specs=specs,
        out_specs=[tile] * 6,
        out_shape=[out] * 6,
        compiler_params=_params("parallel", "arbitrary"),
    )(*ins)


def _wkv_kernel(r_ref, lw_ref, k_ref, v_ref, a_ref, hp_ref, y_out,
                state, avec_s, bvec_s, kmod_s, y_s):
    C = WKV_CHUNK
    ts = r_ref.shape[1]
    n_tiles = r_ref.shape[2] // LANES
    n_chunks = ts // C
    R2 = HEADS_PER_TILE * C
    inv_n = 1.0 / RWKV_HEAD_DIM

    @pl.when(pl.program_id(2) == 0)
    def _():
        state[...] = jnp.zeros_like(state)

    li = lax.broadcasted_iota(jnp.int32, (LANES, LANES), 0)
    lj = lax.broadcasted_iota(jnp.int32, (LANES, LANES), 1)
    head_ones = (li // RWKV_HEAD_DIM == lj // RWKV_HEAD_DIM).astype(BF16)

    for t in range(n_tiles):
        ls = slice(t * LANES, (t + 1) * LANES)
        k = k_ref[0, :, ls]
        a = a_ref[0, :, ls]
        kk = k * hp_ref[0:1, ls]
        n2 = _dot_exact_rhs(kk * kk, head_ones)
        kk = kk / jnp.maximum(jnp.sqrt(n2), L2_EPS)
        avec_s[:, ls] = -kk
        bvec_s[:, ls] = kk * a
        kmod_s[:, ls] = k * (1.0 + (a - 1.0) * hp_ref[1:2, ls])

    ri = lax.broadcasted_iota(jnp.int32, (R2, R2), 0)
    rj = lax.broadcasted_iota(jnp.int32, (R2, R2), 1)
    same_head = (ri // C) == (rj // C)
    strict = same_head & ((ri % C) > (rj % C))
    incl = same_head & ((ri % C) >= (rj % C))
    ci = lax.broadcasted_iota(jnp.int32, (C, C), 0)
    cj = lax.broadcasted_iota(jnp.int32, (C, C), 1)
    tri = (ci >= cj).astype(BF16)
    lane = lax.broadcasted_iota(jnp.int32, (C, LANES), 1)
    head0 = lane < RWKV_HEAD_DIM
    n_doublings = int(math.log2(C))

    def stack(t):
        return jnp.concatenate([jnp.where(head0, t, 0.0), jnp.where(head0, 0.0, t)], axis=0)

    tiles = range(n_tiles)
    lanes_of = lambda t: slice(t * LANES, (t + 1) * LANES)

    def chunk(c, carry):
        sl = pl.ds(pl.multiple_of(c * C, C), C)
        each = lambda f: [f(t) for t in tiles]
        s0 = each(lambda t: state[t])
        lw = each(lambda t: lw_ref[0, sl, lanes_of(t)])
        cum = each(lambda t: _dot_exact_lhs(tri, lw[t]))
        tot = each(lambda t: cum[t][C - 1:C, :])
        e_neg = each(lambda t: jnp.exp(-cum[t]))
        av = each(lambda t: avec_s[sl, lanes_of(t)])
        bv = each(lambda t: bvec_s[sl, lanes_of(t)])
        km = each(lambda t: kmod_s[sl, lanes_of(t)])
        lhs = each(lambda t: jnp.concatenate(
            [stack(av[t] * jnp.exp(cum[t] - lw[t])),
             stack(r_ref[0, sl, lanes_of(t)] * jnp.exp(cum[t]))], axis=0))
        rhs = each(lambda t: jnp.concatenate([stack(bv[t] * e_neg[t]), stack(km[t] * e_neg[t])], axis=0))
        g = each(lambda t: _dot_nt(lhs[t], rhs[t]))
        a_s = each(lambda t: _dot_nt(lhs[t], s0[t]))
        vm = each(lambda t: stack(v_ref[0, sl, lanes_of(t)]))
        akv = each(lambda t: _dot(jnp.where(strict, g[t][:R2, R2:], 0.0), vm[t]))
        u = each(lambda t: a_s[t][:R2] + akv[t])
        p = each(lambda t: jnp.where(strict, g[t][:R2, :R2], 0.0))
        for it in range(n_doublings):
            pu = each(lambda t: _dot(p[t], u[t]))
            if it + 1 < n_doublings:
                p = each(lambda t: _dot(p[t], p[t]))
            u = each(lambda t: u[t] + pu[t])
        uv = each(lambda t: jnp.concatenate([u[t], vm[t]], axis=0))
        w_y = each(lambda t: jnp.concatenate([jnp.where(incl, g[t][R2:, :R2], 0.0),
                                              jnp.where(incl, g[t][R2:, R2:], 0.0)], axis=1))
        y_sm = each(lambda t: a_s[t][R2:] + _dot(w_y[t], uv[t]))
        bk_end = each(lambda t: jnp.concatenate(
            [stack(bv[t] * jnp.exp(tot[t] - cum[t])), stack(km[t] * jnp.exp(tot[t] - cum[t]))], axis=0))
        s1 = each(lambda t: s0[t] * jnp.exp(tot[t]) + _dot(uv[t].T, bk_end[t]))
        for t in tiles:
            y_s[sl, lanes_of(t)] = y_sm[t][:C] + y_sm[t][C:]
            state[t] = s1[t]
        return carry

    lax.fori_loop(0, n_chunks, chunk, 0)

    for t in range(n_tiles):
        ls = slice(t * LANES, (t + 1) * LANES)
        y = y_s[:, ls]
        mu = _dot_exact_rhs(y, head_ones) * inv_n
        d = y - mu
        var = _dot_exact_rhs(d * d, head_ones) * inv_n
        yn = d * lax.rsqrt(var + RWKV_GN_EPS) * hp_ref[3:4, ls] + hp_ref[4:5, ls]
        bonus = _dot_exact_rhs(r_ref[0, :, ls] * kmod_s[:, ls] * hp_ref[2:3, ls], head_ones)
        y_out[0, :, ls] = yn + bonus * v_ref[0, :, ls]


def _wkv(r, lw, k, v, a, head_params):
    B, S, D = r.shape
    ts = _row_tile(S, 512)
    assert ts % WKV_CHUNK == 0
    width = WKV_TILES_PER_STEP * LANES
    tile = pl.BlockSpec((1, ts, width), lambda b, p, i: (b, i, p))
    hp = pl.BlockSpec((SUBLANES, width), lambda b, p, i: (0, p))
    return pl.pallas_call(
        _wkv_kernel,
        grid=(B, D // width, S // ts),
        in_specs=[tile] * 5 + [hp],
        out_specs=tile,
        out_shape=jax.ShapeDtypeStruct((B, S, D), F32),
        scratch_shapes=[pltpu.VMEM((WKV_TILES_PER_STEP, LANES, LANES), F32)]
        + [pltpu.VMEM((ts, width), F32)] * 4,
        compiler_params=_params("parallel", "parallel", "arbitrary"),
    )(r, lw, k, v, a, head_params)


def _matmul_kernel(a_ref, w_ref, o_ref):
    o_ref[...] = _dot(a_ref[...], w_ref[...])


def _matmul(a, w):
    T, K = a.shape
    N = w.shape[1]
    tm = _row_tile(T, 512)
    return pl.pallas_call(
        _matmul_kernel,
        grid=(T // tm,),
        in_specs=[pl.BlockSpec((tm, K), lambda i: (i, 0)), pl.BlockSpec((K, N), lambda i: (0, 0))],
        out_specs=pl.BlockSpec((tm, N), lambda i: (i, 0)),
        out_shape=jax.ShapeDtypeStruct((T, N), F32),
        compiler_params=_params("parallel"),
    )(a, w)


def _proj_ln_kernel(has_gate, *refs):
    if has_gate:
        a_ref, g_ref, w_ref, x_ref, ln_ref, o_ref = refs
        a = a_ref[...] * g_ref[...]
    else:
        a_ref, w_ref, x_ref, ln_ref, o_ref = refs
        a = a_ref[...]
    z = DEEPNORM_ALPHA * x_ref[...] + _dot(a, w_ref[...])
    o_ref[...] = _layer_norm(z, ln_ref[0:1, :], ln_ref[1:2, :])


def _proj_ln(a, gate, w, x, ln):
    T, D = x.shape
    tm = _row_tile(T, 512)
    tile = pl.BlockSpec((tm, D), lambda i: (i, 0))
    full = lambda t: pl.BlockSpec(t.shape, lambda i: (0, 0))
    has_gate = gate is not None
    ins = [a] + ([gate] if has_gate else []) + [w, x, ln]
    specs = [tile] + ([tile] if has_gate else []) + [full(w), tile, full(ln)]
    return pl.pallas_call(
        functools.partial(_proj_ln_kernel, has_gate),
        grid=(T // tm,),
        in_specs=specs,
        out_specs=tile,
        out_shape=jax.ShapeDtypeStruct((T, D), F32),
        compiler_params=_params("parallel"),
    )(*ins)


def _max_rows(s, groups=4):
    step = s.shape[0] // groups
    part = s[0:step]
    for g in range(1, groups):
        part = jnp.maximum(part, s[g * step:(g + 1) * step])
    return jnp.max(part, axis=0, keepdims=True)


def _attn_kernel(out_scale, q_ref, k_ref, v_ref, slope_ref, lam_ref, g_ref, o_ref,
                 k_s, vt_s, acc_s):
    i = pl.program_id(2)
    tq = q_ref.shape[1]
    tk = tq
    n_heads = q_ref.shape[2] // LANES
    n_kv_blocks = k_ref.shape[1] // tk
    heads = range(n_heads)
    chains = [(h, c) for h in heads for c in range(2)]
    lanes_of = lambda h: slice(h * LANES, (h + 1) * LANES)

    @pl.when(i == 0)
    def _():
        lane = lax.broadcasted_iota(jnp.int32, (tk, LANES), 1)
        for h in heads:
            for blk in range(n_kv_blocks):
                sl = slice(blk * tk, (blk + 1) * tk)
                kb = k_ref[0, sl, lanes_of(h)]
                k_s[2 * h, sl, :] = jnp.where(lane < DA_HEAD_DIM, kb, 0.0).astype(BF16)
                k_s[2 * h + 1, sl, :] = jnp.where(lane < DA_HEAD_DIM, 0.0, kb).astype(BF16)
                vt_s[h, blk, 0:LANES, :] = v_ref[0, sl, lanes_of(h)].T.astype(BF16)
                vt_s[h, blk, LANES:, :] = jnp.ones((ATTN_ONES_ROWS, tk), BF16)

    q = [(q_ref[0, :, lanes_of(h)] * (DA_HEAD_DIM ** -0.5)).astype(BF16) for h in heads]
    slope = [slope_ref[0, :, h * LANES:h * LANES + 1] for h in heads]
    k_loc = lax.broadcasted_iota(jnp.int32, (tk, tq), 0)
    q_loc = lax.broadcasted_iota(jnp.int32, (tk, tq), 1)
    k_loc_f = k_loc.astype(F32)
    acc_s[...] = jnp.zeros_like(acc_s)

    def block(j, ms, diagonal):
        sl = pl.ds(pl.multiple_of(j * tk, tk), tk)
        rel = ((j - i) * tk).astype(F32)
        each = lambda f: [f(n, h) for n, (h, _) in enumerate(chains)]
        s = each(lambda n, h: _dot_nt(k_s[n, sl, :], q[h]) + slope[h] * k_loc_f)
        if diagonal:
            s = each(lambda n, h: jnp.where(k_loc > q_loc, NEG_BIG, s[n]))
        offset = [slope[h] * rel for h in heads]
        m_new = each(lambda n, h: jnp.maximum(ms[n], _max_rows(s[n]) + offset[h]))
        p = each(lambda n, h: jnp.exp(s[n] - (m_new[n] - offset[h])).astype(BF16))
        acc_old = each(lambda n, h: acc_s[n])
        pv = each(lambda n, h: jnp.dot(vt_s[h, j], p[n], preferred_element_type=F32))
        for n in range(len(chains)):
            acc_s[n] = jnp.exp(ms[n] - m_new[n]) * acc_old[n] + pv[n]
        return tuple(m_new)

    init = tuple(jnp.full((1, tq), NEG_BIG, F32) for _ in chains)
    ms = lax.fori_loop(0, i, lambda j, ms: block(j, ms, False), init)
    block(i, ms, True)

    for h in heads:
        acc0, acc1 = acc_s[2 * h], acc_s[2 * h + 1]
        o_t = (acc0[:LANES] / acc0[LANES:LANES + 1]
               - lam_ref[:, 0:1] * (acc1[:LANES] / acc1[LANES:LANES + 1]))
        o = o_t.T
        o = o * lax.rsqrt(jnp.mean(o * o, axis=-1, keepdims=True) + SUBLN_EPS) * g_ref[...]
        o_ref[0, :, lanes_of(h)] = o * out_scale


def _diff_attention(q, kv, slopes, lam_row, subln_g, lambda_init):
    B, S, D = q.shape
    tq = _row_tile(S, 256)
    nh = ATTN_HEADS_PER_STEP
    width = nh * LANES
    n_groups = DA_HEADS // nh
    return pl.pallas_call(
        functools.partial(_attn_kernel, 1.0 - lambda_init),
        grid=(B, n_groups, S // tq),
        in_specs=[pl.BlockSpec((1, tq, width), lambda b, h, i: (b, i, h)),
                  pl.BlockSpec((1, S, width), lambda b, h, i: (b, 0, h)),
                  pl.BlockSpec((1, S, width), lambda b, h, i: (b, 0, n_groups + h)),
                  pl.BlockSpec((1, 1, width), lambda b, h, i: (h, 0, 0)),
                  pl.BlockSpec((1, LANES), lambda b, h, i: (0, 0)),
                  pl.BlockSpec((1, LANES), lambda b, h, i: (0, 0))],
        out_specs=pl.BlockSpec((1, tq, width), lambda b, h, i: (b, i, h)),
        out_shape=jax.ShapeDtypeStruct((B, S, D), F32),
        scratch_shapes=[pltpu.VMEM((2 * nh, S, LANES), BF16),
                        pltpu.VMEM((nh, S // tq, LANES + ATTN_ONES_ROWS, tq), BF16),
                        pltpu.VMEM((2 * nh, LANES + ATTN_ONES_ROWS, tq), F32)],
        compiler_params=_params("parallel", "parallel", "arbitrary"),
    )(q, kv, kv, slopes.reshape(n_groups, 1, width), lam_row, subln_g)


def _router_kernel(x_ref, w_ref, b_ref, idx_out, gate_out):
    x = x_ref[...]
    xh, xm_, xl = _split3(x)
    w = w_ref[...]
    f = lambda p, q: jnp.dot(p, q, preferred_element_type=F32)
    logits = (f(xh, w[0]) + (f(xh, w[1]) + f(xm_, w[0]))
              + (f(xh, w[2]) + f(xm_, w[1]) + f(xl, w[0]))) + b_ref[...]
    lane = lax.broadcasted_iota(jnp.int32, logits.shape, 1)
    vals, idxs = [], []
    cur = logits
    for _ in range(TOP_K):
        m = jnp.max(cur, axis=-1, keepdims=True)
        idx = jnp.min(jnp.where(cur == m, lane, LANES), axis=-1, keepdims=True)
        cur = jnp.where(lane == idx, -jnp.inf, cur)
        vals.append(m)
        idxs.append(idx)
    es = [jnp.exp(v - vals[0]) for v in vals]
    denom = es[0] + es[1] + es[2] + es[3]
    idx_row = jnp.zeros(logits.shape, jnp.int32)
    gate_row = jnp.zeros(logits.shape, F32)
    for kx in range(TOP_K):
        idx_row = jnp.where(lane == kx, idxs[kx], idx_row)
        gate_row = jnp.where(lane == kx, es[kx] / denom, gate_row)
    idx_out[...] = idx_row
    gate_out[...] = gate_row


def _router(x, w3, b_row):
    T, D = x.shape
    tm = _row_tile(T, 512)
    tile = pl.BlockSpec((tm, D), lambda i: (i, 0))
    out_tile = pl.BlockSpec((tm, LANES), lambda i: (i, 0))
    return pl.pallas_call(
        _router_kernel,
        grid=(T // tm,),
        in_specs=[tile, pl.BlockSpec(w3.shape, lambda i: (0, 0, 0)),
                  pl.BlockSpec(b_row.shape, lambda i: (0, 0))],
        out_specs=[out_tile, out_tile],
        out_shape=[jax.ShapeDtypeStruct((T, LANES), jnp.int32),
                   jax.ShapeDtypeStruct((T, LANES), F32)],
        compiler_params=_params("parallel"),
    )(x, w3, b_row)


def _clamped_swiglu(h):
    gate, up = h[:, :EXPERT_FF], h[:, EXPERT_FF:]
    gate = jnp.minimum(gate, SWIGLU_LIMIT)
    up = jnp.clip(up, -SWIGLU_LIMIT, SWIGLU_LIMIT)
    return (up + 1.0) * (gate * jax.nn.sigmoid(SWIGLU_ALPHA * gate))


def _moe_kernel(bexp_ref, g_first, g_b, g_next, s_prev, s_a, s_b, x_hbm,
                wgu_a, bgu_a, wdn_a, bdn_a, wgu_b, bgu_b, wdn_b, bdn_b, y_hbm,
                xbuf_a, xbuf_b, ybuf_a, ybuf_b, sem_ga, sem_gb, sem_sa, sem_sb):
    del bexp_ref
    g = pl.program_id(0)
    last = pl.num_programs(0) - 1

    def gather(idx_ref, xbuf, sem, j):
        return pltpu.make_async_copy(x_hbm.at[idx_ref[0, 0, j]], xbuf.at[j], sem)

    def scatter(idx_ref, ybuf, sem, j):
        return pltpu.make_async_copy(ybuf.at[j], y_hbm.at[idx_ref[0, 0, j]], sem)

    def wait_gather(xbuf, sem):
        pltpu.make_async_copy(x_hbm.at[pl.ds(0, EXPERT_BLOCK)], xbuf, sem).wait()

    def wait_scatter(ybuf, sem):
        pltpu.make_async_copy(ybuf, y_hbm.at[pl.ds(0, EXPERT_BLOCK)], sem).wait()

    def expert_mlp(xbuf, ybuf, wgu, bgu, wdn, bdn):
        h = _dot(xbuf[...], wgu[0]) + bgu[0]
        ybuf[...] = _dot(_clamped_swiglu(h), wdn[0]) + bdn[0]

    @pl.when(g == 0)
    def _():
        ybuf_b[...] = jnp.zeros_like(ybuf_b)

        def first(j, c):
            gather(g_first, xbuf_a, sem_ga, j).start()
            return c
        lax.fori_loop(0, EXPERT_BLOCK, first, 0)

    wait_gather(xbuf_a, sem_ga)

    @pl.when(g > 0)
    def _():
        wait_scatter(ybuf_a, sem_sa)

    for j in range(EXPERT_BLOCK):
        gather(g_b, xbuf_b, sem_gb, j).start()
        scatter(s_prev, ybuf_b, sem_sb, j).start()
    expert_mlp(xbuf_a, ybuf_a, wgu_a, bgu_a, wdn_a, bdn_a)

    wait_gather(xbuf_b, sem_gb)
    wait_scatter(ybuf_b, sem_sb)

    for j in range(EXPERT_BLOCK):
        gather(g_next, xbuf_a, sem_ga, j).start()
        scatter(s_a, ybuf_a, sem_sa, j).start()
    expert_mlp(xbuf_b, ybuf_b, wgu_b, bgu_b, wdn_b, bdn_b)

    @pl.when(g == last)
    def _():
        def final(j, c):
            scatter(s_b, ybuf_b, sem_sb, j).start()
            return c
        lax.fori_loop(0, EXPERT_BLOCK, final, 0)
        wait_gather(xbuf_a, sem_ga)
        wait_scatter(ybuf_a, sem_sa)
        wait_scatter(ybuf_b, sem_sb)


def _moe_experts(x, gather_tok, scatter_row, block_exp, n_out_rows, w_gu, b_gu, w_dn, b_dn):
    T, D = x.shape
    n_blocks = block_exp.shape[0]
    assert n_blocks % 2 == 0
    FF2 = w_gu.shape[2]
    idx_spec = lambda fn: pl.BlockSpec((1, 1, EXPERT_BLOCK), fn, memory_space=pltpu.SMEM)
    weights = lambda off: [
        pl.BlockSpec((1, D, FF2), lambda g, be: (be[2 * g + off], 0, 0)),
        pl.BlockSpec((1, 1, FF2), lambda g, be: (be[2 * g + off], 0, 0)),
        pl.BlockSpec((1, EXPERT_FF, D), lambda g, be: (be[2 * g + off], 0, 0)),
        pl.BlockSpec((1, 1, D), lambda g, be: (be[2 * g + off], 0, 0)),
    ]
    grid_spec = pltpu.PrefetchScalarGridSpec(
        num_scalar_prefetch=1,
        grid=(n_blocks // 2,),
        in_specs=[
            idx_spec(lambda g, be: (0, 0, 0)),
            idx_spec(lambda g, be: (2 * g + 1, 0, 0)),
            idx_spec(lambda g, be: (2 * g + 2, 0, 0)),
            idx_spec(lambda g, be: (2 * g, 0, 0)),
            idx_spec(lambda g, be: (2 * g + 1, 0, 0)),
            idx_spec(lambda g, be: (2 * g + 2, 0, 0)),
            pl.BlockSpec(memory_space=pl.ANY),
        ] + weights(0) + weights(1),
        out_specs=pl.BlockSpec(memory_space=pl.ANY),
        scratch_shapes=[pltpu.VMEM((EXPERT_BLOCK, D), F32)] * 4 + [pltpu.SemaphoreType.DMA(())] * 4,
    )
    b_gu = b_gu.reshape(N_EXPERTS, 1, FF2)
    b_dn = b_dn.reshape(N_EXPERTS, 1, D)
    return pl.pallas_call(
        _moe_kernel,
        grid_spec=grid_spec,
        out_shape=jax.ShapeDtypeStruct((n_out_rows, D), F32),
        compiler_params=_params("arbitrary"),
    )(block_exp, gather_tok, gather_tok, gather_tok, scatter_row, scatter_row, scatter_row, x,
      w_gu, b_gu, w_dn, b_dn, w_gu, b_gu, w_dn, b_dn)


def _combine_ln_kernel(y0_ref, y1_ref, y2_ref, y3_ref, gate_ref, x_ref, ln_ref, o_ref):
    gate = gate_ref[...]
    ffn = jnp.zeros(x_ref.shape, F32)
    for kx, y_ref in enumerate((y0_ref, y1_ref, y2_ref, y3_ref)):
        ffn = ffn + y_ref[...] * gate[:, kx:kx + 1]
    z = DEEPNORM_ALPHA * x_ref[...] + ffn
    o_ref[...] = _layer_norm(z, ln_ref[0:1, :], ln_ref[1:2, :])


def _combine_ln(y_rows, gate_rows, x, ln):
    T, D = x.shape
    tm = _row_tile(T, 512)
    tile = pl.BlockSpec((tm, D), lambda i: (i, 0))
    slot = lambda kx: pl.BlockSpec((tm, D), lambda i: (kx * (T // tm) + i, 0))
    return pl.pallas_call(
        _combine_ln_kernel,
        grid=(T // tm,),
        in_specs=[slot(kx) for kx in range(TOP_K)]
        + [pl.BlockSpec((tm, LANES), lambda i: (i, 0)), tile, pl.BlockSpec(ln.shape, lambda i: (0, 0))],
        out_specs=tile,
        out_shape=jax.ShapeDtypeStruct((T, D), F32),
        compiler_params=_params("parallel"),
    )(y_rows, y_rows, y_rows, y_rows, gate_rows, x, ln)


def _routing_plan(top_idx):
    T = top_idx.shape[0]
    TK = T * TOP_K
    flat_e = top_idx.reshape(TK)
    order = jnp.argsort(flat_e).astype(jnp.int32)
    counts = jnp.bincount(flat_e, length=N_EXPERTS).astype(jnp.int32)
    padded = (counts + EXPERT_BLOCK - 1) // EXPERT_BLOCK * EXPERT_BLOCK
    pad_end = jnp.cumsum(padded)
    pad_start = pad_end - padded
    grp_start = jnp.cumsum(counts) - counts
    n_blocks = -(-TK // EXPERT_BLOCK) + N_EXPERTS
    n_rows = n_blocks * EXPERT_BLOCK
    block_start = jnp.arange(n_blocks, dtype=jnp.int32) * EXPERT_BLOCK
    block_exp = jnp.minimum(jnp.searchsorted(pad_end, block_start, side='right'),
                            N_EXPERTS - 1).astype(jnp.int32)
    pos = jnp.arange(n_rows, dtype=jnp.int32)
    e_of = jnp.repeat(block_exp, EXPERT_BLOCK)
    off = pos - pad_start[e_of]
    valid = (off >= 0) & (off < counts[e_of])
    pair = order[jnp.clip(grp_start[e_of] + off, 0, TK - 1)]
    tok, slot = pair // TOP_K, pair % TOP_K
    spare = TK + jnp.cumsum(jnp.logical_not(valid).astype(jnp.int32)) - 1
    gather_tok = jnp.where(valid, tok, 0).astype(jnp.int32)
    scatter_row = jnp.where(valid, slot * T + tok, spare).astype(jnp.int32)
    fill = jnp.arange(EXPERT_BLOCK, dtype=jnp.int32)
    gather_tok = jnp.concatenate([gather_tok, jnp.zeros_like(fill)])
    scatter_row = jnp.concatenate([n_rows + fill, scatter_row])
    shape = (n_blocks + 1, 1, EXPERT_BLOCK)
    return gather_tok.reshape(shape), scatter_row.reshape(shape), block_exp, n_rows + EXPERT_BLOCK


def _moe_ffn_ln(x, router_w, router_b, w_gu, b_gu, w_dn, b_dn, ln):
    T, D = x.shape
    w_pad = jnp.zeros((D, LANES), F32).at[:, :N_EXPERTS].set(router_w)
    w3 = jnp.stack(_split3(w_pad))
    b_row = jnp.full((1, LANES), NEG_BIG, F32).at[0, :N_EXPERTS].set(router_b)
    idx_rows, gate_rows = _router(x, w3, b_row)
    gather_tok, scatter_row, block_exp, n_out_rows = _routing_plan(idx_rows[:, :TOP_K])
    y_rows = _moe_experts(x, gather_tok, scatter_row, block_exp, n_out_rows,
                          w_gu.astype(BF16), b_gu, w_dn.astype(BF16), b_dn)
    return _combine_ln(y_rows, gate_rows, x, ln)


def _pad_rows(rows, n=SUBLANES):
    out = jnp.zeros((n, rows[0].shape[-1]), F32)
    for j, r in enumerate(rows):
        out = out.at[j].set(r.reshape(-1))
    return out


def kernel(x, ln_g, ln_b, rwkv_mix, rwkv_w_rkv, rwkv_w_o, rwkv_w0, rwkv_w1, rwkv_w2, rwkv_a0, rwkv_a1, rwkv_a2, rwkv_g1, rwkv_g2, rwkv_k_k, rwkv_k_a, rwkv_r_k, rwkv_lnx_g, rwkv_lnx_b, rwkv_v0, rwkv_v1, rwkv_v2, kv_w, da_w_q, da_w_o, da_lambda, da_subln_g, moe_router_w, moe_router_b, moe_w_gu, moe_b_gu, moe_w_dn, moe_b_dn):
    B, S, D = x.shape
    T = B * S
    v_first = None
    kv = None
    slopes = 2.0 ** (-8.0 * jnp.arange(1, DA_HEADS + 1, dtype=F32) / DA_HEADS)
    slopes = jnp.repeat(slopes, LANES).reshape(1, DA_HEADS * LANES)
    for l in range(DEPTH):
        ln_mix = jnp.stack([ln_g[l, 0], ln_b[l, 0]])
        ln_ffn = jnp.stack([ln_g[l, 1], ln_b[l, 1]])
        xt = x.reshape(T, D)
        if l < N_A_LAYERS:
            zero = jnp.zeros((D,), F32)
            vecs = _pad_rows([rwkv_w0[l], rwkv_a0[l], rwkv_v0[l - 1] if l > 0 else zero])
            vres = None if l == 0 else (rwkv_v1[l - 1].astype(BF16), rwkv_v2[l - 1].astype(BF16), v_first)
            r, lw, k, v, a, g = _rwkv_proj(
                x, _pad_rows(list(rwkv_mix[l])), rwkv_w_rkv[l].astype(BF16),
                rwkv_w1[l].astype(BF16), rwkv_w2[l].astype(BF16), rwkv_a1[l].astype(BF16),
                rwkv_a2[l].astype(BF16), rwkv_g1[l].astype(BF16), rwkv_g2[l].astype(BF16), vecs, vres)
            if l == 0:
                v_first = v
            head_params = _pad_rows([rwkv_k_k[l], rwkv_k_a[l],
                                     jnp.tile(rwkv_r_k[l].reshape(-1), 1), rwkv_lnx_g[l], rwkv_lnx_b[l]])
            y = _wkv(r, lw, k, v, a, head_params)
            xt = _proj_ln(y.reshape(T, D), g.reshape(T, D), rwkv_w_o[l].astype(BF16), xt, ln_mix)
        else:
            if l == N_A_LAYERS:
                kv = _matmul(xt, kv_w.astype(BF16)).reshape(B, S, 2 * D)
            j = l - N_A_LAYERS
            lambda_init = 0.8 - 0.6 * math.exp(-0.3 * l)
            lam = da_lambda[j].astype(F32)
            lam_full = (jnp.exp(jnp.sum(lam[0] * lam[1])) - jnp.exp(jnp.sum(lam[2] * lam[3]))
                        + lambda_init)
            q = _matmul(xt, da_w_q[j].astype(BF16)).reshape(B, S, D)
            o = _diff_attention(q, kv, slopes, jnp.full((1, LANES), lam_full, F32),
                                da_subln_g[j].reshape(1, LANES), lambda_init)
            xt = _proj_ln(o.reshape(T, D), None, da_w_o[j].astype(BF16), xt, ln_mix)
        xt = _moe_ffn_ln(xt, moe_router_w[l], moe_router_b[l], moe_w_gu[l], moe_b_gu[l],
                         moe_w_dn[l], moe_b_dn[l], ln_ffn)
        x = xt.reshape(B, S, D)
    return x
```

```python
import functools
import math

import jax
import jax.numpy as jnp
from jax import lax
from jax.experimental import pallas as pl
from jax.experimental.pallas import tpu as pltpu

F32 = jnp.float32
BF16 = jnp.bfloat16

D_MODEL = 1024
DEPTH = 4
N_A_LAYERS = DEPTH // 2
RWKV_HEAD_DIM = 64
RWKV_GN_EPS = 64e-5
L2_EPS = 1e-12
DA_HEAD_DIM = 64
DA_HEADS = D_MODEL // (2 * DA_HEAD_DIM)
SUBLN_EPS = 1e-5
N_EXPERTS = 32
TOP_K = 4
EXPERT_FF = D_MODEL
SWIGLU_LIMIT = 7.0
SWIGLU_ALPHA = 1.702
EXPERT_BLOCK = 512
DEEPNORM_ALPHA = (2 * DEPTH) ** 0.25
LN_EPS = 1e-5

LANES = 128
SUBLANES = 8
VMEM_LIMIT_BYTES = 56 * 1024 * 1024

WKV_CHUNK = 64
WKV_TILES_PER_STEP = 8
HEADS_PER_TILE = LANES // RWKV_HEAD_DIM
NEG_BIG = -1e30
ATTN_ONES_ROWS = 16
ATTN_HEADS_PER_STEP = 4


def _params(*sem):
    return pltpu.CompilerParams(dimension_semantics=sem, vmem_limit_bytes=VMEM_LIMIT_BYTES)


def _dot(a, b):
    return jnp.dot(a.astype(BF16), b.astype(BF16), preferred_element_type=F32)


def _dot_nt(a, b):
    return lax.dot_general(a.astype(BF16), b.astype(BF16), (((1,), (1,)), ((), ())),
                           preferred_element_type=F32)


def _split3(x):
    hi = x.astype(BF16)
    r1 = x - hi.astype(F32)
    mid = r1.astype(BF16)
    lo = (r1 - mid.astype(F32)).astype(BF16)
    return hi, mid, lo


def _split2(x):
    hi = x.astype(BF16)
    return hi, (x - hi.astype(F32)).astype(BF16)


def _dot_exact_rhs(x, m_bf16):
    hi, lo = _split2(x)
    return (jnp.dot(hi, m_bf16, preferred_element_type=F32)
            + jnp.dot(lo, m_bf16, preferred_element_type=F32))


def _dot_exact_lhs(m_bf16, x):
    hi, lo = _split2(x)
    return (jnp.dot(m_bf16, hi, preferred_element_type=F32)
            + jnp.dot(m_bf16, lo, preferred_element_type=F32))


def _layer_norm(z, g, b):
    mu = jnp.mean(z, axis=-1, keepdims=True)
    d = z - mu
    var = jnp.mean(d * d, axis=-1, keepdims=True)
    return d * lax.rsqrt(var + LN_EPS) * g + b


def _row_tile(n, want):
    t = min(want, n)
    assert n % t == 0 and t % SUBLANES == 0
    return t


def _rwkv_proj_kernel(has_vres, *refs):
    if has_vres:
        (x_ref, xp_ref, mix_ref, wrkv_ref, w1_ref, w2_ref, a1_ref, a2_ref, g1_ref, g2_ref,
         vec_ref, v1_ref, v2_ref, vfirst_ref,
         r_out, lw_out, k_out, v_out, a_out, g_out) = refs
    else:
        (x_ref, xp_ref, mix_ref, wrkv_ref, w1_ref, w2_ref, a1_ref, a2_ref, g1_ref, g2_ref,
         vec_ref, r_out, lw_out, k_out, v_out, a_out, g_out) = refs
    i = pl.program_id(1)
    x = x_ref[0]
    prev = xp_ref[0][SUBLANES - 1:SUBLANES, :]
    prev = jnp.where(i == 0, 0.0, prev)
    row = lax.broadcasted_iota(jnp.int32, x.shape, 0)
    x_prev = jnp.where(row == 0, prev, pltpu.roll(x, 1, axis=0))
    xx = x_prev - x

    def xm(j):
        return x + xx * mix_ref[j:j + 1, :]

    xv = xm(2)
    r_out[0] = _dot(xm(0), wrkv_ref[0])
    k_out[0] = _dot(xm(1), wrkv_ref[1])
    v = _dot(xv, wrkv_ref[2])
    wl = vec_ref[0:1, :] + _dot(jnp.tanh(_dot(xm(3), w1_ref[...])), w2_ref[...])
    lw_out[0] = -jax.nn.sigmoid(wl) * math.exp(-0.5)
    a_out[0] = jax.nn.sigmoid(vec_ref[1:2, :] + _dot(_dot(xm(4), a1_ref[...]), a2_ref[...]))
    g_out[0] = _dot(jax.nn.sigmoid(_dot(xm(5), g1_ref[...])), g2_ref[...])
    if has_vres:
        mixv = jax.nn.sigmoid(vec_ref[2:3, :] + _dot(_dot(xv, v1_ref[...]), v2_ref[...]))
        v = v + (vfirst_ref[0] - v) * mixv
    v_out[0] = v


def _rwkv_proj(x, mix, w_rkv, w1, w2, a1, a2, g1, g2, vecs, vres):
    B, S, D = x.shape
    tm = _row_tile(S, 512)
    has_vres = vres is not None
    full = lambda a: pl.BlockSpec(a.shape, lambda b, i: (0,) * a.ndim)
    tile = pl.BlockSpec((1, tm, D), lambda b, i: (b, i, 0))
    prev = pl.BlockSpec((1, SUBLANES, D),
                        lambda b, i: (b, jnp.maximum(i * (tm // SUBLANES) - 1, 0), 0))
    ins = [x, x, mix, w_rkv, w1, w2, a1, a2, g1, g2, vecs]
    specs = [tile, prev] + [full(a) for a in ins[2:]]
    if has_vres:
        v1, v2, v_first = vres
        ins += [v1, v2, v_first]
        specs += [full(v1), full(v2), tile]
    out = jax.ShapeDtypeStruct((B, S, D), F32)
    return pl.pallas_call(
        functools.partial(_rwkv_proj_kernel, has_vres),
        grid=(B, S // tm),
        in_specs=specs,
        out_specs=[tile] * 6,
        out_shape=[out] * 6,
        compiler_params=_params("parallel", "arbitrary"),
    )(*ins)


def _wkv_kernel(r_ref, lw_ref, k_ref, v_ref, a_ref, hp_ref, y_out,
                state, avec_s, bvec_s, kmod_s, y_s):
    C = WKV_CHUNK
    ts = r_ref.shape[1]
    n_tiles = r_ref.shape[2] // LANES
    n_chunks = ts // C
    R2 = HEADS_PER_TILE * C
    inv_n = 1.0 / RWKV_HEAD_DIM

    @pl.when(pl.program_id(2) == 0)
    def _():
        state[...] = jnp.zeros_like(state)

    li = lax.broadcasted_iota(jnp.int32, (LANES, LANES), 0)
    lj = lax.broadcasted_iota(jnp.int32, (LANES, LANES), 1)
    head_ones = (li // RWKV_HEAD_DIM == lj // RWKV_HEAD_DIM).astype(BF16)

    for t in range(n_tiles):
        ls = slice(t * LANES, (t + 1) * LANES)
        k = k_ref[0, :, ls]
        a = a_ref[0, :, ls]
        kk = k * hp_ref[0:1, ls]
        n2 = _dot_exact_rhs(kk * kk, head_ones)
        kk = kk / jnp.maximum(jnp.sqrt(n2), L2_EPS)
        avec_s[:, ls] = -kk
        bvec_s[:, ls] = kk * a
        kmod_s[:, ls] = k * (1.0 + (a - 1.0) * hp_ref[1:2, ls])

    ri = lax.broadcasted_iota(jnp.int32, (R2, R2), 0)
    rj = lax.broadcasted_iota(jnp.int32, (R2, R2), 1)
    same_head = (ri // C) == (rj // C)
    strict = same_head & ((ri % C) > (rj % C))
    incl = same_head & ((ri % C) >= (rj % C))
    ci = lax.broadcasted_iota(jnp.int32, (C, C), 0)
    cj = lax.broadcasted_iota(jnp.int32, (C, C), 1)
    tri = (ci >= cj).astype(BF16)
    lane = lax.broadcasted_iota(jnp.int32, (C, LANES), 1)
    head0 = lane < RWKV_HEAD_DIM
    n_doublings = int(math.log2(C))

    def stack(t):
        return jnp.concatenate([jnp.where(head0, t, 0.0), jnp.where(head0, 0.0, t)], axis=0)

    tiles = range(n_tiles)
    lanes_of = lambda t: slice(t * LANES, (t + 1) * LANES)

    def chunk(c, carry):
        sl = pl.ds(pl.multiple_of(c * C, C), C)
        each = lambda f: [f(t) for t in tiles]
        s0 = each(lambda t: state[t])
        lw = each(lambda t: lw_ref[0, sl, lanes_of(t)])
        cum = each(lambda t: _dot_exact_lhs(tri, lw[t]))
        tot = each(lambda t: cum[t][C - 1:C, :])
        e_neg = each(lambda t: jnp.exp(-cum[t]))
        av = each(lambda t: avec_s[sl, lanes_of(t)])
        bv = each(lambda t: bvec_s[sl, lanes_of(t)])
        km = each(lambda t: kmod_s[sl, lanes_of(t)])
        lhs = each(lambda t: jnp.concatenate(
            [stack(av[t] * jnp.exp(cum[t] - lw[t])),
             stack(r_ref[0, sl, lanes_of(t)] * jnp.exp(cum[t]))], axis=0))
        rhs = each(lambda t: jnp.concatenate([stack(bv[t] * e_neg[t]), stack(km[t] * e_neg[t])], axis=0))
        g = each(lambda t: _dot_nt(lhs[t], rhs[t]))
        a_s = each(lambda t: _dot_nt(lhs[t], s0[t]))
        vm = each(lambda t: stack(v_ref[0, sl, lanes_of(t)]))
        akv = each(lambda t: _dot(jnp.where(strict, g[t][:R2, R2:], 0.0), vm[t]))
        u = each(lambda t: a_s[t][:R2] + akv[t])
        p = each(lambda t: jnp.where(strict, g[t][:R2, :R2], 0.0))
        for it in range(n_doublings):
            pu = each(lambda t: _dot(p[t], u[t]))
            if it + 1 < n_doublings:
                p = each(lambda t: _dot(p[t], p[t]))
            u = each(lambda t: u[t] + pu[t])
        uv = each(lambda t: jnp.concatenate([u[t], vm[t]], axis=0))
        w_y = each(lambda t: jnp.concatenate([jnp.where(incl, g[t][R2:, :R2], 0.0),
                                              jnp.where(incl, g[t][R2:, R2:], 0.0)], axis=1))
        y_sm = each(lambda t: a_s[t][R2:] + _dot(w_y[t], uv[t]))
        bk_end = each(lambda t: jnp.concatenate(
            [stack(bv[t] * jnp.exp(tot[t] - cum[t])), stack(km[t] * jnp.exp(tot[t] - cum[t]))], axis=0))
        s1 = each(lambda t: s0[t] * jnp.exp(tot[t]) + _dot(uv[t].T, bk_end[t]))
        for t in tiles:
            y_s[sl, lanes_of(t)] = y_sm[t][:C] + y_sm[t][C:]
            state[t] = s1[t]
        return carry

    lax.fori_loop(0, n_chunks, chunk, 0)

    for t in range(n_tiles):
        ls = slice(t * LANES, (t + 1) * LANES)
        y = y_s[:, ls]
        mu = _dot_exact_rhs(y, head_ones) * inv_n
        d = y - mu
        var = _dot_exact_rhs(d * d, head_ones) * inv_n
        yn = d * lax.rsqrt(var + RWKV_GN_EPS) * hp_ref[3:4, ls] + hp_ref[4:5, ls]
        bonus = _dot_exact_rhs(r_ref[0, :, ls] * kmod_s[:, ls] * hp_ref[2:3, ls], head_ones)
        y_out[0, :, ls] = yn + bonus * v_ref[0, :, ls]


def _wkv(r, lw, k, v, a, head_params):
    B, S, D = r.shape
    ts = _row_tile(S, 512)
    assert ts % WKV_CHUNK == 0
    width = WKV_TILES_PER_STEP * LANES
    tile = pl.BlockSpec((1, ts, width), lambda b, p, i: (b, i, p))
    hp = pl.BlockSpec((SUBLANES, width), lambda b, p, i: (0, p))
    return pl.pallas_call(
        _wkv_kernel,
        grid=(B, D // width, S // ts),
        in_specs=[tile] * 5 + [hp],
        out_specs=tile,
        out_shape=jax.ShapeDtypeStruct((B, S, D), F32),
        scratch_shapes=[pltpu.VMEM((WKV_TILES_PER_STEP, LANES, LANES), F32)]
        + [pltpu.VMEM((ts, width), F32)] * 4,
        compiler_params=_params("parallel", "parallel", "arbitrary"),
    )(r, lw, k, v, a, head_params)


def _matmul_kernel(a_ref, w_ref, o_ref):
    o_ref[...] = _dot(a_ref[...], w_ref[...])


def _matmul(a, w):
    T, K = a.shape
    N = w.shape[1]
    tm = _row_tile(T, 512)
    return pl.pallas_call(
        _matmul_kernel,
        grid=(T // tm,),
        in_specs=[pl.BlockSpec((tm, K), lambda i: (i, 0)), pl.BlockSpec((K, N), lambda i: (0, 0))],
        out_specs=pl.BlockSpec((tm, N), lambda i: (i, 0)),
        out_shape=jax.ShapeDtypeStruct((T, N), F32),
        compiler_params=_params("parallel"),
    )(a, w)


def _to_row_tiles(ref, value):
    n = value.shape[1] // LANES
    for c in range(n):
        ref[pl.ds(c, value.shape[0], stride=n), :] = value[:, c * LANES:(c + 1) * LANES]


def _from_row_tiles(ref, n_rows, n):
    return jnp.concatenate([ref[pl.ds(c, n_rows, stride=n), :] for c in range(n)], axis=1)


def _proj_ln_kernel(has_gate, *refs):
    if has_gate:
        a_ref, g_ref, w_ref, x_ref, ln_ref, o_ref, o_tiles_ref = refs
        a = a_ref[...] * g_ref[...]
    else:
        a_ref, w_ref, x_ref, ln_ref, o_ref, o_tiles_ref = refs
        a = a_ref[...]
    z = DEEPNORM_ALPHA * x_ref[...] + _dot(a, w_ref[...])
    out = _layer_norm(z, ln_ref[0:1, :], ln_ref[1:2, :])
    o_ref[...] = out
    _to_row_tiles(o_tiles_ref, out)


def _proj_ln(a, gate, w, x, ln):
    T, D = x.shape
    tm = _row_tile(T, 512)
    n = D // LANES
    tile = pl.BlockSpec((tm, D), lambda i: (i, 0))
    full = lambda t: pl.BlockSpec(t.shape, lambda i: (0, 0))
    has_gate = gate is not None
    ins = [a] + ([gate] if has_gate else []) + [w, x, ln]
    specs = [tile] + ([tile] if has_gate else []) + [full(w), tile, full(ln)]
    return pl.pallas_call(
        functools.partial(_proj_ln_kernel, has_gate),
        grid=(T // tm,),
        in_specs=specs,
        out_specs=[tile, pl.BlockSpec((tm * n, LANES), lambda i: (i, 0))],
        out_shape=[jax.ShapeDtypeStruct((T, D), F32), jax.ShapeDtypeStruct((T * n, LANES), F32)],
        compiler_params=_params("parallel"),
    )(*ins)


def _max_rows(s, groups=4):
    step = s.shape[0] // groups
    part = s[0:step]
    for g in range(1, groups):
        part = jnp.maximum(part, s[g * step:(g + 1) * step])
    return jnp.max(part, axis=0, keepdims=True)


def _attn_kernel(out_scale, q_ref, k_ref, v_ref, slope_ref, lam_ref, g_ref, o_ref,
                 k_s, vt_s, acc_s):
    i = pl.program_id(2)
    tq = q_ref.shape[1]
    tk = tq
    n_heads = q_ref.shape[2] // LANES
    n_kv_blocks = k_ref.shape[1] // tk
    heads = range(n_heads)
    chains = [(h, c) for h in heads for c in range(2)]
    lanes_of = lambda h: slice(h * LANES, (h + 1) * LANES)

    @pl.when(i == 0)
    def _():
        lane = lax.broadcasted_iota(jnp.int32, (tk, LANES), 1)
        for h in heads:
            for blk in range(n_kv_blocks):
                sl = slice(blk * tk, (blk + 1) * tk)
                kb = k_ref[0, sl, lanes_of(h)]
                k_s[2 * h, sl, :] = jnp.where(lane < DA_HEAD_DIM, kb, 0.0).astype(BF16)
                k_s[2 * h + 1, sl, :] = jnp.where(lane < DA_HEAD_DIM, 0.0, kb).astype(BF16)
                vt_s[h, blk, 0:LANES, :] = v_ref[0, sl, lanes_of(h)].T.astype(BF16)
                vt_s[h, blk, LANES:, :] = jnp.ones((ATTN_ONES_ROWS, tk), BF16)

    q = [(q_ref[0, :, lanes_of(h)] * (DA_HEAD_DIM ** -0.5)).astype(BF16) for h in heads]
    slope = [slope_ref[0, :, h * LANES:h * LANES + 1] for h in heads]
    k_loc = lax.broadcasted_iota(jnp.int32, (tk, tq), 0)
    q_loc = lax.broadcasted_iota(jnp.int32, (tk, tq), 1)
    k_loc_f = k_loc.astype(F32)
    acc_s[...] = jnp.zeros_like(acc_s)

    def block(j, ms, diagonal):
        sl = pl.ds(pl.multiple_of(j * tk, tk), tk)
        rel = ((j - i) * tk).astype(F32)
        each = lambda f: [f(n, h) for n, (h, _) in enumerate(chains)]
        s = each(lambda n, h: _dot_nt(k_s[n, sl, :], q[h]) + slope[h] * k_loc_f)
        if diagonal:
            s = each(lambda n, h: jnp.where(k_loc > q_loc, NEG_BIG, s[n]))
        offset = [slope[h] * rel for h in heads]
        m_new = each(lambda n, h: jnp.maximum(ms[n], _max_rows(s[n]) + offset[h]))
        p = each(lambda n, h: jnp.exp(s[n] - (m_new[n] - offset[h])).astype(BF16))
        acc_old = each(lambda n, h: acc_s[n])
        pv = each(lambda n, h: jnp.dot(vt_s[h, j], p[n], preferred_element_type=F32))
        for n in range(len(chains)):
            acc_s[n] = jnp.exp(ms[n] - m_new[n]) * acc_old[n] + pv[n]
        return tuple(m_new)

    init = tuple(jnp.full((1, tq), NEG_BIG, F32) for _ in chains)
    ms = lax.fori_loop(0, i, lambda j, ms: block(j, ms, False), init)
    block(i, ms, True)

    for h in heads:
        acc0, acc1 = acc_s[2 * h], acc_s[2 * h + 1]
        o_t = (acc0[:LANES] / acc0[LANES:LANES + 1]
               - lam_ref[:, 0:1] * (acc1[:LANES] / acc1[LANES:LANES + 1]))
        o = o_t.T
        o = o * lax.rsqrt(jnp.mean(o * o, axis=-1, keepdims=True) + SUBLN_EPS) * g_ref[...]
        o_ref[0, :, lanes_of(h)] = o * out_scale


def _diff_attention(q, kv, slopes, lam_row, subln_g, lambda_init):
    B, S, D = q.shape
    tq = _row_tile(S, 256)
    nh = ATTN_HEADS_PER_STEP
    width = nh * LANES
    n_groups = DA_HEADS // nh
    return pl.pallas_call(
        functools.partial(_attn_kernel, 1.0 - lambda_init),
        grid=(B, n_groups, S // tq),
        in_specs=[pl.BlockSpec((1, tq, width), lambda b, h, i: (b, i, h)),
                  pl.BlockSpec((1, S, width), lambda b, h, i: (b, 0, h)),
                  pl.BlockSpec((1, S, width), lambda b, h, i: (b, 0, n_groups + h)),
                  pl.BlockSpec((1, 1, width), lambda b, h, i: (h, 0, 0)),
                  pl.BlockSpec((1, LANES), lambda b, h, i: (0, 0)),
                  pl.BlockSpec((1, LANES), lambda b, h, i: (0, 0))],
        out_specs=pl.BlockSpec((1, tq, width), lambda b, h, i: (b, i, h)),
        out_shape=jax.ShapeDtypeStruct((B, S, D), F32),
        scratch_shapes=[pltpu.VMEM((2 * nh, S, LANES), BF16),
                        pltpu.VMEM((nh, S // tq, LANES + ATTN_ONES_ROWS, tq), BF16),
                        pltpu.VMEM((2 * nh, LANES + ATTN_ONES_ROWS, tq), F32)],
        compiler_params=_params("parallel", "parallel", "arbitrary"),
    )(q, kv, kv, slopes.reshape(n_groups, 1, width), lam_row, subln_g)


def _router_kernel(x_ref, w_ref, b_ref, idx_out, gate_out):
    x = x_ref[...]
    xh, xm_, xl = _split3(x)
    w = w_ref[...]
    f = lambda p, q: jnp.dot(p, q, preferred_element_type=F32)
    logits = (f(xh, w[0]) + (f(xh, w[1]) + f(xm_, w[0]))
              + (f(xh, w[2]) + f(xm_, w[1]) + f(xl, w[0]))) + b_ref[...]
    lane = lax.broadcasted_iota(jnp.int32, logits.shape, 1)
    vals, idxs = [], []
    cur = logits
    for _ in range(TOP_K):
        m = jnp.max(cur, axis=-1, keepdims=True)
        idx = jnp.min(jnp.where(cur == m, lane, LANES), axis=-1, keepdims=True)
        cur = jnp.where(lane == idx, -jnp.inf, cur)
        vals.append(m)
        idxs.append(idx)
    es = [jnp.exp(v - vals[0]) for v in vals]
    denom = es[0] + es[1] + es[2] + es[3]
    idx_row = jnp.zeros(logits.shape, jnp.int32)
    gate_row = jnp.zeros(logits.shape, F32)
    for kx in range(TOP_K):
        idx_row = jnp.where(lane == kx, idxs[kx], idx_row)
        gate_row = jnp.where(lane == kx, es[kx] / denom, gate_row)
    idx_out[...] = idx_row
    gate_out[...] = gate_row


def _router(x, w3, b_row):
    T, D = x.shape
    tm = _row_tile(T, 512)
    tile = pl.BlockSpec((tm, D), lambda i: (i, 0))
    out_tile = pl.BlockSpec((tm, LANES), lambda i: (i, 0))
    return pl.pallas_call(
        _router_kernel,
        grid=(T // tm,),
        in_specs=[tile, pl.BlockSpec(w3.shape, lambda i: (0, 0, 0)),
                  pl.BlockSpec(b_row.shape, lambda i: (0, 0))],
        out_specs=[out_tile, out_tile],
        out_shape=[jax.ShapeDtypeStruct((T, LANES), jnp.int32),
                   jax.ShapeDtypeStruct((T, LANES), F32)],
        compiler_params=_params("parallel"),
    )(x, w3, b_row)


def _clamped_swiglu(h):
    gate, up = h[:, :EXPERT_FF], h[:, EXPERT_FF:]
    gate = jnp.minimum(gate, SWIGLU_LIMIT)
    up = jnp.clip(up, -SWIGLU_LIMIT, SWIGLU_LIMIT)
    return (up + 1.0) * (gate * jax.nn.sigmoid(SWIGLU_ALPHA * gate))


def _moe_kernel(bexp_ref, g_first, g_b, g_next, s_prev, s_a, s_b, x_hbm,
                wgu_a, bgu_a, wdn_a, bdn_a, wgu_b, bgu_b, wdn_b, bdn_b, y_hbm,
                xbuf_a, xbuf_b, ybuf_a, ybuf_b, sem_ga, sem_gb, sem_sa, sem_sb):
    del bexp_ref
    g = pl.program_id(0)
    last = pl.num_programs(0) - 1

    n = wdn_a.shape[2] // LANES
    tile_of = lambda r: pl.ds(pl.multiple_of(r, n), n)

    def gather(idx_ref, xbuf, sem, j):
        return pltpu.make_async_copy(x_hbm.at[tile_of(idx_ref[0, 0, j])], xbuf.at[pl.ds(j * n, n)], sem)

    def scatter(idx_ref, ybuf, sem, j):
        return pltpu.make_async_copy(ybuf.at[pl.ds(j * n, n)], y_hbm.at[tile_of(idx_ref[0, 0, j])], sem)

    def wait_gather(xbuf, sem):
        pltpu.make_async_copy(x_hbm.at[pl.ds(0, EXPERT_BLOCK * n)], xbuf, sem).wait()

    def wait_scatter(ybuf, sem):
        pltpu.make_async_copy(ybuf, y_hbm.at[pl.ds(0, EXPERT_BLOCK * n)], sem).wait()

    def expert_mlp(xbuf, ybuf, wgu, bgu, wdn, bdn):
        h = _dot(_from_row_tiles(xbuf, EXPERT_BLOCK, n), wgu[0]) + bgu[0]
        _to_row_tiles(ybuf, _dot(_clamped_swiglu(h), wdn[0]) + bdn[0])

    @pl.when(g == 0)
    def _():
        ybuf_b[...] = jnp.zeros_like(ybuf_b)

        def first(j, c):
            gather(g_first, xbuf_a, sem_ga, j).start()
            return c
        lax.fori_loop(0, EXPERT_BLOCK, first, 0)

    wait_gather(xbuf_a, sem_ga)

    @pl.when(g > 0)
    def _():
        wait_scatter(ybuf_a, sem_sa)

    for j in range(EXPERT_BLOCK):
        gather(g_b, xbuf_b, sem_gb, j).start()
        scatter(s_prev, ybuf_b, sem_sb, j).start()
    expert_mlp(xbuf_a, ybuf_a, wgu_a, bgu_a, wdn_a, bdn_a)

    wait_gather(xbuf_b, sem_gb)
    wait_scatter(ybuf_b, sem_sb)

    for j in range(EXPERT_BLOCK):
        gather(g_next, xbuf_a, sem_ga, j).start()
        scatter(s_a, ybuf_a, sem_sa, j).start()
    expert_mlp(xbuf_b, ybuf_b, wgu_b, bgu_b, wdn_b, bdn_b)

    @pl.when(g == last)
    def _():
        def final(j, c):
            scatter(s_b, ybuf_b, sem_sb, j).start()
            return c
        lax.fori_loop(0, EXPERT_BLOCK, final, 0)
        wait_gather(xbuf_a, sem_ga)
        wait_scatter(ybuf_a, sem_sa)
        wait_scatter(ybuf_b, sem_sb)


def _moe_experts(x_tiles, gather_tok, scatter_row, block_exp, n_out_rows, w_gu, b_gu, w_dn, b_dn):
    D = w_gu.shape[1]
    n = D // LANES
    n_blocks = block_exp.shape[0]
    assert n_blocks % 2 == 0
    FF2 = w_gu.shape[2]
    idx_spec = lambda fn: pl.BlockSpec((1, 1, EXPERT_BLOCK), fn, memory_space=pltpu.SMEM)
    weights = lambda off: [
        pl.BlockSpec((1, D, FF2), lambda g, be: (be[2 * g + off], 0, 0)),
        pl.BlockSpec((1, 1, FF2), lambda g, be: (be[2 * g + off], 0, 0)),
        pl.BlockSpec((1, EXPERT_FF, D), lambda g, be: (be[2 * g + off], 0, 0)),
        pl.BlockSpec((1, 1, D), lambda g, be: (be[2 * g + off], 0, 0)),
    ]
    grid_spec = pltpu.PrefetchScalarGridSpec(
        num_scalar_prefetch=1,
        grid=(n_blocks // 2,),
        in_specs=[
            idx_spec(lambda g, be: (0, 0, 0)),
            idx_spec(lambda g, be: (2 * g + 1, 0, 0)),
            idx_spec(lambda g, be: (2 * g + 2, 0, 0)),
            idx_spec(lambda g, be: (2 * g, 0, 0)),
            idx_spec(lambda g, be: (2 * g + 1, 0, 0)),
            idx_spec(lambda g, be: (2 * g + 2, 0, 0)),
            pl.BlockSpec(memory_space=pl.ANY),
        ] + weights(0) + weights(1),
        out_specs=pl.BlockSpec(memory_space=pl.ANY),
        scratch_shapes=[pltpu.VMEM((EXPERT_BLOCK * n, LANES), F32)] * 4
        + [pltpu.SemaphoreType.DMA(())] * 4,
    )
    b_gu = b_gu.reshape(N_EXPERTS, 1, FF2)
    b_dn = b_dn.reshape(N_EXPERTS, 1, D)
    return pl.pallas_call(
        _moe_kernel,
        grid_spec=grid_spec,
        out_shape=jax.ShapeDtypeStruct((n_out_rows * n, LANES), F32),
        compiler_params=_params("arbitrary"),
    )(block_exp, gather_tok, gather_tok, gather_tok, scatter_row, scatter_row, scatter_row, x_tiles,
      w_gu, b_gu, w_dn, b_dn, w_gu, b_gu, w_dn, b_dn)


def _combine_ln_kernel(y0_ref, y1_ref, y2_ref, y3_ref, gate_ref, x_ref, ln_ref, o_ref):
    gate = gate_ref[...]
    tm, D = x_ref.shape
    ffn = jnp.zeros(x_ref.shape, F32)
    for kx, y_ref in enumerate((y0_ref, y1_ref, y2_ref, y3_ref)):
        ffn = ffn + _from_row_tiles(y_ref, tm, D // LANES) * gate[:, kx:kx + 1]
    z = DEEPNORM_ALPHA * x_ref[...] + ffn
    o_ref[...] = _layer_norm(z, ln_ref[0:1, :], ln_ref[1:2, :])


def _combine_ln(y_rows, gate_rows, x, ln):
    T, D = x.shape
    tm = _row_tile(T, 512)
    tile = pl.BlockSpec((tm, D), lambda i: (i, 0))
    slot = lambda kx: pl.BlockSpec((tm * (D // LANES), LANES), lambda i: (kx * (T // tm) + i, 0))
    return pl.pallas_call(
        _combine_ln_kernel,
        grid=(T // tm,),
        in_specs=[slot(kx) for kx in range(TOP_K)]
        + [pl.BlockSpec((tm, LANES), lambda i: (i, 0)), tile, pl.BlockSpec(ln.shape, lambda i: (0, 0))],
        out_specs=tile,
        out_shape=jax.ShapeDtypeStruct((T, D), F32),
        compiler_params=_params("parallel"),
    )(y_rows, y_rows, y_rows, y_rows, gate_rows, x, ln)


def _routing_plan(top_idx):
    T = top_idx.shape[0]
    TK = T * TOP_K
    flat_e = top_idx.reshape(TK)
    order = jnp.argsort(flat_e).astype(jnp.int32)
    counts = jnp.bincount(flat_e, length=N_EXPERTS).astype(jnp.int32)
    padded = (counts + EXPERT_BLOCK - 1) // EXPERT_BLOCK * EXPERT_BLOCK
    pad_end = jnp.cumsum(padded)
    pad_start = pad_end - padded
    grp_start = jnp.cumsum(counts) - counts
    n_blocks = -(-TK // EXPERT_BLOCK) + N_EXPERTS
    n_rows = n_blocks * EXPERT_BLOCK
    block_start = jnp.arange(n_blocks, dtype=jnp.int32) * EXPERT_BLOCK
    block_exp = jnp.minimum(jnp.searchsorted(pad_end, block_start, side='right'),
                            N_EXPERTS - 1).astype(jnp.int32)
    pos = jnp.arange(n_rows, dtype=jnp.int32)
    e_of = jnp.repeat(block_exp, EXPERT_BLOCK)
    off = pos - pad_start[e_of]
    valid = (off >= 0) & (off < counts[e_of])
    pair = order[jnp.clip(grp_start[e_of] + off, 0, TK - 1)]
    tok, slot = pair // TOP_K, pair % TOP_K
    spare = TK + jnp.cumsum(jnp.logical_not(valid).astype(jnp.int32)) - 1
    gather_tok = jnp.where(valid, tok, 0).astype(jnp.int32)
    scatter_row = jnp.where(valid, slot * T + tok, spare).astype(jnp.int32)
    fill = jnp.arange(EXPERT_BLOCK, dtype=jnp.int32)
    gather_tok = jnp.concatenate([gather_tok, jnp.zeros_like(fill)])
    scatter_row = jnp.concatenate([n_rows + fill, scatter_row])
    shape = (n_blocks + 1, 1, EXPERT_BLOCK)
    scale = D_MODEL // LANES
    return ((gather_tok * scale).reshape(shape), (scatter_row * scale).reshape(shape), block_exp,
            n_rows + EXPERT_BLOCK)


def _moe_ffn_ln(x, x_tiles, router_w, router_b, w_gu, b_gu, w_dn, b_dn, ln):
    T, D = x.shape
    w_pad = jnp.zeros((D, LANES), F32).at[:, :N_EXPERTS].set(router_w)
    w3 = jnp.stack(_split3(w_pad))
    b_row = jnp.full((1, LANES), NEG_BIG, F32).at[0, :N_EXPERTS].set(router_b)
    idx_rows, gate_rows = _router(x, w3, b_row)
    gather_tok, scatter_row, block_exp, n_out_rows = _routing_plan(idx_rows[:, :TOP_K])
    y_rows = _moe_experts(x_tiles, gather_tok, scatter_row, block_exp, n_out_rows,
                          w_gu, b_gu, w_dn, b_dn)
    return _combine_ln(y_rows, gate_rows, x, ln)


def _cast_kernel(w_ref, o_ref):
    o_ref[...] = w_ref[...].astype(BF16)


def _cast_bf16(w):
    lead = w.shape[:-2]
    w3 = w.reshape((-1,) + w.shape[-2:])
    spec = pl.BlockSpec((1,) + w3.shape[1:], lambda i: (i, 0, 0))
    out = pl.pallas_call(
        _cast_kernel,
        grid=(w3.shape[0],),
        in_specs=[spec],
        out_specs=spec,
        out_shape=jax.ShapeDtypeStruct(w3.shape, BF16),
        compiler_params=_params("parallel"),
    )(w3)
    return out.reshape(lead + w.shape[-2:])


def _pad_rows(rows, n=SUBLANES):
    out = jnp.zeros((n, rows[0].shape[-1]), F32)
    for j, r in enumerate(rows):
        out = out.at[j].set(r.reshape(-1))
    return out


def kernel(x, ln_g, ln_b, rwkv_mix, rwkv_w_rkv, rwkv_w_o, rwkv_w0, rwkv_w1, rwkv_w2, rwkv_a0, rwkv_a1, rwkv_a2, rwkv_g1, rwkv_g2, rwkv_k_k, rwkv_k_a, rwkv_r_k, rwkv_lnx_g, rwkv_lnx_b, rwkv_v0, rwkv_v1, rwkv_v2, kv_w, da_w_q, da_w_o, da_lambda, da_subln_g, moe_router_w, moe_router_b, moe_w_gu, moe_b_gu, moe_w_dn, moe_b_dn):
    B, S, D = x.shape
    T = B * S
    v_first = None
    kv = None
    slopes = 2.0 ** (-8.0 * jnp.arange(1, DA_HEADS + 1, dtype=F32) / DA_HEADS)
    slopes = jnp.repeat(slopes, LANES).reshape(1, DA_HEADS * LANES)
    w_gu_bf16 = _cast_bf16(moe_w_gu)
    w_dn_bf16 = _cast_bf16(moe_w_dn)
    for l in range(DEPTH):
        ln_mix = jnp.stack([ln_g[l, 0], ln_b[l, 0]])
        ln_ffn = jnp.stack([ln_g[l, 1], ln_b[l, 1]])
        xt = x.reshape(T, D)
        if l < N_A_LAYERS:
            zero = jnp.zeros((D,), F32)
            vecs = _pad_rows([rwkv_w0[l], rwkv_a0[l], rwkv_v0[l - 1] if l > 0 else zero])
            vres = None if l == 0 else (rwkv_v1[l - 1].astype(BF16), rwkv_v2[l - 1].astype(BF16), v_first)
            r, lw, k, v, a, g = _rwkv_proj(
                x, _pad_rows(list(rwkv_mix[l])), rwkv_w_rkv[l].astype(BF16),
                rwkv_w1[l].astype(BF16), rwkv_w2[l].astype(BF16), rwkv_a1[l].astype(BF16),
                rwkv_a2[l].astype(BF16), rwkv_g1[l].astype(BF16), rwkv_g2[l].astype(BF16), vecs, vres)
            if l == 0:
                v_first = v
            head_params = _pad_rows([rwkv_k_k[l], rwkv_k_a[l],
                                     jnp.tile(rwkv_r_k[l].reshape(-1), 1), rwkv_lnx_g[l], rwkv_lnx_b[l]])
            y = _wkv(r, lw, k, v, a, head_params)
            xt, xt_tiles = _proj_ln(y.reshape(T, D), g.reshape(T, D), rwkv_w_o[l].astype(BF16), xt, ln_mix)
        else:
            if l == N_A_LAYERS:
                kv = _matmul(xt, kv_w.astype(BF16)).reshape(B, S, 2 * D)
            j = l - N_A_LAYERS
            lambda_init = 0.8 - 0.6 * math.exp(-0.3 * l)
            lam = da_lambda[j].astype(F32)
            lam_full = (jnp.exp(jnp.sum(lam[0] * lam[1])) - jnp.exp(jnp.sum(lam[2] * lam[3]))
                        + lambda_init)
            q = _matmul(xt, da_w_q[j].astype(BF16)).reshape(B, S, D)
            o = _diff_attention(q, kv, slopes, jnp.full((1, LANES), lam_full, F32),
                                da_subln_g[j].reshape(1, LANES), lambda_init)
            xt, xt_tiles = _proj_ln(o.reshape(T, D), None, da_w_o[j].astype(BF16), xt, ln_mix)
        xt = _moe_ffn_ln(xt, xt_tiles, moe_router_w[l], moe_router_b[l], w_gu_bf16[l], moe_b_gu[l],
                         w_dn_bf16[l], moe_b_dn[l], ln_ffn)
        x = xt.reshape(B, S, D)
    return x
```

```python
import functools
import math

import jax
import jax.numpy as jnp
from jax import lax
from jax.experimental import pallas as pl
from jax.experimental.pallas import tpu as pltpu

F32 = jnp.float32
BF16 = jnp.bfloat16

D_MODEL = 1024
DEPTH = 4
N_A_LAYERS = DEPTH // 2
RWKV_HEAD_DIM = 64
RWKV_GN_EPS = 64e-5
L2_EPS = 1e-12
DA_HEAD_DIM = 64
DA_HEADS = D_MODEL // (2 * DA_HEAD_DIM)
SUBLN_EPS = 1e-5
N_EXPERTS = 32
TOP_K = 4
EXPERT_FF = D_MODEL
SWIGLU_LIMIT = 7.0
SWIGLU_ALPHA = 1.702
EXPERT_BLOCK = 512
DEEPNORM_ALPHA = (2 * DEPTH) ** 0.25
LN_EPS = 1e-5

LANES = 128
SUBLANES = 8
VMEM_LIMIT_BYTES = 56 * 1024 * 1024
DMA_PRIORITIES = 2

WKV_CHUNK = 64
WKV_TILES_PER_STEP = 8
HEADS_PER_TILE = LANES // RWKV_HEAD_DIM
NEG_BIG = -1e30
ATTN_ONES_ROWS = 16
ATTN_HEADS_PER_STEP = 4


def _params(*sem):
    return pltpu.CompilerParams(dimension_semantics=sem, vmem_limit_bytes=VMEM_LIMIT_BYTES)


def _dot(a, b):
    return jnp.dot(a.astype(BF16), b.astype(BF16), preferred_element_type=F32)


def _dot_nt(a, b):
    return lax.dot_general(a.astype(BF16), b.astype(BF16), (((1,), (1,)), ((), ())),
                           preferred_element_type=F32)


def _split3(x):
    hi = x.astype(BF16)
    r1 = x - hi.astype(F32)
    mid = r1.astype(BF16)
    lo = (r1 - mid.astype(F32)).astype(BF16)
    return hi, mid, lo


def _split2(x):
    hi = x.astype(BF16)
    return hi, (x - hi.astype(F32)).astype(BF16)


def _dot_exact_lhs(m_bf16, x):
    hi, lo = _split2(x)
    return (jnp.dot(m_bf16, hi, preferred_element_type=F32)
            + jnp.dot(m_bf16, lo, preferred_element_type=F32))


def _layer_norm(z, g, b):
    mu = jnp.mean(z, axis=-1, keepdims=True)
    d = z - mu
    var = jnp.mean(d * d, axis=-1, keepdims=True)
    return d * lax.rsqrt(var + LN_EPS) * g + b


def _row_tile(n, want):
    t = min(want, n)
    assert n % t == 0 and t % SUBLANES == 0
    return t


def _rwkv_proj_kernel(has_vres, *refs):
    if has_vres:
        (x_ref, xp_ref, mix_ref, wrkv_ref, w1_ref, w2_ref, a1_ref, a2_ref, g1_ref, g2_ref,
         vec_ref, v1_ref, v2_ref, vfirst_ref,
         r_out, lw_out, k_out, v_out, a_out, g_out) = refs
    else:
        (x_ref, xp_ref, mix_ref, wrkv_ref, w1_ref, w2_ref, a1_ref, a2_ref, g1_ref, g2_ref,
         vec_ref, r_out, lw_out, k_out, v_out, a_out, g_out) = refs
    i = pl.program_id(1)
    x = x_ref[0]
    prev = xp_ref[0][SUBLANES - 1:SUBLANES, :]
    prev = jnp.where(i == 0, 0.0, prev)
    row = lax.broadcasted_iota(jnp.int32, x.shape, 0)
    x_prev = jnp.where(row == 0, prev, pltpu.roll(x, 1, axis=0))
    xx = x_prev - x

    def xm(j):
        return x + xx * mix_ref[j:j + 1, :]

    xv = xm(2)
    r_out[0] = _dot(xm(0), wrkv_ref[0])
    k_out[0] = _dot(xm(1), wrkv_ref[1])
    v = _dot(xv, wrkv_ref[2])
    wl = vec_ref[0:1, :] + _dot(jnp.tanh(_dot(xm(3), w1_ref[...])), w2_ref[...])
    lw_out[0] = -jax.nn.sigmoid(wl) * math.exp(-0.5)
    a_out[0] = jax.nn.sigmoid(vec_ref[1:2, :] + _dot(_dot(xm(4), a1_ref[...]), a2_ref[...]))
    g_out[0] = _dot(jax.nn.sigmoid(_dot(xm(5), g1_ref[...])), g2_ref[...])
    if has_vres:
        mixv = jax.nn.sigmoid(vec_ref[2:3, :] + _dot(_dot(xv, v1_ref[...]), v2_ref[...]))
        v = v + (vfirst_ref[0] - v) * mixv
    v_out[0] = v


def _rwkv_proj(x, mix, w_rkv, w1, w2, a1, a2, g1, g2, vecs, vres):
    B, S, D = x.shape
    tm = _row_tile(S, 512)
    has_vres = vres is not None
    full = lambda a: pl.BlockSpec(a.shape, lambda b, i: (0,) * a.ndim)
    tile = pl.BlockSpec((1, tm, D), lambda b, i: (b, i, 0))
    prev = pl.BlockSpec((1, SUBLANES, D),
                        lambda b, i: (b, jnp.maximum(i * (tm // SUBLANES) - 1, 0), 0))
    ins = [x, x, mix, w_rkv, w1, w2, a1, a2, g1, g2, vecs]
    specs = [tile, prev] + [full(a) for a in ins[2:]]
    if has_vres:
        v1, v2, v_first = vres
        ins += [v1, v2, v_first]
        specs += [full(v1), full(v2), tile]
    out = jax.ShapeDtypeStruct((B, S, D), F32)
    return pl.pallas_call(
        functools.partial(_rwkv_proj_kernel, has_vres),
        grid=(B, S // tm),
        in_specs=specs,
        out_specs=[tile] * 6,
        out_shape=[out] * 6,
        compiler_params=_params("parallel", "arbitrary"),
    )(*ins)


def _wkv_kernel(r_ref, lw_ref, k_ref, v_ref, a_ref, hp_ref, y_out,
                state, avec_s, bvec_s, kmod_s, y_s):
    C = WKV_CHUNK
    ts = r_ref.shape[1]
    n_tiles = r_ref.shape[2] // LANES
    n_chunks = ts // C
    R2 = HEADS_PER_TILE * C
    inv_n = 1.0 / RWKV_HEAD_DIM

    @pl.when(pl.program_id(2) == 0)
    def _():
        state[...] = jnp.zeros_like(state)

    li = lax.broadcasted_iota(jnp.int32, (LANES, LANES), 0)
    lj = lax.broadcasted_iota(jnp.int32, (LANES, LANES), 1)
    head_ones = (li // RWKV_HEAD_DIM == lj // RWKV_HEAD_DIM).astype(BF16)

    for t in range(n_tiles):
        ls = slice(t * LANES, (t + 1) * LANES)
        k = k_ref[0, :, ls]
        a = a_ref[0, :, ls]
        kk = k * hp_ref[0:1, ls]
        n2 = _dot(kk * kk, head_ones)
        kk = kk / jnp.maximum(jnp.sqrt(n2), L2_EPS)
        avec_s[:, ls] = -kk
        bvec_s[:, ls] = kk * a
        kmod_s[:, ls] = k * (1.0 + (a - 1.0) * hp_ref[1:2, ls])

    ri = lax.broadcasted_iota(jnp.int32, (R2, R2), 0)
    rj = lax.broadcasted_iota(jnp.int32, (R2, R2), 1)
    same_head = (ri // C) == (rj // C)
    strict = same_head & ((ri % C) > (rj % C))
    incl = same_head & ((ri % C) >= (rj % C))
    ci = lax.broadcasted_iota(jnp.int32, (C, C), 0)
    cj = lax.broadcasted_iota(jnp.int32, (C, C), 1)
    tri = (ci >= cj).astype(BF16)
    lane = lax.broadcasted_iota(jnp.int32, (C, LANES), 1)
    head0 = lane < RWKV_HEAD_DIM
    n_doublings = int(math.log2(C))

    def stack(t):
        return jnp.concatenate([jnp.where(head0, t, 0.0), jnp.where(head0, 0.0, t)], axis=0)

    tiles = range(n_tiles)
    lanes_of = lambda t: slice(t * LANES, (t + 1) * LANES)

    def chunk(c, carry):
        sl = pl.ds(pl.multiple_of(c * C, C), C)
        each = lambda f: [f(t) for t in tiles]
        s0 = each(lambda t: state[t])
        lw = each(lambda t: lw_ref[0, sl, lanes_of(t)])
        cum = each(lambda t: _dot_exact_lhs(tri, lw[t]))
        tot = each(lambda t: cum[t][C - 1:C, :])
        e_neg = each(lambda t: jnp.exp(-cum[t]))
        av = each(lambda t: avec_s[sl, lanes_of(t)])
        bv = each(lambda t: bvec_s[sl, lanes_of(t)])
        km = each(lambda t: kmod_s[sl, lanes_of(t)])
        lhs = each(lambda t: jnp.concatenate(
            [stack(av[t] * jnp.exp(cum[t] - lw[t])),
             stack(r_ref[0, sl, lanes_of(t)] * jnp.exp(cum[t]))], axis=0))
        rhs = each(lambda t: jnp.concatenate([stack(bv[t] * e_neg[t]), stack(km[t] * e_neg[t])], axis=0))
        g = each(lambda t: _dot_nt(lhs[t], rhs[t]))
        a_s = each(lambda t: _dot_nt(lhs[t], s0[t]))
        vm = each(lambda t: stack(v_ref[0, sl, lanes_of(t)]))
        akv = each(lambda t: _dot(jnp.where(strict, g[t][:R2, R2:], 0.0), vm[t]))
        u = each(lambda t: a_s[t][:R2] + akv[t])
        p = each(lambda t: jnp.where(strict, g[t][:R2, :R2], 0.0))
        for it in range(n_doublings):
            pu = each(lambda t: _dot(p[t], u[t]))
            if it + 1 < n_doublings:
                p = each(lambda t: _dot(p[t], p[t]))
            u = each(lambda t: u[t] + pu[t])
        uv = each(lambda t: jnp.concatenate([u[t], vm[t]], axis=0))
        w_y = each(lambda t: jnp.concatenate([jnp.where(incl, g[t][R2:, :R2], 0.0),
                                              jnp.where(incl, g[t][R2:, R2:], 0.0)], axis=1))
        y_sm = each(lambda t: a_s[t][R2:] + _dot(w_y[t], uv[t]))
        bk_end = each(lambda t: jnp.concatenate(
            [stack(bv[t] * jnp.exp(tot[t] - cum[t])), stack(km[t] * jnp.exp(tot[t] - cum[t]))], axis=0))
        s1 = each(lambda t: s0[t] * jnp.exp(tot[t]) + _dot(uv[t].T, bk_end[t]))
        for t in tiles:
            y_s[sl, lanes_of(t)] = y_sm[t][:C] + y_sm[t][C:]
            state[t] = s1[t]
        return carry

    lax.fori_loop(0, n_chunks, chunk, 0)

    for t in range(n_tiles):
        ls = slice(t * LANES, (t + 1) * LANES)
        y = y_s[:, ls]
        mu = _dot(y, head_ones) * inv_n
        d = y - mu
        var = _dot(d * d, head_ones) * inv_n
        yn = d * lax.rsqrt(var + RWKV_GN_EPS) * hp_ref[3:4, ls] + hp_ref[4:5, ls]
        bonus = _dot(r_ref[0, :, ls] * kmod_s[:, ls] * hp_ref[2:3, ls], head_ones)
        y_out[0, :, ls] = yn + bonus * v_ref[0, :, ls]


def _wkv(r, lw, k, v, a, head_params):
    B, S, D = r.shape
    ts = _row_tile(S, 512)
    assert ts % WKV_CHUNK == 0
    width = WKV_TILES_PER_STEP * LANES
    tile = pl.BlockSpec((1, ts, width), lambda b, p, i: (b, i, p))
    hp = pl.BlockSpec((SUBLANES, width), lambda b, p, i: (0, p))
    return pl.pallas_call(
        _wkv_kernel,
        grid=(B, D // width, S // ts),
        in_specs=[tile] * 5 + [hp],
        out_specs=tile,
        out_shape=jax.ShapeDtypeStruct((B, S, D), F32),
        scratch_shapes=[pltpu.VMEM((WKV_TILES_PER_STEP, LANES, LANES), F32)]
        + [pltpu.VMEM((ts, width), F32)] * 4,
        compiler_params=_params("parallel", "parallel", "arbitrary"),
    )(r, lw, k, v, a, head_params)


def _matmul_kernel(a_ref, w_ref, o_ref):
    o_ref[...] = _dot(a_ref[...], w_ref[...])


def _matmul(a, w):
    T, K = a.shape
    N = w.shape[1]
    tm = _row_tile(T, 512)
    return pl.pallas_call(
        _matmul_kernel,
        grid=(T // tm,),
        in_specs=[pl.BlockSpec((tm, K), lambda i: (i, 0)), pl.BlockSpec((K, N), lambda i: (0, 0))],
        out_specs=pl.BlockSpec((tm, N), lambda i: (i, 0)),
        out_shape=jax.ShapeDtypeStruct((T, N), F32),
        compiler_params=_params("parallel"),
    )(a, w)


def _to_row_tiles(ref, value):
    n = value.shape[1] // LANES
    for c in range(n):
        ref[pl.ds(c, value.shape[0], stride=n), :] = value[:, c * LANES:(c + 1) * LANES]


def _from_row_tiles(ref, n_rows, n):
    return jnp.concatenate([ref[pl.ds(c, n_rows, stride=n), :] for c in range(n)], axis=1)


def _proj_ln_kernel(has_gate, *refs):
    if has_gate:
        a_ref, g_ref, w_ref, x_ref, ln_ref, o_ref, o_tiles_ref = refs
        a = a_ref[...] * g_ref[...]
    else:
        a_ref, w_ref, x_ref, ln_ref, o_ref, o_tiles_ref = refs
        a = a_ref[...]
    z = DEEPNORM_ALPHA * x_ref[...] + _dot(a, w_ref[...])
    out = _layer_norm(z, ln_ref[0:1, :], ln_ref[1:2, :])
    o_ref[...] = out
    _to_row_tiles(o_tiles_ref, out)


def _proj_ln(a, gate, w, x, ln):
    T, D = x.shape
    tm = _row_tile(T, 512)
    n = D // LANES
    tile = pl.BlockSpec((tm, D), lambda i: (i, 0))
    full = lambda t: pl.BlockSpec(t.shape, lambda i: (0, 0))
    has_gate = gate is not None
    ins = [a] + ([gate] if has_gate else []) + [w, x, ln]
    specs = [tile] + ([tile] if has_gate else []) + [full(w), tile, full(ln)]
    return pl.pallas_call(
        functools.partial(_proj_ln_kernel, has_gate),
        grid=(T // tm,),
        in_specs=specs,
        out_specs=[tile, pl.BlockSpec((tm * n, LANES), lambda i: (i, 0))],
        out_shape=[jax.ShapeDtypeStruct((T, D), F32), jax.ShapeDtypeStruct((T * n, LANES), F32)],
        compiler_params=_params("parallel"),
    )(*ins)


def _max_rows(s, groups=4):
    step = s.shape[0] // groups
    part = s[0:step]
    for g in range(1, groups):
        part = jnp.maximum(part, s[g * step:(g + 1) * step])
    return jnp.max(part, axis=0, keepdims=True)


def _attn_kernel(out_scale, q_ref, k_ref, v_ref, slope_ref, lam_ref, g_ref, o_ref,
                 k_s, vt_s, acc_s):
    i = pl.program_id(2)
    tq = q_ref.shape[1]
    tk = tq
    n_heads = q_ref.shape[2] // LANES
    n_kv_blocks = k_ref.shape[1] // tk
    heads = range(n_heads)
    chains = [(h, c) for h in heads for c in range(2)]
    lanes_of = lambda h: slice(h * LANES, (h + 1) * LANES)

    @pl.when(i == 0)
    def _():
        lane = lax.broadcasted_iota(jnp.int32, (tk, LANES), 1)
        for h in heads:
            for blk in range(n_kv_blocks):
                sl = slice(blk * tk, (blk + 1) * tk)
                kb = k_ref[0, sl, lanes_of(h)]
                k_s[2 * h, sl, :] = jnp.where(lane < DA_HEAD_DIM, kb, 0.0).astype(BF16)
                k_s[2 * h + 1, sl, :] = jnp.where(lane < DA_HEAD_DIM, 0.0, kb).astype(BF16)
                vt_s[h, blk, 0:LANES, :] = v_ref[0, sl, lanes_of(h)].T.astype(BF16)
                vt_s[h, blk, LANES:, :] = jnp.ones((ATTN_ONES_ROWS, tk), BF16)

    q = [(q_ref[0, :, lanes_of(h)] * (DA_HEAD_DIM ** -0.5)).astype(BF16) for h in heads]
    slope = [slope_ref[0, :, h * LANES:h * LANES + 1] for h in heads]
    k_loc = lax.broadcasted_iota(jnp.int32, (tk, tq), 0)
    q_loc = lax.broadcasted_iota(jnp.int32, (tk, tq), 1)
    k_loc_f = k_loc.astype(F32)
    acc_s[...] = jnp.zeros_like(acc_s)

    def block(j, ms, diagonal):
        sl = pl.ds(pl.multiple_of(j * tk, tk), tk)
        rel = ((j - i) * tk).astype(F32)
        each = lambda f: [f(n, h) for n, (h, _) in enumerate(chains)]
        s = each(lambda n, h: _dot_nt(k_s[n, sl, :], q[h]) + slope[h] * k_loc_f)
        if diagonal:
            s = each(lambda n, h: jnp.where(k_loc > q_loc, NEG_BIG, s[n]))
        offset = [slope[h] * rel for h in heads]
        m_new = each(lambda n, h: jnp.maximum(ms[n], _max_rows(s[n]) + offset[h]))
        p = each(lambda n, h: jnp.exp(s[n] - (m_new[n] - offset[h])).astype(BF16))
        acc_old = each(lambda n, h: acc_s[n])
        pv = each(lambda n, h: jnp.dot(vt_s[h, j], p[n], preferred_element_type=F32))
        for n in range(len(chains)):
            acc_s[n] = jnp.exp(ms[n] - m_new[n]) * acc_old[n] + pv[n]
        return tuple(m_new)

    init = tuple(jnp.full((1, tq), NEG_BIG, F32) for _ in chains)
    ms = lax.fori_loop(0, i, lambda j, ms: block(j, ms, False), init)
    block(i, ms, True)

    for h in heads:
        acc0, acc1 = acc_s[2 * h], acc_s[2 * h + 1]
        o_t = (acc0[:LANES] / acc0[LANES:LANES + 1]
               - lam_ref[:, 0:1] * (acc1[:LANES] / acc1[LANES:LANES + 1]))
        o = o_t.T
        o = o * lax.rsqrt(jnp.mean(o * o, axis=-1, keepdims=True) + SUBLN_EPS) * g_ref[...]
        o_ref[0, :, lanes_of(h)] = o * out_scale


def _diff_attention(q, kv, slopes, lam_row, subln_g, lambda_init):
    B, S, D = q.shape
    tq = _row_tile(S, 256)
    nh = ATTN_HEADS_PER_STEP
    width = nh * LANES
    n_groups = DA_HEADS // nh
    return pl.pallas_call(
        functools.partial(_attn_kernel, 1.0 - lambda_init),
        grid=(B, n_groups, S // tq),
        in_specs=[pl.BlockSpec((1, tq, width), lambda b, h, i: (b, i, h)),
                  pl.BlockSpec((1, S, width), lambda b, h, i: (b, 0, h)),
                  pl.BlockSpec((1, S, width), lambda b, h, i: (b, 0, n_groups + h)),
                  pl.BlockSpec((1, 1, width), lambda b, h, i: (h, 0, 0)),
                  pl.BlockSpec((1, LANES), lambda b, h, i: (0, 0)),
                  pl.BlockSpec((1, LANES), lambda b, h, i: (0, 0))],
        out_specs=pl.BlockSpec((1, tq, width), lambda b, h, i: (b, i, h)),
        out_shape=jax.ShapeDtypeStruct((B, S, D), F32),
        scratch_shapes=[pltpu.VMEM((2 * nh, S, LANES), BF16),
                        pltpu.VMEM((nh, S // tq, LANES + ATTN_ONES_ROWS, tq), BF16),
                        pltpu.VMEM((2 * nh, LANES + ATTN_ONES_ROWS, tq), F32)],
        compiler_params=_params("parallel", "parallel", "arbitrary"),
    )(q, kv, kv, slopes.reshape(n_groups, 1, width), lam_row, subln_g)


def _router_kernel(x_ref, w_ref, b_ref, idx_out, gate_out, count_out):
    @pl.when(pl.program_id(0) == 0)
    def _():
        count_out[...] = jnp.zeros_like(count_out)

    x = x_ref[...]
    xh, xm_, xl = _split3(x)
    w = w_ref[...]
    f = lambda p, q: jnp.dot(p, q, preferred_element_type=F32)
    logits = (f(xh, w[0]) + (f(xh, w[1]) + f(xm_, w[0]))
              + (f(xh, w[2]) + f(xm_, w[1]) + f(xl, w[0]))) + b_ref[...]
    lane = lax.broadcasted_iota(jnp.int32, logits.shape, 1)
    vals, idxs = [], []
    cur = logits
    for _ in range(TOP_K):
        m = jnp.max(cur, axis=-1, keepdims=True)
        idx = jnp.min(jnp.where(cur == m, lane, LANES), axis=-1, keepdims=True)
        cur = jnp.where(lane == idx, -jnp.inf, cur)
        vals.append(m)
        idxs.append(idx)
    es = [jnp.exp(v - vals[0]) for v in vals]
    denom = es[0] + es[1] + es[2] + es[3]
    idx_row = jnp.zeros(logits.shape, jnp.int32)
    gate_row = jnp.zeros(logits.shape, F32)
    for kx in range(TOP_K):
        idx_row = jnp.where(lane == kx, idxs[kx], idx_row)
        gate_row = jnp.where(lane == kx, es[kx] / denom, gate_row)
    idx_out[...] = idx_row
    gate_out[...] = gate_row
    picked = sum((lane == idx).astype(jnp.int32) for idx in idxs)
    count_out[0:1, :] += jnp.sum(picked, axis=0, keepdims=True)


def _router(x, w3, b_row):
    T, D = x.shape
    tm = _row_tile(T, 512)
    tile = pl.BlockSpec((tm, D), lambda i: (i, 0))
    out_tile = pl.BlockSpec((tm, LANES), lambda i: (i, 0))
    idx_rows, gate_rows, counts = pl.pallas_call(
        _router_kernel,
        grid=(T // tm,),
        in_specs=[tile, pl.BlockSpec(w3.shape, lambda i: (0, 0, 0)),
                  pl.BlockSpec(b_row.shape, lambda i: (0, 0))],
        out_specs=[out_tile, out_tile, pl.BlockSpec((SUBLANES, LANES), lambda i: (0, 0))],
        out_shape=[jax.ShapeDtypeStruct((T, LANES), jnp.int32),
                   jax.ShapeDtypeStruct((T, LANES), F32),
                   jax.ShapeDtypeStruct((SUBLANES, LANES), jnp.int32)],
        compiler_params=_params("arbitrary"),
    )(x, w3, b_row)
    return idx_rows, gate_rows, counts[0, :N_EXPERTS]


def _clamped_swiglu(h):
    gate, up = h[:, :EXPERT_FF], h[:, EXPERT_FF:]
    gate = jnp.minimum(gate, SWIGLU_LIMIT)
    up = jnp.clip(up, -SWIGLU_LIMIT, SWIGLU_LIMIT)
    return (up + 1.0) * (gate * jax.nn.sigmoid(SWIGLU_ALPHA * gate))


def _moe_kernel(bexp_ref, g_first, g_b, g_next, s_prev, s_a, s_b, x_hbm,
                wgu_a, bgu_a, wdn_a, bdn_a, wgu_b, bgu_b, wdn_b, bdn_b, y_hbm,
                xbuf_a, xbuf_b, ybuf_a, ybuf_b, sem_ga, sem_gb, sem_sa, sem_sb):
    del bexp_ref
    g = pl.program_id(0)
    last = pl.num_programs(0) - 1

    n = wdn_a.shape[2] // LANES
    tile_of = lambda r: pl.ds(pl.multiple_of(r, n), n)

    def gather(idx_ref, xbuf, sem, j):
        return pltpu.make_async_copy(x_hbm.at[tile_of(idx_ref[0, 0, j])], xbuf.at[pl.ds(j * n, n)], sem)

    def scatter(idx_ref, ybuf, sem, j):
        return pltpu.make_async_copy(ybuf.at[pl.ds(j * n, n)], y_hbm.at[tile_of(idx_ref[0, 0, j])], sem)

    def wait_gather(xbuf, sem):
        pltpu.make_async_copy(x_hbm.at[pl.ds(0, EXPERT_BLOCK * n)], xbuf, sem).wait()

    def wait_scatter(ybuf, sem):
        pltpu.make_async_copy(ybuf, y_hbm.at[pl.ds(0, EXPERT_BLOCK * n)], sem).wait()

    def expert_mlp(xbuf, ybuf, wgu, bgu, wdn, bdn):
        h = _dot(_from_row_tiles(xbuf, EXPERT_BLOCK, n), wgu[0]) + bgu[0]
        _to_row_tiles(ybuf, _dot(_clamped_swiglu(h), wdn[0]) + bdn[0])

    @pl.when(g == 0)
    def _():
        ybuf_b[...] = jnp.zeros_like(ybuf_b)

        def first(j, c):
            gather(g_first, xbuf_a, sem_ga, j).start()
            return c
        lax.fori_loop(0, EXPERT_BLOCK, first, 0)

    wait_gather(xbuf_a, sem_ga)

    @pl.when(g > 0)
    def _():
        wait_scatter(ybuf_a, sem_sa)

    for j in range(EXPERT_BLOCK):
        gather(g_b, xbuf_b, sem_gb, j).start(priority=j % DMA_PRIORITIES)
        scatter(s_prev, ybuf_b, sem_sb, j).start(priority=(j + 1) % DMA_PRIORITIES)
    expert_mlp(xbuf_a, ybuf_a, wgu_a, bgu_a, wdn_a, bdn_a)

    wait_gather(xbuf_b, sem_gb)
    wait_scatter(ybuf_b, sem_sb)

    for j in range(EXPERT_BLOCK):
        gather(g_next, xbuf_a, sem_ga, j).start(priority=j % DMA_PRIORITIES)
        scatter(s_a, ybuf_a, sem_sa, j).start(priority=(j + 1) % DMA_PRIORITIES)
    expert_mlp(xbuf_b, ybuf_b, wgu_b, bgu_b, wdn_b, bdn_b)

    @pl.when(g == last)
    def _():
        def final(j, c):
            scatter(s_b, ybuf_b, sem_sb, j).start()
            return c
        lax.fori_loop(0, EXPERT_BLOCK, final, 0)
        wait_gather(xbuf_a, sem_ga)
        wait_scatter(ybuf_a, sem_sa)
        wait_scatter(ybuf_b, sem_sb)


def _moe_experts(x_tiles, gather_tok, scatter_row, block_exp, n_out_rows, w_gu, b_gu, w_dn, b_dn):
    D = w_gu.shape[1]
    n = D // LANES
    n_blocks = block_exp.shape[0]
    assert n_blocks % 2 == 0
    FF2 = w_gu.shape[2]
    idx_spec = lambda fn: pl.BlockSpec((1, 1, EXPERT_BLOCK), fn, memory_space=pltpu.SMEM)
    weights = lambda off: [
        pl.BlockSpec((1, D, FF2), lambda g, be: (be[2 * g + off], 0, 0)),
        pl.BlockSpec((1, 1, FF2), lambda g, be: (be[2 * g + off], 0, 0)),
        pl.BlockSpec((1, EXPERT_FF, D), lambda g, be: (be[2 * g + off], 0, 0)),
        pl.BlockSpec((1, 1, D), lambda g, be: (be[2 * g + off], 0, 0)),
    ]
    grid_spec = pltpu.PrefetchScalarGridSpec(
        num_scalar_prefetch=1,
        grid=(n_blocks // 2,),
        in_specs=[
            idx_spec(lambda g, be: (0, 0, 0)),
            idx_spec(lambda g, be: (2 * g + 1, 0, 0)),
            idx_spec(lambda g, be: (2 * g + 2, 0, 0)),
            idx_spec(lambda g, be: (2 * g, 0, 0)),
            idx_spec(lambda g, be: (2 * g + 1, 0, 0)),
            idx_spec(lambda g, be: (2 * g + 2, 0, 0)),
            pl.BlockSpec(memory_space=pl.ANY),
        ] + weights(0) + weights(1),
        out_specs=pl.BlockSpec(memory_space=pl.ANY),
        scratch_shapes=[pltpu.VMEM((EXPERT_BLOCK * n, LANES), F32)] * 4
        + [pltpu.SemaphoreType.DMA(())] * 4,
    )
    b_gu = b_gu.reshape(N_EXPERTS, 1, FF2)
    b_dn = b_dn.reshape(N_EXPERTS, 1, D)
    return pl.pallas_call(
        _moe_kernel,
        grid_spec=grid_spec,
        out_shape=jax.ShapeDtypeStruct((n_out_rows * n, LANES), F32),
        compiler_params=_params("arbitrary"),
    )(block_exp, gather_tok, gather_tok, gather_tok, scatter_row, scatter_row, scatter_row, x_tiles,
      w_gu, b_gu, w_dn, b_dn, w_gu, b_gu, w_dn, b_dn)


def _combine_ln_kernel(y0_ref, y1_ref, y2_ref, y3_ref, gate_ref, x_ref, ln_ref, o_ref):
    gate = gate_ref[...]
    tm, D = x_ref.shape
    ffn = jnp.zeros(x_ref.shape, F32)
    for kx, y_ref in enumerate((y0_ref, y1_ref, y2_ref, y3_ref)):
        ffn = ffn + _from_row_tiles(y_ref, tm, D // LANES) * gate[:, kx:kx + 1]
    z = DEEPNORM_ALPHA * x_ref[...] + ffn
    o_ref[...] = _layer_norm(z, ln_ref[0:1, :], ln_ref[1:2, :])


def _combine_ln(y_rows, gate_rows, x, ln):
    T, D = x.shape
    tm = _row_tile(T, 512)
    tile = pl.BlockSpec((tm, D), lambda i: (i, 0))
    slot = lambda kx: pl.BlockSpec((tm * (D // LANES), LANES), lambda i: (kx * (T // tm) + i, 0))
    return pl.pallas_call(
        _combine_ln_kernel,
        grid=(T // tm,),
        in_specs=[slot(kx) for kx in range(TOP_K)]
        + [pl.BlockSpec((tm, LANES), lambda i: (i, 0)), tile, pl.BlockSpec(ln.shape, lambda i: (0, 0))],
        out_specs=tile,
        out_shape=jax.ShapeDtypeStruct((T, D), F32),
        compiler_params=_params("parallel"),
    )(y_rows, y_rows, y_rows, y_rows, gate_rows, x, ln)


def _routing_plan(top_idx, counts):
    T = top_idx.shape[0]
    TK = T * TOP_K
    order = jnp.argsort(top_idx.reshape(TK)).astype(jnp.int32)
    padded = (counts + EXPERT_BLOCK - 1) // EXPERT_BLOCK * EXPERT_BLOCK
    pad_end = jnp.cumsum(padded)
    pad_start = pad_end - padded
    grp_start = jnp.cumsum(counts) - counts
    spare_start = TK + jnp.cumsum(padded - counts) - (padded - counts)
    n_blocks = -(-TK // EXPERT_BLOCK) + N_EXPERTS
    n_rows = n_blocks * EXPERT_BLOCK
    block_start = jnp.arange(n_blocks, dtype=jnp.int32) * EXPERT_BLOCK
    block_exp = jnp.minimum(jnp.sum(pad_end[None, :] <= block_start[:, None], axis=1),
                            N_EXPERTS - 1).astype(jnp.int32)
    off = (block_start - pad_start[block_exp])[:, None] + jnp.arange(EXPERT_BLOCK, dtype=jnp.int32)[None, :]
    cnt = counts[block_exp][:, None]
    valid = off < cnt
    pair = order[jnp.clip(grp_start[block_exp][:, None] + off, 0, TK - 1)]
    tok, slot = pair // TOP_K, pair % TOP_K
    spare = spare_start[block_exp][:, None] + (off - cnt)
    scale = D_MODEL // LANES
    gather_tok = jnp.where(valid, tok, 0).astype(jnp.int32) * scale
    scatter_row = jnp.where(valid, slot * T + tok, spare).astype(jnp.int32) * scale
    fill = jnp.arange(EXPERT_BLOCK, dtype=jnp.int32)[None, :]
    gather_tok = jnp.concatenate([gather_tok, jnp.zeros_like(fill)])
    scatter_row = jnp.concatenate([(n_rows + fill) * scale, scatter_row])
    shape = (n_blocks + 1, 1, EXPERT_BLOCK)
    return gather_tok.reshape(shape), scatter_row.reshape(shape), block_exp, n_rows + EXPERT_BLOCK


def _moe_ffn_ln(x, x_tiles, router_w, router_b, w_gu, b_gu, w_dn, b_dn, ln):
    T, D = x.shape
    w_pad = jnp.zeros((D, LANES), F32).at[:, :N_EXPERTS].set(router_w)
    w3 = jnp.stack(_split3(w_pad))
    b_row = jnp.full((1, LANES), NEG_BIG, F32).at[0, :N_EXPERTS].set(router_b)
    idx_rows, gate_rows, counts = _router(x, w3, b_row)
    gather_tok, scatter_row, block_exp, n_out_rows = _routing_plan(idx_rows[:, :TOP_K], counts)
    y_rows = _moe_experts(x_tiles, gather_tok, scatter_row, block_exp, n_out_rows,
                          w_gu, b_gu, w_dn, b_dn)
    return _combine_ln(y_rows, gate_rows, x, ln)


def _cast_kernel(w_ref, o_ref):
    o_ref[...] = w_ref[...].astype(BF16)


def _cast_bf16(w):
    lead = w.shape[:-2]
    w3 = w.reshape((-1,) + w.shape[-2:])
    spec = pl.BlockSpec((1,) + w3.shape[1:], lambda i: (i, 0, 0))
    out = pl.pallas_call(
        _cast_kernel,
        grid=(w3.shape[0],),
        in_specs=[spec],
        out_specs=spec,
        out_shape=jax.ShapeDtypeStruct(w3.shape, BF16),
        compiler_params=_params("parallel"),
    )(w3)
    return out.reshape(lead + w.shape[-2:])


def _pad_rows(rows, n=SUBLANES):
    out = jnp.zeros((n, rows[0].shape[-1]), F32)
    for j, r in enumerate(rows):
        out = out.at[j].set(r.reshape(-1))
    return out


def kernel(x, ln_g, ln_b, rwkv_mix, rwkv_w_rkv, rwkv_w_o, rwkv_w0, rwkv_w1, rwkv_w2, rwkv_a0, rwkv_a1, rwkv_a2, rwkv_g1, rwkv_g2, rwkv_k_k, rwkv_k_a, rwkv_r_k, rwkv_lnx_g, rwkv_lnx_b, rwkv_v0, rwkv_v1, rwkv_v2, kv_w, da_w_q, da_w_o, da_lambda, da_subln_g, moe_router_w, moe_router_b, moe_w_gu, moe_b_gu, moe_w_dn, moe_b_dn):
    B, S, D = x.shape
    T = B * S
    v_first = None
    kv = None
    slopes = 2.0 ** (-8.0 * jnp.arange(1, DA_HEADS + 1, dtype=F32) / DA_HEADS)
    slopes = jnp.repeat(slopes, LANES).reshape(1, DA_HEADS * LANES)
    w_gu_bf16 = _cast_bf16(moe_w_gu)
    w_dn_bf16 = _cast_bf16(moe_w_dn)
    for l in range(DEPTH):
        ln_mix = jnp.stack([ln_g[l, 0], ln_b[l, 0]])
        ln_ffn = jnp.stack([ln_g[l, 1], ln_b[l, 1]])
        xt = x.reshape(T, D)
        if l < N_A_LAYERS:
            zero = jnp.zeros((D,), F32)
            vecs = _pad_rows([rwkv_w0[l], rwkv_a0[l], rwkv_v0[l - 1] if l > 0 else zero])
            vres = None if l == 0 else (rwkv_v1[l - 1].astype(BF16), rwkv_v2[l - 1].astype(BF16), v_first)
            r, lw, k, v, a, g = _rwkv_proj(
                x, _pad_rows(list(rwkv_mix[l])), rwkv_w_rkv[l].astype(BF16),
                rwkv_w1[l].astype(BF16), rwkv_w2[l].astype(BF16), rwkv_a1[l].astype(BF16),
                rwkv_a2[l].astype(BF16), rwkv_g1[l].astype(BF16), rwkv_g2[l].astype(BF16), vecs, vres)
            if l == 0:
                v_first = v
            head_params = _pad_rows([rwkv_k_k[l], rwkv_k_a[l],
                                     jnp.tile(rwkv_r_k[l].reshape(-1), 1), rwkv_lnx_g[l], rwkv_lnx_b[l]])
            y = _wkv(r, lw, k, v, a, head_params)
            xt, xt_tiles = _proj_ln(y.reshape(T, D), g.reshape(T, D), rwkv_w_o[l].astype(BF16), xt, ln_mix)
        else:
            if l == N_A_LAYERS:
                kv = _matmul(xt, kv_w.astype(BF16)).reshape(B, S, 2 * D)
            j = l - N_A_LAYERS
            lambda_init = 0.8 - 0.6 * math.exp(-0.3 * l)
            lam = da_lambda[j].astype(F32)
            lam_full = (jnp.exp(jnp.sum(lam[0] * lam[1])) - jnp.exp(jnp.sum(lam[2] * lam[3]))
                        + lambda_init)
            q = _matmul(xt, da_w_q[j].astype(BF16)).reshape(B, S, D)
            o = _diff_attention(q, kv, slopes, jnp.full((1, LANES), lam_full, F32),
                                da_subln_g[j].reshape(1, LANES), lambda_init)
            xt, xt_tiles = _proj_ln(o.reshape(T, D), None, da_w_o[j].astype(BF16), xt, ln_mix)
        xt = _moe_ffn_ln(xt, xt_tiles, moe_router_w[l], moe_router_b[l], w_gu_bf16[l], moe_b_gu[l],
                         w_dn_bf16[l], moe_b_dn[l], ln_ffn)
        x = xt.reshape(B, S, D)
    return x
```

```python
import functools
import math

import jax
import jax.numpy as jnp
from jax import lax
from jax.experimental import pallas as pl
from jax.experimental.pallas import tpu as pltpu

F32 = jnp.float32
BF16 = jnp.bfloat16

D_MODEL = 1024
DEPTH = 4
N_A_LAYERS = DEPTH // 2
RWKV_HEAD_DIM = 64
RWKV_GN_EPS = 64e-5
L2_EPS = 1e-12
DA_HEAD_DIM = 64
DA_HEADS = D_MODEL // (2 * DA_HEAD_DIM)
SUBLN_EPS = 1e-5
N_EXPERTS = 32
TOP_K = 4
EXPERT_FF = D_MODEL
SWIGLU_LIMIT = 7.0
SWIGLU_ALPHA = 1.702
EXPERT_BLOCK = 512
DEEPNORM_ALPHA = (2 * DEPTH) ** 0.25
LN_EPS = 1e-5

LANES = 128
SUBLANES = 8
VMEM_LIMIT_BYTES = 56 * 1024 * 1024
DMA_PRIORITIES = 2

WKV_CHUNK = 64
WKV_TILES_PER_STEP = 8
HEADS_PER_TILE = LANES // RWKV_HEAD_DIM
NEG_BIG = -1e30
ATTN_ONES_ROWS = 16
ATTN_HEADS_PER_STEP = 4


def _params(*sem):
    return pltpu.CompilerParams(dimension_semantics=sem, vmem_limit_bytes=VMEM_LIMIT_BYTES)


def _dot(a, b):
    return jnp.dot(a.astype(BF16), b.astype(BF16), preferred_element_type=F32)


def _dot_nt(a, b):
    return lax.dot_general(a.astype(BF16), b.astype(BF16), (((1,), (1,)), ((), ())),
                           preferred_element_type=F32)


def _split2(x):
    hi = x.astype(BF16)
    return hi, (x - hi.astype(F32)).astype(BF16)


def _dot_exact_lhs(m_bf16, x):
    hi, lo = _split2(x)
    return (jnp.dot(m_bf16, hi, preferred_element_type=F32)
            + jnp.dot(m_bf16, lo, preferred_element_type=F32))


def _layer_norm(z, g, b):
    mu = jnp.mean(z, axis=-1, keepdims=True)
    d = z - mu
    var = jnp.mean(d * d, axis=-1, keepdims=True)
    return d * lax.rsqrt(var + LN_EPS) * g + b


def _row_tile(n, want):
    t = min(want, n)
    assert n % t == 0 and t % SUBLANES == 0
    return t


def _rwkv_proj_kernel(has_vres, *refs):
    if has_vres:
        (x_ref, xp_ref, mix_ref, wrkv_ref, w1_ref, w2_ref, a1_ref, a2_ref, g1_ref, g2_ref,
         vec_ref, v1_ref, v2_ref, vfirst_ref,
         r_out, lw_out, k_out, v_out, a_out, g_out) = refs
    else:
        (x_ref, xp_ref, mix_ref, wrkv_ref, w1_ref, w2_ref, a1_ref, a2_ref, g1_ref, g2_ref,
         vec_ref, r_out, lw_out, k_out, v_out, a_out, g_out) = refs
    i = pl.program_id(1)
    x = x_ref[0]
    prev = xp_ref[0][SUBLANES - 1:SUBLANES, :]
    prev = jnp.where(i == 0, 0.0, prev)
    row = lax.broadcasted_iota(jnp.int32, x.shape, 0)
    x_prev = jnp.where(row == 0, prev, pltpu.roll(x, 1, axis=0))
    xx = x_prev - x

    def xm(j):
        return x + xx * mix_ref[j:j + 1, :]

    xv = xm(2)
    r_out[0] = _dot(xm(0), wrkv_ref[0])
    k_out[0] = _dot(xm(1), wrkv_ref[1])
    v = _dot(xv, wrkv_ref[2])
    wl = vec_ref[0:1, :] + _dot(jnp.tanh(_dot(xm(3), w1_ref[...])), w2_ref[...])
    lw_out[0] = -jax.nn.sigmoid(wl) * math.exp(-0.5)
    a_out[0] = jax.nn.sigmoid(vec_ref[1:2, :] + _dot(_dot(xm(4), a1_ref[...]), a2_ref[...]))
    g_out[0] = _dot(jax.nn.sigmoid(_dot(xm(5), g1_ref[...])), g2_ref[...])
    if has_vres:
        mixv = jax.nn.sigmoid(vec_ref[2:3, :] + _dot(_dot(xv, v1_ref[...]), v2_ref[...]))
        v = v + (vfirst_ref[0] - v) * mixv
    v_out[0] = v


def _rwkv_proj(x, mix, w_rkv, w1, w2, a1, a2, g1, g2, vecs, vres):
    B, S, D = x.shape
    tm = _row_tile(S, 512)
    has_vres = vres is not None
    full = lambda a: pl.BlockSpec(a.shape, lambda b, i: (0,) * a.ndim)
    tile = pl.BlockSpec((1, tm, D), lambda b, i: (b, i, 0))
    prev = pl.BlockSpec((1, SUBLANES, D),
                        lambda b, i: (b, jnp.maximum(i * (tm // SUBLANES) - 1, 0), 0))
    ins = [x, x, mix, w_rkv, w1, w2, a1, a2, g1, g2, vecs]
    specs = [tile, prev] + [full(a) for a in ins[2:]]
    if has_vres:
        v1, v2, v_first = vres
        ins += [v1, v2, v_first]
        specs += [full(v1), full(v2), tile]
    out = jax.ShapeDtypeStruct((B, S, D), F32)
    return pl.pallas_call(
        functools.partial(_rwkv_proj_kernel, has_vres),
        grid=(B, S // tm),
        in_specs=specs,
        out_specs=[tile] * 6,
        out_shape=[out] * 6,
        compiler_params=_params("parallel", "arbitrary"),
    )(*ins)


def _wkv_kernel(r_ref, lw_ref, k_ref, v_ref, a_ref, hp_ref, y_out,
                state, avec_s, bvec_s, kmod_s, y_s):
    C = WKV_CHUNK
    ts = r_ref.shape[1]
    n_tiles = r_ref.shape[2] // LANES
    n_chunks = ts // C
    R2 = HEADS_PER_TILE * C
    inv_n = 1.0 / RWKV_HEAD_DIM

    @pl.when(pl.program_id(2) == 0)
    def _():
        state[...] = jnp.zeros_like(state)

    li = lax.broadcasted_iota(jnp.int32, (LANES, LANES), 0)
    lj = lax.broadcasted_iota(jnp.int32, (LANES, LANES), 1)
    head_ones = (li // RWKV_HEAD_DIM == lj // RWKV_HEAD_DIM).astype(BF16)

    for t in range(n_tiles):
        ls = slice(t * LANES, (t + 1) * LANES)
        k = k_ref[0, :, ls]
        a = a_ref[0, :, ls]
        kk = k * hp_ref[0:1, ls]
        n2 = _dot(kk * kk, head_ones)
        kk = kk / jnp.maximum(jnp.sqrt(n2), L2_EPS)
        avec_s[:, ls] = -kk
        bvec_s[:, ls] = kk * a
        kmod_s[:, ls] = k * (1.0 + (a - 1.0) * hp_ref[1:2, ls])

    ri = lax.broadcasted_iota(jnp.int32, (R2, R2), 0)
    rj = lax.broadcasted_iota(jnp.int32, (R2, R2), 1)
    same_head = (ri // C) == (rj // C)
    strict = same_head & ((ri % C) > (rj % C))
    incl = same_head & ((ri % C) >= (rj % C))
    ci = lax.broadcasted_iota(jnp.int32, (C, C), 0)
    cj = lax.broadcasted_iota(jnp.int32, (C, C), 1)
    tri = (ci >= cj).astype(BF16)
    lane = lax.broadcasted_iota(jnp.int32, (C, LANES), 1)
    head0 = lane < RWKV_HEAD_DIM
    n_doublings = int(math.log2(C))

    def stack(t):
        return jnp.concatenate([jnp.where(head0, t, 0.0), jnp.where(head0, 0.0, t)], axis=0)

    tiles = range(n_tiles)
    lanes_of = lambda t: slice(t * LANES, (t + 1) * LANES)

    def chunk(c, carry):
        sl = pl.ds(pl.multiple_of(c * C, C), C)
        each = lambda f: [f(t) for t in tiles]
        s0 = each(lambda t: state[t])
        lw = each(lambda t: lw_ref[0, sl, lanes_of(t)])
        cum = each(lambda t: _dot_exact_lhs(tri, lw[t]))
        tot = each(lambda t: cum[t][C - 1:C, :])
        e_neg = each(lambda t: jnp.exp(-cum[t]))
        av = each(lambda t: avec_s[sl, lanes_of(t)])
        bv = each(lambda t: bvec_s[sl, lanes_of(t)])
        km = each(lambda t: kmod_s[sl, lanes_of(t)])
        lhs = each(lambda t: jnp.concatenate(
            [stack(av[t] * jnp.exp(cum[t] - lw[t])),
             stack(r_ref[0, sl, lanes_of(t)] * jnp.exp(cum[t]))], axis=0))
        rhs = each(lambda t: jnp.concatenate([stack(bv[t] * e_neg[t]), stack(km[t] * e_neg[t])], axis=0))
        g = each(lambda t: _dot_nt(lhs[t], rhs[t]))
        a_s = each(lambda t: _dot_nt(lhs[t], s0[t]))
        vm = each(lambda t: stack(v_ref[0, sl, lanes_of(t)]))
        akv = each(lambda t: _dot(jnp.where(strict, g[t][:R2, R2:], 0.0), vm[t]))
        u = each(lambda t: a_s[t][:R2] + akv[t])
        p = each(lambda t: jnp.where(strict, g[t][:R2, :R2], 0.0))
        for it in range(n_doublings):
            pu = each(lambda t: _dot(p[t], u[t]))
            if it + 1 < n_doublings:
                p = each(lambda t: _dot(p[t], p[t]))
            u = each(lambda t: u[t] + pu[t])
        uv = each(lambda t: jnp.concatenate([u[t], vm[t]], axis=0))
        w_y = each(lambda t: jnp.concatenate([jnp.where(incl, g[t][R2:, :R2], 0.0),
                                              jnp.where(incl, g[t][R2:, R2:], 0.0)], axis=1))
        y_sm = each(lambda t: a_s[t][R2:] + _dot(w_y[t], uv[t]))
        bk_end = each(lambda t: jnp.concatenate(
            [stack(bv[t] * jnp.exp(tot[t] - cum[t])), stack(km[t] * jnp.exp(tot[t] - cum[t]))], axis=0))
        s1 = each(lambda t: s0[t] * jnp.exp(tot[t]) + _dot(uv[t].T, bk_end[t]))
        for t in tiles:
            y_s[sl, lanes_of(t)] = y_sm[t][:C] + y_sm[t][C:]
            state[t] = s1[t]
        return carry

    lax.fori_loop(0, n_chunks, chunk, 0)

    for t in range(n_tiles):
        ls = slice(t * LANES, (t + 1) * LANES)
        y = y_s[:, ls]
        mu = _dot(y, head_ones) * inv_n
        d = y - mu
        var = _dot(d * d, head_ones) * inv_n
        yn = d * lax.rsqrt(var + RWKV_GN_EPS) * hp_ref[3:4, ls] + hp_ref[4:5, ls]
        bonus = _dot(r_ref[0, :, ls] * kmod_s[:, ls] * hp_ref[2:3, ls], head_ones)
        y_out[0, :, ls] = yn + bonus * v_ref[0, :, ls]


def _wkv(r, lw, k, v, a, head_params):
    B, S, D = r.shape
    ts = _row_tile(S, 512)
    assert ts % WKV_CHUNK == 0
    width = WKV_TILES_PER_STEP * LANES
    tile = pl.BlockSpec((1, ts, width), lambda b, p, i: (b, i, p))
    hp = pl.BlockSpec((SUBLANES, width), lambda b, p, i: (0, p))
    return pl.pallas_call(
        _wkv_kernel,
        grid=(B, D // width, S // ts),
        in_specs=[tile] * 5 + [hp],
        out_specs=tile,
        out_shape=jax.ShapeDtypeStruct((B, S, D), F32),
        scratch_shapes=[pltpu.VMEM((WKV_TILES_PER_STEP, LANES, LANES), F32)]
        + [pltpu.VMEM((ts, width), F32)] * 4,
        compiler_params=_params("parallel", "parallel", "arbitrary"),
    )(r, lw, k, v, a, head_params)


def _matmul_kernel(a_ref, w_ref, o_ref):
    o_ref[...] = _dot(a_ref[...], w_ref[...])


def _matmul(a, w):
    T, K = a.shape
    N = w.shape[1]
    tm = _row_tile(T, 512)
    return pl.pallas_call(
        _matmul_kernel,
        grid=(T // tm,),
        in_specs=[pl.BlockSpec((tm, K), lambda i: (i, 0)), pl.BlockSpec((K, N), lambda i: (0, 0))],
        out_specs=pl.BlockSpec((tm, N), lambda i: (i, 0)),
        out_shape=jax.ShapeDtypeStruct((T, N), F32),
        compiler_params=_params("parallel"),
    )(a, w)


def _to_row_tiles(ref, value):
    n = value.shape[1] // LANES
    for c in range(n):
        ref[pl.ds(c, value.shape[0], stride=n), :] = value[:, c * LANES:(c + 1) * LANES]


def _from_row_tiles(ref, n_rows, n):
    return jnp.concatenate([ref[pl.ds(c, n_rows, stride=n), :] for c in range(n)], axis=1)


def _proj_ln_kernel(has_gate, *refs):
    if has_gate:
        a_ref, g_ref, w_ref, x_ref, ln_ref, o_ref, o_tiles_ref = refs
        a = a_ref[...] * g_ref[...]
    else:
        a_ref, w_ref, x_ref, ln_ref, o_ref, o_tiles_ref = refs
        a = a_ref[...]
    z = DEEPNORM_ALPHA * x_ref[...] + _dot(a, w_ref[...])
    out = _layer_norm(z, ln_ref[0:1, :], ln_ref[1:2, :])
    o_ref[...] = out
    _to_row_tiles(o_tiles_ref, out)


def _proj_ln(a, gate, w, x, ln):
    T, D = x.shape
    tm = _row_tile(T, 512)
    n = D // LANES
    tile = pl.BlockSpec((tm, D), lambda i: (i, 0))
    full = lambda t: pl.BlockSpec(t.shape, lambda i: (0, 0))
    has_gate = gate is not None
    ins = [a] + ([gate] if has_gate else []) + [w, x, ln]
    specs = [tile] + ([tile] if has_gate else []) + [full(w), tile, full(ln)]
    return pl.pallas_call(
        functools.partial(_proj_ln_kernel, has_gate),
        grid=(T // tm,),
        in_specs=specs,
        out_specs=[tile, pl.BlockSpec((tm * n, LANES), lambda i: (i, 0))],
        out_shape=[jax.ShapeDtypeStruct((T, D), F32), jax.ShapeDtypeStruct((T * n, LANES), F32)],
        compiler_params=_params("parallel"),
    )(*ins)


def _max_rows(s, groups=4):
    step = s.shape[0] // groups
    part = s[0:step]
    for g in range(1, groups):
        part = jnp.maximum(part, s[g * step:(g + 1) * step])
    return jnp.max(part, axis=0, keepdims=True)


def _attn_kernel(out_scale, q_ref, k_ref, v_ref, slope_ref, lam_ref, g_ref, o_ref,
                 k_s, vt_s, acc_s):
    i = pl.program_id(2)
    tq = q_ref.shape[1]
    tk = tq
    n_heads = q_ref.shape[2] // LANES
    n_kv_blocks = k_ref.shape[1] // tk
    heads = range(n_heads)
    chains = [(h, c) for h in heads for c in range(2)]
    lanes_of = lambda h: slice(h * LANES, (h + 1) * LANES)

    @pl.when(i == 0)
    def _():
        lane = lax.broadcasted_iota(jnp.int32, (tk, LANES), 1)
        for h in heads:
            for blk in range(n_kv_blocks):
                sl = slice(blk * tk, (blk + 1) * tk)
                kb = k_ref[0, sl, lanes_of(h)]
                k_s[2 * h, sl, :] = jnp.where(lane < DA_HEAD_DIM, kb, 0.0).astype(BF16)
                k_s[2 * h + 1, sl, :] = jnp.where(lane < DA_HEAD_DIM, 0.0, kb).astype(BF16)
                vt_s[h, blk, 0:LANES, :] = v_ref[0, sl, lanes_of(h)].T.astype(BF16)
                vt_s[h, blk, LANES:, :] = jnp.ones((ATTN_ONES_ROWS, tk), BF16)

    q = [(q_ref[0, :, lanes_of(h)] * (DA_HEAD_DIM ** -0.5)).astype(BF16) for h in heads]
    slope = [slope_ref[0, :, h * LANES:h * LANES + 1] for h in heads]
    k_loc = lax.broadcasted_iota(jnp.int32, (tk, tq), 0)
    q_loc = lax.broadcasted_iota(jnp.int32, (tk, tq), 1)
    k_loc_f = k_loc.astype(F32)
    acc_s[...] = jnp.zeros_like(acc_s)

    def block(j, ms, diagonal):
        sl = pl.ds(pl.multiple_of(j * tk, tk), tk)
        rel = ((j - i) * tk).astype(F32)
        each = lambda f: [f(n, h) for n, (h, _) in enumerate(chains)]
        s = each(lambda n, h: _dot_nt(k_s[n, sl, :], q[h]) + slope[h] * k_loc_f)
        if diagonal:
            s = each(lambda n, h: jnp.where(k_loc > q_loc, NEG_BIG, s[n]))
        offset = [slope[h] * rel for h in heads]
        m_new = each(lambda n, h: jnp.maximum(ms[n], _max_rows(s[n]) + offset[h]))
        p = each(lambda n, h: jnp.exp(s[n] - (m_new[n] - offset[h])).astype(BF16))
        acc_old = each(lambda n, h: acc_s[n])
        pv = each(lambda n, h: jnp.dot(vt_s[h, j], p[n], preferred_element_type=F32))
        for n in range(len(chains)):
            acc_s[n] = jnp.exp(ms[n] - m_new[n]) * acc_old[n] + pv[n]
        return tuple(m_new)

    init = tuple(jnp.full((1, tq), NEG_BIG, F32) for _ in chains)
    ms = lax.fori_loop(0, i, lambda j, ms: block(j, ms, False), init)
    block(i, ms, True)

    for h in heads:
        acc0, acc1 = acc_s[2 * h], acc_s[2 * h + 1]
        o_t = (acc0[:LANES] / acc0[LANES:LANES + 1]
               - lam_ref[:, 0:1] * (acc1[:LANES] / acc1[LANES:LANES + 1]))
        o = o_t.T
        o = o * lax.rsqrt(jnp.mean(o * o, axis=-1, keepdims=True) + SUBLN_EPS) * g_ref[...]
        o_ref[0, :, lanes_of(h)] = o * out_scale


def _diff_attention(q, kv, slopes, lam_row, subln_g, lambda_init):
    B, S, D = q.shape
    tq = _row_tile(S, 256)
    nh = ATTN_HEADS_PER_STEP
    width = nh * LANES
    n_groups = DA_HEADS // nh
    return pl.pallas_call(
        functools.partial(_attn_kernel, 1.0 - lambda_init),
        grid=(B, n_groups, S // tq),
        in_specs=[pl.BlockSpec((1, tq, width), lambda b, h, i: (b, i, h)),
                  pl.BlockSpec((1, S, width), lambda b, h, i: (b, 0, h)),
                  pl.BlockSpec((1, S, width), lambda b, h, i: (b, 0, n_groups + h)),
                  pl.BlockSpec((1, 1, width), lambda b, h, i: (h, 0, 0)),
                  pl.BlockSpec((1, LANES), lambda b, h, i: (0, 0)),
                  pl.BlockSpec((1, LANES), lambda b, h, i: (0, 0))],
        out_specs=pl.BlockSpec((1, tq, width), lambda b, h, i: (b, i, h)),
        out_shape=jax.ShapeDtypeStruct((B, S, D), F32),
        scratch_shapes=[pltpu.VMEM((2 * nh, S, LANES), BF16),
                        pltpu.VMEM((nh, S // tq, LANES + ATTN_ONES_ROWS, tq), BF16),
                        pltpu.VMEM((2 * nh, LANES + ATTN_ONES_ROWS, tq), F32)],
        compiler_params=_params("parallel", "parallel", "arbitrary"),
    )(q, kv, kv, slopes.reshape(n_groups, 1, width), lam_row, subln_g)


def _router_kernel(x_ref, w_ref, b_ref, idx_out, gate_out, count_out):
    @pl.when(pl.program_id(0) == 0)
    def _():
        count_out[...] = jnp.zeros_like(count_out)

    x = x_ref[...]
    xh, xl = _split2(x)
    w = w_ref[...]
    hh = jnp.dot(xh, w, preferred_element_type=F32)
    lh = jnp.dot(xl, w[:, :LANES], preferred_element_type=F32)
    logits = hh[:, :LANES] + (hh[:, LANES:] + lh) + b_ref[...]
    lane = lax.broadcasted_iota(jnp.int32, logits.shape, 1)
    vals, idxs = [], []
    cur = logits
    for _ in range(TOP_K):
        m = jnp.max(cur, axis=-1, keepdims=True)
        idx = jnp.min(jnp.where(cur == m, lane, LANES), axis=-1, keepdims=True)
        cur = jnp.where(lane == idx, -jnp.inf, cur)
        vals.append(m)
        idxs.append(idx)
    es = [jnp.exp(v - vals[0]) for v in vals]
    denom = es[0] + es[1] + es[2] + es[3]
    idx_row = jnp.zeros(logits.shape, jnp.int32)
    gate_row = jnp.zeros(logits.shape, F32)
    for kx in range(TOP_K):
        idx_row = jnp.where(lane == kx, idxs[kx], idx_row)
        gate_row = jnp.where(lane == kx, es[kx] / denom, gate_row)
    idx_out[...] = idx_row
    gate_out[...] = gate_row
    picked = sum((lane == idx).astype(jnp.int32) for idx in idxs)
    count_out[0:1, :] += jnp.sum(picked, axis=0, keepdims=True)


def _router(x, w3, b_row):
    T, D = x.shape
    tm = _row_tile(T, 512)
    tile = pl.BlockSpec((tm, D), lambda i: (i, 0))
    out_tile = pl.BlockSpec((tm, LANES), lambda i: (i, 0))
    idx_rows, gate_rows, counts = pl.pallas_call(
        _router_kernel,
        grid=(T // tm,),
        in_specs=[tile, pl.BlockSpec(w3.shape, lambda i: (0, 0)),
                  pl.BlockSpec(b_row.shape, lambda i: (0, 0))],
        out_specs=[out_tile, out_tile, pl.BlockSpec((SUBLANES, LANES), lambda i: (0, 0))],
        out_shape=[jax.ShapeDtypeStruct((T, LANES), jnp.int32),
                   jax.ShapeDtypeStruct((T, LANES), F32),
                   jax.ShapeDtypeStruct((SUBLANES, LANES), jnp.int32)],
        compiler_params=_params("arbitrary"),
    )(x, w3, b_row)
    return idx_rows, gate_rows, counts[0, :N_EXPERTS]


def _clamped_swiglu(h):
    gate, up = h[:, :EXPERT_FF], h[:, EXPERT_FF:]
    gate = jnp.minimum(gate, SWIGLU_LIMIT)
    up = jnp.clip(up, -SWIGLU_LIMIT, SWIGLU_LIMIT)
    return (up + 1.0) * (gate * jax.nn.sigmoid(SWIGLU_ALPHA * gate))


def _moe_kernel(bexp_ref, g_first, g_b, g_next, s_prev, s_a, s_b, x_hbm,
                wgu_a, bgu_a, wdn_a, bdn_a, wgu_b, bgu_b, wdn_b, bdn_b, y_hbm,
                xbuf_a, xbuf_b, ybuf_a, ybuf_b, sem_ga, sem_gb, sem_sa, sem_sb):
    del bexp_ref
    g = pl.program_id(0)
    last = pl.num_programs(0) - 1

    n = wdn_a.shape[2] // LANES
    tile_of = lambda r: pl.ds(pl.multiple_of(r, n), n)

    def gather(idx_ref, xbuf, sem, j):
        return pltpu.make_async_copy(x_hbm.at[tile_of(idx_ref[0, 0, j])], xbuf.at[pl.ds(j * n, n)], sem)

    def scatter(idx_ref, ybuf, sem, j):
        return pltpu.make_async_copy(ybuf.at[pl.ds(j * n, n)], y_hbm.at[tile_of(idx_ref[0, 0, j])], sem)

    def wait_gather(xbuf, sem):
        pltpu.make_async_copy(x_hbm.at[pl.ds(0, EXPERT_BLOCK * n)], xbuf, sem).wait()

    def wait_scatter(ybuf, sem):
        pltpu.make_async_copy(ybuf, y_hbm.at[pl.ds(0, EXPERT_BLOCK * n)], sem).wait()

    def expert_phase(xbuf, ybuf, wgu, bgu, wdn, bdn, sem_x, sem_y, gather_next, scatter_done):
        wait_gather(xbuf, sem_x)
        for j in range(EXPERT_BLOCK):
            gather_next(j).start(priority=j % DMA_PRIORITIES)
        h = _dot(_from_row_tiles(xbuf, EXPERT_BLOCK, n), wgu[0]) + bgu[0]
        act = _clamped_swiglu(h)
        wait_scatter(ybuf, sem_y)
        for j in range(EXPERT_BLOCK):
            scatter_done(j).start(priority=j % DMA_PRIORITIES)
        _to_row_tiles(ybuf, _dot(act, wdn[0]) + bdn[0])

    @pl.when(g == 0)
    def _():
        ybuf_a[...] = jnp.zeros_like(ybuf_a)
        ybuf_b[...] = jnp.zeros_like(ybuf_b)
        spare0 = y_hbm.shape[0] - EXPERT_BLOCK * n

        def first(j, c):
            gather(g_first, xbuf_a, sem_ga, j).start()
            pltpu.make_async_copy(ybuf_a.at[pl.ds(j * n, n)],
                                  y_hbm.at[pl.ds(pl.multiple_of(spare0 + j * n, n), n)], sem_sa).start()
            return c
        lax.fori_loop(0, EXPERT_BLOCK, first, 0)

    expert_phase(xbuf_a, ybuf_a, wgu_a, bgu_a, wdn_a, bdn_a, sem_ga, sem_sa,
                 lambda j: gather(g_b, xbuf_b, sem_gb, j),
                 lambda j: scatter(s_prev, ybuf_b, sem_sb, j))
    expert_phase(xbuf_b, ybuf_b, wgu_b, bgu_b, wdn_b, bdn_b, sem_gb, sem_sb,
                 lambda j: gather(g_next, xbuf_a, sem_ga, j),
                 lambda j: scatter(s_a, ybuf_a, sem_sa, j))

    @pl.when(g == last)
    def _():
        def final(j, c):
            scatter(s_b, ybuf_b, sem_sb, j).start()
            return c
        lax.fori_loop(0, EXPERT_BLOCK, final, 0)
        wait_gather(xbuf_a, sem_ga)
        wait_scatter(ybuf_a, sem_sa)
        wait_scatter(ybuf_b, sem_sb)


def _moe_experts(x_tiles, gather_tok, scatter_row, block_exp, n_out_rows, w_gu, b_gu, w_dn, b_dn):
    D = w_gu.shape[1]
    n = D // LANES
    n_blocks = block_exp.shape[0]
    assert n_blocks % 2 == 0
    FF2 = w_gu.shape[2]
    idx_spec = lambda fn: pl.BlockSpec((1, 1, EXPERT_BLOCK), fn, memory_space=pltpu.SMEM)
    weights = lambda off: [
        pl.BlockSpec((1, D, FF2), lambda g, be: (be[2 * g + off], 0, 0)),
        pl.BlockSpec((1, 1, FF2), lambda g, be: (be[2 * g + off], 0, 0)),
        pl.BlockSpec((1, EXPERT_FF, D), lambda g, be: (be[2 * g + off], 0, 0)),
        pl.BlockSpec((1, 1, D), lambda g, be: (be[2 * g + off], 0, 0)),
    ]
    grid_spec = pltpu.PrefetchScalarGridSpec(
        num_scalar_prefetch=1,
        grid=(n_blocks // 2,),
        in_specs=[
            idx_spec(lambda g, be: (0, 0, 0)),
            idx_spec(lambda g, be: (2 * g + 1, 0, 0)),
            idx_spec(lambda g, be: (2 * g + 2, 0, 0)),
            idx_spec(lambda g, be: (2 * g, 0, 0)),
            idx_spec(lambda g, be: (2 * g + 1, 0, 0)),
            idx_spec(lambda g, be: (2 * g + 2, 0, 0)),
            pl.BlockSpec(memory_space=pl.ANY),
        ] + weights(0) + weights(1),
        out_specs=pl.BlockSpec(memory_space=pl.ANY),
        scratch_shapes=[pltpu.VMEM((EXPERT_BLOCK * n, LANES), F32)] * 4
        + [pltpu.SemaphoreType.DMA(())] * 4,
    )
    b_gu = b_gu.reshape(N_EXPERTS, 1, FF2)
    b_dn = b_dn.reshape(N_EXPERTS, 1, D)
    return pl.pallas_call(
        _moe_kernel,
        grid_spec=grid_spec,
        out_shape=jax.ShapeDtypeStruct((n_out_rows * n, LANES), F32),
        compiler_params=_params("arbitrary"),
    )(block_exp, gather_tok, gather_tok, gather_tok, scatter_row, scatter_row, scatter_row, x_tiles,
      w_gu, b_gu, w_dn, b_dn, w_gu, b_gu, w_dn, b_dn)


def _combine_ln_kernel(y0_ref, y1_ref, y2_ref, y3_ref, gate_ref, x_ref, ln_ref, o_ref):
    gate = gate_ref[...]
    tm, D = x_ref.shape
    ffn = jnp.zeros(x_ref.shape, F32)
    for kx, y_ref in enumerate((y0_ref, y1_ref, y2_ref, y3_ref)):
        ffn = ffn + _from_row_tiles(y_ref, tm, D // LANES) * gate[:, kx:kx + 1]
    z = DEEPNORM_ALPHA * x_ref[...] + ffn
    o_ref[...] = _layer_norm(z, ln_ref[0:1, :], ln_ref[1:2, :])


def _combine_ln(y_rows, gate_rows, x, ln):
    T, D = x.shape
    tm = _row_tile(T, 512)
    tile = pl.BlockSpec((tm, D), lambda i: (i, 0))
    slot = lambda kx: pl.BlockSpec((tm * (D // LANES), LANES), lambda i: (kx * (T // tm) + i, 0))
    return pl.pallas_call(
        _combine_ln_kernel,
        grid=(T // tm,),
        in_specs=[slot(kx) for kx in range(TOP_K)]
        + [pl.BlockSpec((tm, LANES), lambda i: (i, 0)), tile, pl.BlockSpec(ln.shape, lambda i: (0, 0))],
        out_specs=tile,
        out_shape=jax.ShapeDtypeStruct((T, D), F32),
        compiler_params=_params("parallel"),
    )(y_rows, y_rows, y_rows, y_rows, gate_rows, x, ln)


def _routing_plan(top_idx, counts):
    T = top_idx.shape[0]
    TK = T * TOP_K
    order = jnp.argsort(top_idx.reshape(TK)).astype(jnp.int32)
    padded = (counts + EXPERT_BLOCK - 1) // EXPERT_BLOCK * EXPERT_BLOCK
    pad_end = jnp.cumsum(padded)
    pad_start = pad_end - padded
    grp_start = jnp.cumsum(counts) - counts
    spare_start = TK + jnp.cumsum(padded - counts) - (padded - counts)
    n_blocks = -(-TK // EXPERT_BLOCK) + N_EXPERTS
    n_rows = n_blocks * EXPERT_BLOCK
    block_start = jnp.arange(n_blocks, dtype=jnp.int32) * EXPERT_BLOCK
    block_exp = jnp.minimum(jnp.sum(pad_end[None, :] <= block_start[:, None], axis=1),
                            N_EXPERTS - 1).astype(jnp.int32)
    off = (block_start - pad_start[block_exp])[:, None] + jnp.arange(EXPERT_BLOCK, dtype=jnp.int32)[None, :]
    cnt = counts[block_exp][:, None]
    valid = off < cnt
    pair = order[jnp.clip(grp_start[block_exp][:, None] + off, 0, TK - 1)]
    tok, slot = pair // TOP_K, pair % TOP_K
    spare = spare_start[block_exp][:, None] + (off - cnt)
    scale = D_MODEL // LANES
    gather_tok = jnp.where(valid, tok, 0).astype(jnp.int32) * scale
    scatter_row = jnp.where(valid, slot * T + tok, spare).astype(jnp.int32) * scale
    fill = jnp.arange(EXPERT_BLOCK, dtype=jnp.int32)[None, :]
    gather_tok = jnp.concatenate([gather_tok, jnp.zeros_like(fill)])
    scatter_row = jnp.concatenate([(n_rows + fill) * scale, scatter_row])
    shape = (n_blocks + 1, 1, EXPERT_BLOCK)
    return gather_tok.reshape(shape), scatter_row.reshape(shape), block_exp, n_rows + 2 * EXPERT_BLOCK


def _moe_ffn_ln(x, x_tiles, router_w, router_b, w_gu, b_gu, w_dn, b_dn, ln):
    T, D = x.shape
    w_pad = jnp.zeros((D, LANES), F32).at[:, :N_EXPERTS].set(router_w)
    w3 = jnp.concatenate(_split2(w_pad), axis=1)
    b_row = jnp.full((1, LANES), NEG_BIG, F32).at[0, :N_EXPERTS].set(router_b)
    idx_rows, gate_rows, counts = _router(x, w3, b_row)
    gather_tok, scatter_row, block_exp, n_out_rows = _routing_plan(idx_rows[:, :TOP_K], counts)
    y_rows = _moe_experts(x_tiles, gather_tok, scatter_row, block_exp, n_out_rows,
                          w_gu, b_gu, w_dn, b_dn)
    return _combine_ln(y_rows, gate_rows, x, ln)


def _cast_kernel(w_ref, o_ref):
    o_ref[...] = w_ref[...].astype(BF16)


def _cast_bf16(w):
    lead = w.shape[:-2]
    w3 = w.reshape((-1,) + w.shape[-2:])
    spec = pl.BlockSpec((1,) + w3.shape[1:], lambda i: (i, 0, 0))
    out = pl.pallas_call(
        _cast_kernel,
        grid=(w3.shape[0],),
        in_specs=[spec],
        out_specs=spec,
        out_shape=jax.ShapeDtypeStruct(w3.shape, BF16),
        compiler_params=_params("parallel"),
    )(w3)
    return out.reshape(lead + w.shape[-2:])


def _pad_rows(rows, n=SUBLANES):
    out = jnp.zeros((n, rows[0].shape[-1]), F32)
    for j, r in enumerate(rows):
        out = out.at[j].set(r.reshape(-1))
    return out


def kernel(x, ln_g, ln_b, rwkv_mix, rwkv_w_rkv, rwkv_w_o, rwkv_w0, rwkv_w1, rwkv_w2, rwkv_a0, rwkv_a1, rwkv_a2, rwkv_g1, rwkv_g2, rwkv_k_k, rwkv_k_a, rwkv_r_k, rwkv_lnx_g, rwkv_lnx_b, rwkv_v0, rwkv_v1, rwkv_v2, kv_w, da_w_q, da_w_o, da_lambda, da_subln_g, moe_router_w, moe_router_b, moe_w_gu, moe_b_gu, moe_w_dn, moe_b_dn):
    B, S, D = x.shape
    T = B * S
    v_first = None
    kv = None
    slopes = 2.0 ** (-8.0 * jnp.arange(1, DA_HEADS + 1, dtype=F32) / DA_HEADS)
    slopes = jnp.repeat(slopes, LANES).reshape(1, DA_HEADS * LANES)
    w_gu_bf16 = _cast_bf16(moe_w_gu)
    w_dn_bf16 = _cast_bf16(moe_w_dn)
    for l in range(DEPTH):
        ln_mix = jnp.stack([ln_g[l, 0], ln_b[l, 0]])
        ln_ffn = jnp.stack([ln_g[l, 1], ln_b[l, 1]])
        xt = x.reshape(T, D)
        if l < N_A_LAYERS:
            zero = jnp.zeros((D,), F32)
            vecs = _pad_rows([rwkv_w0[l], rwkv_a0[l], rwkv_v0[l - 1] if l > 0 else zero])
            vres = None if l == 0 else (rwkv_v1[l - 1].astype(BF16), rwkv_v2[l - 1].astype(BF16), v_first)
            r, lw, k, v, a, g = _rwkv_proj(
                x, _pad_rows(list(rwkv_mix[l])), rwkv_w_rkv[l].astype(BF16),
                rwkv_w1[l].astype(BF16), rwkv_w2[l].astype(BF16), rwkv_a1[l].astype(BF16),
                rwkv_a2[l].astype(BF16), rwkv_g1[l].astype(BF16), rwkv_g2[l].astype(BF16), vecs, vres)
            if l == 0:
                v_first = v
            head_params = _pad_rows([rwkv_k_k[l], rwkv_k_a[l],
                                     jnp.tile(rwkv_r_k[l].reshape(-1), 1), rwkv_lnx_g[l], rwkv_lnx_b[l]])
            y = _wkv(r, lw, k, v, a, head_params)
            xt, xt_tiles = _proj_ln(y.reshape(T, D), g.reshape(T, D), rwkv_w_o[l].astype(BF16), xt, ln_mix)
        else:
            if l == N_A_LAYERS:
                kv = _matmul(xt, kv_w.astype(BF16)).reshape(B, S, 2 * D)
            j = l - N_A_LAYERS
            lambda_init = 0.8 - 0.6 * math.exp(-0.3 * l)
            lam = da_lambda[j].astype(F32)
            lam_full = (jnp.exp(jnp.sum(lam[0] * lam[1])) - jnp.exp(jnp.sum(lam[2] * lam[3]))
                        + lambda_init)
            q = _matmul(xt, da_w_q[j].astype(BF16)).reshape(B, S, D)
            o = _diff_attention(q, kv, slopes, jnp.full((1, LANES), lam_full, F32),
                                da_subln_g[j].reshape(1, LANES), lambda_init)
            xt, xt_tiles = _proj_ln(o.reshape(T, D), None, da_w_o[j].astype(BF16), xt, ln_mix)
        xt = _moe_ffn_ln(xt, xt_tiles, moe_router_w[l], moe_router_b[l], w_gu_bf16[l], moe_b_gu[l],
                         w_dn_bf16[l], moe_b_dn[l], ln_ffn)
        x = xt.reshape(B, S, D)
    return x
```

```python
import functools
import math

import jax
import jax.numpy as jnp
from jax import lax
from jax.experimental import pallas as pl
from jax.experimental.pallas import tpu as pltpu

F32 = jnp.float32
BF16 = jnp.bfloat16

D_MODEL = 1024
DEPTH = 4
N_A_LAYERS = DEPTH // 2
RWKV_HEAD_DIM = 64
RWKV_GN_EPS = 64e-5
L2_EPS = 1e-12
DA_HEAD_DIM = 64
DA_HEADS = D_MODEL // (2 * DA_HEAD_DIM)
SUBLN_EPS = 1e-5
N_EXPERTS = 32
TOP_K = 4
EXPERT_FF = D_MODEL
SWIGLU_LIMIT = 7.0
SWIGLU_ALPHA = 1.702
EXPERT_BLOCK = 512
DEEPNORM_ALPHA = (2 * DEPTH) ** 0.25
LN_EPS = 1e-5

LANES = 128
SUBLANES = 8
VMEM_LIMIT_BYTES = 56 * 1024 * 1024
DMA_PRIORITIES = 2

WKV_CHUNK = 64
WKV_TILES_PER_STEP = 8
HEADS_PER_TILE = LANES // RWKV_HEAD_DIM
NEG_BIG = -1e30
ATTN_ONES_ROWS = 16
ATTN_HEADS_PER_STEP = 4


def _params(*sem):
    return pltpu.CompilerParams(dimension_semantics=sem, vmem_limit_bytes=VMEM_LIMIT_BYTES)


def _dot(a, b):
    return jnp.dot(a.astype(BF16), b.astype(BF16), preferred_element_type=F32)


def _dot_nt(a, b):
    return lax.dot_general(a.astype(BF16), b.astype(BF16), (((1,), (1,)), ((), ())),
                           preferred_element_type=F32)


def _split2(x):
    hi = x.astype(BF16)
    return hi, (x - hi.astype(F32)).astype(BF16)


def _dot_exact_lhs(m_bf16, x):
    hi, lo = _split2(x)
    return (jnp.dot(m_bf16, hi, preferred_element_type=F32)
            + jnp.dot(m_bf16, lo, preferred_element_type=F32))


def _layer_norm(z, g, b):
    mu = jnp.mean(z, axis=-1, keepdims=True)
    d = z - mu
    var = jnp.mean(d * d, axis=-1, keepdims=True)
    return d * lax.rsqrt(var + LN_EPS) * g + b


def _row_tile(n, want):
    t = min(want, n)
    assert n % t == 0 and t % SUBLANES == 0
    return t


def _rwkv_proj_kernel(has_vres, *refs):
    if has_vres:
        (x_ref, xp_ref, mix_ref, wrkv_ref, w1_ref, w2_ref, a1_ref, a2_ref, g1_ref, g2_ref,
         vec_ref, v1_ref, v2_ref, vfirst_ref,
         r_out, lw_out, k_out, v_out, a_out, g_out) = refs
    else:
        (x_ref, xp_ref, mix_ref, wrkv_ref, w1_ref, w2_ref, a1_ref, a2_ref, g1_ref, g2_ref,
         vec_ref, r_out, lw_out, k_out, v_out, a_out, g_out) = refs
    i = pl.program_id(1)
    x = x_ref[0]
    prev = xp_ref[0][SUBLANES - 1:SUBLANES, :]
    prev = jnp.where(i == 0, 0.0, prev)
    row = lax.broadcasted_iota(jnp.int32, x.shape, 0)
    x_prev = jnp.where(row == 0, prev, pltpu.roll(x, 1, axis=0))
    xx = x_prev - x

    def xm(j):
        return x + xx * mix_ref[j:j + 1, :]

    xv = xm(2)
    r_out[0] = _dot(xm(0), wrkv_ref[0])
    k_out[0] = _dot(xm(1), wrkv_ref[1])
    v = _dot(xv, wrkv_ref[2])
    wl = vec_ref[0:1, :] + _dot(jnp.tanh(_dot(xm(3), w1_ref[...])), w2_ref[...])
    lw_out[0] = -jax.nn.sigmoid(wl) * math.exp(-0.5)
    a_out[0] = jax.nn.sigmoid(vec_ref[1:2, :] + _dot(_dot(xm(4), a1_ref[...]), a2_ref[...]))
    g_out[0] = _dot(jax.nn.sigmoid(_dot(xm(5), g1_ref[...])), g2_ref[...])
    if has_vres:
        mixv = jax.nn.sigmoid(vec_ref[2:3, :] + _dot(_dot(xv, v1_ref[...]), v2_ref[...]))
        v = v + (vfirst_ref[0] - v) * mixv
    v_out[0] = v


def _rwkv_proj(x, mix, w_rkv, w1, w2, a1, a2, g1, g2, vecs, vres):
    B, S, D = x.shape
    tm = _row_tile(S, 512)
    has_vres = vres is not None
    full = lambda a: pl.BlockSpec(a.shape, lambda b, i: (0,) * a.ndim)
    tile = pl.BlockSpec((1, tm, D), lambda b, i: (b, i, 0))
    prev = pl.BlockSpec((1, SUBLANES, D),
                        lambda b, i: (b, jnp.maximum(i * (tm // SUBLANES) - 1, 0), 0))
    ins = [x, x, mix, w_rkv, w1, w2, a1, a2, g1, g2, vecs]
    specs = [tile, prev] + [full(a) for a in ins[2:]]
    if has_vres:
        v1, v2, v_first = vres
        ins += [v1, v2, v_first]
        specs += [full(v1), full(v2), tile]
    out = jax.ShapeDtypeStruct((B, S, D), F32)
    return pl.pallas_call(
        functools.partial(_rwkv_proj_kernel, has_vres),
        grid=(B, S // tm),
        in_specs=specs,
        out_specs=[tile] * 6,
        out_shape=[out] * 6,
        compiler_params=_params("parallel", "arbitrary"),
    )(*ins)


def _wkv_kernel(r_ref, lw_ref, k_ref, v_ref, a_ref, hp_ref, y_out,
                state, avec_s, bvec_s, kmod_s, y_s):
    C = WKV_CHUNK
    ts = r_ref.shape[1]
    n_tiles = r_ref.shape[2] // LANES
    n_chunks = ts // C
    R2 = HEADS_PER_TILE * C
    inv_n = 1.0 / RWKV_HEAD_DIM

    @pl.when(pl.program_id(2) == 0)
    def _():
        state[...] = jnp.zeros_like(state)

    li = lax.broadcasted_iota(jnp.int32, (LANES, LANES), 0)
    lj = lax.broadcasted_iota(jnp.int32, (LANES, LANES), 1)
    head_ones = (li // RWKV_HEAD_DIM == lj // RWKV_HEAD_DIM).astype(BF16)

    for t in range(n_tiles):
        ls = slice(t * LANES, (t + 1) * LANES)
        k = k_ref[0, :, ls]
        a = a_ref[0, :, ls]
        kk = k * hp_ref[0:1, ls]
        n2 = _dot(kk * kk, head_ones)
        kk = kk / jnp.maximum(jnp.sqrt(n2), L2_EPS)
        avec_s[:, ls] = -kk
        bvec_s[:, ls] = kk * a
        kmod_s[:, ls] = k * (1.0 + (a - 1.0) * hp_ref[1:2, ls])

    ri = lax.broadcasted_iota(jnp.int32, (R2, R2), 0)
    rj = lax.broadcasted_iota(jnp.int32, (R2, R2), 1)
    same_head = (ri // C) == (rj // C)
    strict = same_head & ((ri % C) > (rj % C))
    incl = same_head & ((ri % C) >= (rj % C))
    ci = lax.broadcasted_iota(jnp.int32, (C, C), 0)
    cj = lax.broadcasted_iota(jnp.int32, (C, C), 1)
    tri = (ci >= cj).astype(BF16)
    lane = lax.broadcasted_iota(jnp.int32, (C, LANES), 1)
    head0 = lane < RWKV_HEAD_DIM
    n_doublings = int(math.log2(C))

    def stack(t):
        return jnp.concatenate([jnp.where(head0, t, 0.0), jnp.where(head0, 0.0, t)], axis=0)

    tiles = range(n_tiles)
    lanes_of = lambda t: slice(t * LANES, (t + 1) * LANES)

    def chunk(c, carry):
        sl = pl.ds(pl.multiple_of(c * C, C), C)
        each = lambda f: [f(t) for t in tiles]
        s0 = each(lambda t: state[t])
        lw = each(lambda t: lw_ref[0, sl, lanes_of(t)])
        cum = each(lambda t: _dot_exact_lhs(tri, lw[t]))
        tot = each(lambda t: cum[t][C - 1:C, :])
        e_neg = each(lambda t: jnp.exp(-cum[t]))
        av = each(lambda t: avec_s[sl, lanes_of(t)])
        bv = each(lambda t: bvec_s[sl, lanes_of(t)])
        km = each(lambda t: kmod_s[sl, lanes_of(t)])
        lhs = each(lambda t: jnp.concatenate(
            [stack(av[t] * jnp.exp(cum[t] - lw[t])),
             stack(r_ref[0, sl, lanes_of(t)] * jnp.exp(cum[t]))], axis=0))
        rhs = each(lambda t: jnp.concatenate([stack(bv[t] * e_neg[t]), stack(km[t] * e_neg[t])], axis=0))
        g = each(lambda t: _dot_nt(lhs[t], rhs[t]))
        a_s = each(lambda t: _dot_nt(lhs[t], s0[t]))
        vm = each(lambda t: stack(v_ref[0, sl, lanes_of(t)]))
        akv = each(lambda t: _dot(jnp.where(strict, g[t][:R2, R2:], 0.0), vm[t]))
        u = each(lambda t: a_s[t][:R2] + akv[t])
        p = each(lambda t: jnp.where(strict, g[t][:R2, :R2], 0.0))
        for it in range(n_doublings):
            pu = each(lambda t: _dot(p[t], u[t]))
            if it + 1 < n_doublings:
                p = each(lambda t: _dot(p[t], p[t]))
            u = each(lambda t: u[t] + pu[t])
        uv = each(lambda t: jnp.concatenate([u[t], vm[t]], axis=0))
        w_y = each(lambda t: jnp.concatenate([jnp.where(incl, g[t][R2:, :R2], 0.0),
                                              jnp.where(incl, g[t][R2:, R2:], 0.0)], axis=1))
        y_sm = each(lambda t: a_s[t][R2:] + _dot(w_y[t], uv[t]))
        bk_end = each(lambda t: jnp.concatenate(
            [stack(bv[t] * jnp.exp(tot[t] - cum[t])), stack(km[t] * jnp.exp(tot[t] - cum[t]))], axis=0))
        s1 = each(lambda t: s0[t] * jnp.exp(tot[t]) + _dot(uv[t].T, bk_end[t]))
        for t in tiles:
            y_s[sl, lanes_of(t)] = y_sm[t][:C] + y_sm[t][C:]
            state[t] = s1[t]
        return carry

    lax.fori_loop(0, n_chunks, chunk, 0)

    for t in range(n_tiles):
        ls = slice(t * LANES, (t + 1) * LANES)
        y = y_s[:, ls]
        mu = _dot(y, head_ones) * inv_n
        d = y - mu
        var = _dot(d * d, head_ones) * inv_n
        yn = d * lax.rsqrt(var + RWKV_GN_EPS) * hp_ref[3:4, ls] + hp_ref[4:5, ls]
        bonus = _dot(r_ref[0, :, ls] * kmod_s[:, ls] * hp_ref[2:3, ls], head_ones)
        y_out[0, :, ls] = yn + bonus * v_ref[0, :, ls]


def _wkv(r, lw, k, v, a, head_params):
    B, S, D = r.shape
    ts = _row_tile(S, 512)
    assert ts % WKV_CHUNK == 0
    width = WKV_TILES_PER_STEP * LANES
    tile = pl.BlockSpec((1, ts, width), lambda b, p, i: (b, i, p))
    hp = pl.BlockSpec((SUBLANES, width), lambda b, p, i: (0, p))
    return pl.pallas_call(
        _wkv_kernel,
        grid=(B, D // width, S // ts),
        in_specs=[tile] * 5 + [hp],
        out_specs=tile,
        out_shape=jax.ShapeDtypeStruct((B, S, D), F32),
        scratch_shapes=[pltpu.VMEM((WKV_TILES_PER_STEP, LANES, LANES), F32)]
        + [pltpu.VMEM((ts, width), F32)] * 4,
        compiler_params=_params("parallel", "parallel", "arbitrary"),
    )(r, lw, k, v, a, head_params)


def _matmul_kernel(a_ref, w_ref, o_ref):
    o_ref[...] = _dot(a_ref[...], w_ref[...])


def _matmul(a, w):
    T, K = a.shape
    N = w.shape[1]
    tm = _row_tile(T, 512)
    return pl.pallas_call(
        _matmul_kernel,
        grid=(T // tm,),
        in_specs=[pl.BlockSpec((tm, K), lambda i: (i, 0)), pl.BlockSpec((K, N), lambda i: (0, 0))],
        out_specs=pl.BlockSpec((tm, N), lambda i: (i, 0)),
        out_shape=jax.ShapeDtypeStruct((T, N), F32),
        compiler_params=_params("parallel"),
    )(a, w)


def _to_row_tiles(ref, value):
    n = value.shape[1] // LANES
    for c in range(n):
        ref[pl.ds(c, value.shape[0], stride=n), :] = value[:, c * LANES:(c + 1) * LANES]


def _from_row_tiles(ref, n_rows, n):
    return jnp.concatenate([ref[pl.ds(c, n_rows, stride=n), :] for c in range(n)], axis=1)


def _proj_ln_kernel(has_gate, *refs):
    if has_gate:
        a_ref, g_ref, w_ref, x_ref, ln_ref, o_ref, o_tiles_ref = refs
        a = a_ref[...] * g_ref[...]
    else:
        a_ref, w_ref, x_ref, ln_ref, o_ref, o_tiles_ref = refs
        a = a_ref[...]
    z = DEEPNORM_ALPHA * x_ref[...] + _dot(a, w_ref[...])
    out = _layer_norm(z, ln_ref[0:1, :], ln_ref[1:2, :])
    o_ref[...] = out
    _to_row_tiles(o_tiles_ref, out)


def _proj_ln(a, gate, w, x, ln):
    T, D = x.shape
    tm = _row_tile(T, 512)
    n = D // LANES
    tile = pl.BlockSpec((tm, D), lambda i: (i, 0))
    full = lambda t: pl.BlockSpec(t.shape, lambda i: (0, 0))
    has_gate = gate is not None
    ins = [a] + ([gate] if has_gate else []) + [w, x, ln]
    specs = [tile] + ([tile] if has_gate else []) + [full(w), tile, full(ln)]
    return pl.pallas_call(
        functools.partial(_proj_ln_kernel, has_gate),
        grid=(T // tm,),
        in_specs=specs,
        out_specs=[tile, pl.BlockSpec((tm * n, LANES), lambda i: (i, 0))],
        out_shape=[jax.ShapeDtypeStruct((T, D), F32), jax.ShapeDtypeStruct((T * n, LANES), F32)],
        compiler_params=_params("parallel"),
    )(*ins)


def _max_rows(s, groups=4):
    step = s.shape[0] // groups
    part = s[0:step]
    for g in range(1, groups):
        part = jnp.maximum(part, s[g * step:(g + 1) * step])
    return jnp.max(part, axis=0, keepdims=True)


def _attn_kernel(out_scale, q_ref, k_ref, v_ref, slope_ref, lam_ref, g_ref, o_ref,
                 k_s, vt_s, acc_s):
    i = pl.program_id(2)
    tq = q_ref.shape[1]
    tk = tq
    n_heads = q_ref.shape[2] // LANES
    n_kv_blocks = k_ref.shape[1] // tk
    heads = range(n_heads)
    chains = [(h, c) for h in heads for c in range(2)]
    lanes_of = lambda h: slice(h * LANES, (h + 1) * LANES)

    @pl.when(i == 0)
    def _():
        lane = lax.broadcasted_iota(jnp.int32, (tk, LANES), 1)
        for h in heads:
            for blk in range(n_kv_blocks):
                sl = slice(blk * tk, (blk + 1) * tk)
                kb = k_ref[0, sl, lanes_of(h)]
                k_s[2 * h, sl, :] = jnp.where(lane < DA_HEAD_DIM, kb, 0.0).astype(BF16)
                k_s[2 * h + 1, sl, :] = jnp.where(lane < DA_HEAD_DIM, 0.0, kb).astype(BF16)
                vt_s[h, blk, 0:LANES, :] = v_ref[0, sl, lanes_of(h)].T.astype(BF16)
                vt_s[h, blk, LANES:, :] = jnp.ones((ATTN_ONES_ROWS, tk), BF16)

    q = [(q_ref[0, :, lanes_of(h)] * (DA_HEAD_DIM ** -0.5)).astype(BF16) for h in heads]
    slope = [slope_ref[0, :, h * LANES:h * LANES + 1] for h in heads]
    k_loc = lax.broadcasted_iota(jnp.int32, (tk, tq), 0)
    q_loc = lax.broadcasted_iota(jnp.int32, (tk, tq), 1)
    k_loc_f = k_loc.astype(F32)
    acc_s[...] = jnp.zeros_like(acc_s)

    def block(j, ms, diagonal):
        sl = pl.ds(pl.multiple_of(j * tk, tk), tk)
        rel = ((j - i) * tk).astype(F32)
        each = lambda f: [f(n, h) for n, (h, _) in enumerate(chains)]
        s = each(lambda n, h: _dot_nt(k_s[n, sl, :], q[h]) + slope[h] * k_loc_f)
        if diagonal:
            s = each(lambda n, h: jnp.where(k_loc > q_loc, NEG_BIG, s[n]))
        offset = [slope[h] * rel for h in heads]
        m_new = each(lambda n, h: jnp.maximum(ms[n], _max_rows(s[n]) + offset[h]))
        p = each(lambda n, h: jnp.exp(s[n] - (m_new[n] - offset[h])).astype(BF16))
        acc_old = each(lambda n, h: acc_s[n])
        pv = each(lambda n, h: jnp.dot(vt_s[h, j], p[n], preferred_element_type=F32))
        for n in range(len(chains)):
            acc_s[n] = jnp.exp(ms[n] - m_new[n]) * acc_old[n] + pv[n]
        return tuple(m_new)

    init = tuple(jnp.full((1, tq), NEG_BIG, F32) for _ in chains)
    ms = lax.fori_loop(0, i, lambda j, ms: block(j, ms, False), init)
    block(i, ms, True)

    for h in heads:
        acc0, acc1 = acc_s[2 * h], acc_s[2 * h + 1]
        o_t = (acc0[:LANES] / acc0[LANES:LANES + 1]
               - lam_ref[:, 0:1] * (acc1[:LANES] / acc1[LANES:LANES + 1]))
        o = o_t.T
        o = o * lax.rsqrt(jnp.mean(o * o, axis=-1, keepdims=True) + SUBLN_EPS) * g_ref[...]
        o_ref[0, :, lanes_of(h)] = o * out_scale


def _diff_attention(q, kv, slopes, lam_row, subln_g, lambda_init):
    B, S, D = q.shape
    tq = _row_tile(S, 256)
    nh = ATTN_HEADS_PER_STEP
    width = nh * LANES
    n_groups = DA_HEADS // nh
    return pl.pallas_call(
        functools.partial(_attn_kernel, 1.0 - lambda_init),
        grid=(B, n_groups, S // tq),
        in_specs=[pl.BlockSpec((1, tq, width), lambda b, h, i: (b, i, h)),
                  pl.BlockSpec((1, S, width), lambda b, h, i: (b, 0, h)),
                  pl.BlockSpec((1, S, width), lambda b, h, i: (b, 0, n_groups + h)),
                  pl.BlockSpec((1, 1, width), lambda b, h, i: (h, 0, 0)),
                  pl.BlockSpec((1, LANES), lambda b, h, i: (0, 0)),
                  pl.BlockSpec((1, LANES), lambda b, h, i: (0, 0))],
        out_specs=pl.BlockSpec((1, tq, width), lambda b, h, i: (b, i, h)),
        out_shape=jax.ShapeDtypeStruct((B, S, D), F32),
        scratch_shapes=[pltpu.VMEM((2 * nh, S, LANES), BF16),
                        pltpu.VMEM((nh, S // tq, LANES + ATTN_ONES_ROWS, tq), BF16),
                        pltpu.VMEM((2 * nh, LANES + ATTN_ONES_ROWS, tq), F32)],
        compiler_params=_params("parallel", "parallel", "arbitrary"),
    )(q, kv, kv, slopes.reshape(n_groups, 1, width), lam_row, subln_g)


def _router_kernel(x_ref, w_ref, b_ref, idx_out, gate_out, count_out):
    @pl.when(pl.program_id(0) == 0)
    def _():
        count_out[...] = jnp.zeros_like(count_out)

    x = x_ref[...]
    xh, xl = _split2(x)
    w = w_ref[...]
    hh = jnp.dot(xh, w, preferred_element_type=F32)
    lh = jnp.dot(xl, w[:, :LANES], preferred_element_type=F32)
    logits = hh[:, :LANES] + (hh[:, LANES:] + lh) + b_ref[...]
    lane = lax.broadcasted_iota(jnp.int32, logits.shape, 1)
    vals, idxs = [], []
    cur = logits
    for _ in range(TOP_K):
        m = jnp.max(cur, axis=-1, keepdims=True)
        idx = jnp.min(jnp.where(cur == m, lane, LANES), axis=-1, keepdims=True)
        cur = jnp.where(lane == idx, -jnp.inf, cur)
        vals.append(m)
        idxs.append(idx)
    es = [jnp.exp(v - vals[0]) for v in vals]
    denom = es[0] + es[1] + es[2] + es[3]
    idx_row = jnp.zeros(logits.shape, jnp.int32)
    gate_row = jnp.zeros(logits.shape, F32)
    for kx in range(TOP_K):
        idx_row = jnp.where(lane == kx, idxs[kx], idx_row)
        gate_row = jnp.where(lane == kx, es[kx] / denom, gate_row)
    idx_out[...] = idx_row
    gate_out[...] = gate_row
    picked = sum((lane == idx).astype(jnp.int32) for idx in idxs)
    count_out[0:1, :] += jnp.sum(picked, axis=0, keepdims=True)


def _router(x, w3, b_row):
    T, D = x.shape
    tm = _row_tile(T, 512)
    tile = pl.BlockSpec((tm, D), lambda i: (i, 0))
    out_tile = pl.BlockSpec((tm, LANES), lambda i: (i, 0))
    idx_rows, gate_rows, counts = pl.pallas_call(
        _router_kernel,
        grid=(T // tm,),
        in_specs=[tile, pl.BlockSpec(w3.shape, lambda i: (0, 0)),
                  pl.BlockSpec(b_row.shape, lambda i: (0, 0))],
        out_specs=[out_tile, out_tile, pl.BlockSpec((SUBLANES, LANES), lambda i: (0, 0))],
        out_shape=[jax.ShapeDtypeStruct((T, LANES), jnp.int32),
                   jax.ShapeDtypeStruct((T, LANES), F32),
                   jax.ShapeDtypeStruct((SUBLANES, LANES), jnp.int32)],
        compiler_params=_params("arbitrary"),
    )(x, w3, b_row)
    return idx_rows, gate_rows, counts[0, :N_EXPERTS]


def _clamped_swiglu(gate, up):
    gate = jnp.minimum(gate, SWIGLU_LIMIT)
    up = jnp.clip(up, -SWIGLU_LIMIT, SWIGLU_LIMIT)
    return (up + 1.0) * (gate * jax.nn.sigmoid(SWIGLU_ALPHA * gate))


def _moe_kernel(bexp_ref, glast_ref, g_first, g_b, g_next, s_prev, s_a, s_b, x_hbm,
                wgu_a, bgu_a, wdn_a, bdn_a, wgu_b, bgu_b, wdn_b, bdn_b, y_hbm,
                xbuf_a, xbuf_b, ybuf_a, ybuf_b, zbuf, sem_ga, sem_gb, sem_sa, sem_sb, sem_z):
    del bexp_ref
    g = pl.program_id(0)
    last = glast_ref[0]

    n = wdn_a.shape[2] // LANES
    tile_of = lambda r: pl.ds(pl.multiple_of(r, n), n)

    def gather(idx_ref, xbuf, sem, j):
        return pltpu.make_async_copy(x_hbm.at[tile_of(idx_ref[0, 0, j])], xbuf.at[pl.ds(j * n, n)], sem)

    def scatter(idx_ref, ybuf, sem, j):
        return pltpu.make_async_copy(ybuf.at[pl.ds(j * n, n)], y_hbm.at[tile_of(idx_ref[0, 0, j])], sem)

    def wait_gather(xbuf, sem):
        pltpu.make_async_copy(x_hbm.at[pl.ds(0, EXPERT_BLOCK * n)], xbuf, sem).wait()

    def wait_scatter(ybuf, sem):
        pltpu.make_async_copy(ybuf, y_hbm.at[pl.ds(0, EXPERT_BLOCK * n)], sem).wait()

    def expert_phase(xbuf, ybuf, wgu, bgu, wdn, bdn, sem_x, sem_y, gather_next, scatter_done):
        ff = wdn.shape[1]
        for j in range(EXPERT_BLOCK):
            gather_next(j).start(priority=j % DMA_PRIORITIES)
        wait_gather(xbuf, sem_x)
        h = _dot(_from_row_tiles(xbuf, EXPERT_BLOCK, n), wgu[0]) + bgu[0]
        act = _clamped_swiglu(h[:, :ff], h[:, ff:])
        for j in range(EXPERT_BLOCK):
            scatter_done(j).start(priority=j % DMA_PRIORITIES)
        wait_scatter(ybuf, sem_y)
        _to_row_tiles(ybuf, _dot(act, wdn[0]) + bdn[0])

    @pl.when(g == 0)
    def _():
        ybuf_a[...] = jnp.zeros_like(ybuf_a)
        ybuf_b[...] = jnp.zeros_like(ybuf_b)
        zbuf[...] = jnp.zeros_like(zbuf)
        spare0 = y_hbm.shape[0] - EXPERT_BLOCK * n

        def first(j, c):
            gather(g_first, xbuf_a, sem_ga, j).start()
            pltpu.make_async_copy(ybuf_a.at[pl.ds(j * n, n)],
                                  y_hbm.at[pl.ds(pl.multiple_of(spare0 + j * n, n), n)], sem_sa).start()
            return c
        lax.fori_loop(0, EXPERT_BLOCK, first, 0)

    @pl.when(g <= last)
    def _():
        expert_phase(xbuf_a, ybuf_a, wgu_a, bgu_a, wdn_a, bdn_a, sem_ga, sem_sa,
                     lambda j: gather(g_b, xbuf_b, sem_gb, j),
                     lambda j: scatter(s_prev, ybuf_b, sem_sb, j))
        expert_phase(xbuf_b, ybuf_b, wgu_b, bgu_b, wdn_b, bdn_b, sem_gb, sem_sb,
                     lambda j: gather(g_next, xbuf_a, sem_ga, j),
                     lambda j: scatter(s_a, ybuf_a, sem_sa, j))

    @pl.when(g > last)
    def _():
        fills = [pltpu.make_async_copy(zbuf, y_hbm.at[pl.ds(pl.multiple_of(plan[0, 0, 0], n),
                                                           EXPERT_BLOCK * n)], sem_z)
                 for plan in (s_a, s_b)]
        for f in fills:
            f.start()
        for f in fills:
            f.wait()

    @pl.when(g == last)
    def _():
        def final(j, c):
            scatter(s_b, ybuf_b, sem_sb, j).start()
            return c
        lax.fori_loop(0, EXPERT_BLOCK, final, 0)
        wait_gather(xbuf_a, sem_ga)
        wait_scatter(ybuf_a, sem_sa)
        wait_scatter(ybuf_b, sem_sb)


def _moe_experts(x_tiles, gather_tok, scatter_row, block_exp, last_step, n_out_rows, w_gu, b_gu, w_dn, b_dn):
    D = w_gu.shape[1]
    n = D // LANES
    n_blocks = block_exp.shape[0]
    assert n_blocks % 2 == 0
    FF2 = w_gu.shape[2]
    idx_spec = lambda fn: pl.BlockSpec((1, 1, EXPERT_BLOCK), fn, memory_space=pltpu.SMEM)
    weights = lambda off: [
        pl.BlockSpec((1, D, FF2), lambda g, be, gl: (be[2 * g + off], 0, 0)),
        pl.BlockSpec((1, 1, FF2), lambda g, be, gl: (be[2 * g + off], 0, 0)),
        pl.BlockSpec((1, EXPERT_FF, D), lambda g, be, gl: (be[2 * g + off], 0, 0)),
        pl.BlockSpec((1, 1, D), lambda g, be, gl: (be[2 * g + off], 0, 0)),
    ]
    grid_spec = pltpu.PrefetchScalarGridSpec(
        num_scalar_prefetch=2,
        grid=(n_blocks // 2,),
        in_specs=[
            idx_spec(lambda g, be, gl: (0, 0, 0)),
            idx_spec(lambda g, be, gl: (2 * g + 1, 0, 0)),
            idx_spec(lambda g, be, gl: (2 * g + 2, 0, 0)),
            idx_spec(lambda g, be, gl: (2 * g, 0, 0)),
            idx_spec(lambda g, be, gl: (2 * g + 1, 0, 0)),
            idx_spec(lambda g, be, gl: (2 * g + 2, 0, 0)),
            pl.BlockSpec(memory_space=pl.ANY),
        ] + weights(0) + weights(1),
        out_specs=pl.BlockSpec(memory_space=pl.ANY),
        scratch_shapes=[pltpu.VMEM((EXPERT_BLOCK * n, LANES), F32)] * 5
        + [pltpu.SemaphoreType.DMA(())] * 5,
    )
    b_gu = b_gu.reshape(N_EXPERTS, 1, FF2)
    b_dn = b_dn.reshape(N_EXPERTS, 1, D)
    return pl.pallas_call(
        _moe_kernel,
        grid_spec=grid_spec,
        out_shape=jax.ShapeDtypeStruct((n_out_rows * n, LANES), F32),
        compiler_params=_params("arbitrary"),
    )(block_exp, last_step, gather_tok, gather_tok, gather_tok, scatter_row, scatter_row, scatter_row, x_tiles,
      w_gu, b_gu, w_dn, b_dn, w_gu, b_gu, w_dn, b_dn)


def _combine_ln_kernel(y0_ref, y1_ref, y2_ref, y3_ref, gate_ref, x_ref, ln_ref, o_ref):
    gate = gate_ref[...]
    tm, D = x_ref.shape
    ffn = jnp.zeros(x_ref.shape, F32)
    for kx, y_ref in enumerate((y0_ref, y1_ref, y2_ref, y3_ref)):
        ffn = ffn + _from_row_tiles(y_ref, tm, D // LANES) * gate[:, kx:kx + 1]
    z = DEEPNORM_ALPHA * x_ref[...] + ffn
    o_ref[...] = _layer_norm(z, ln_ref[0:1, :], ln_ref[1:2, :])


def _combine_ln(y_rows, gate_rows, x, ln):
    T, D = x.shape
    tm = _row_tile(T, 512)
    tile = pl.BlockSpec((tm, D), lambda i: (i, 0))
    slot = lambda kx: pl.BlockSpec((tm * (D // LANES), LANES), lambda i: (kx * (T // tm) + i, 0))
    return pl.pallas_call(
        _combine_ln_kernel,
        grid=(T // tm,),
        in_specs=[slot(kx) for kx in range(TOP_K)]
        + [pl.BlockSpec((tm, LANES), lambda i: (i, 0)), tile, pl.BlockSpec(ln.shape, lambda i: (0, 0))],
        out_specs=tile,
        out_shape=jax.ShapeDtypeStruct((T, D), F32),
        compiler_params=_params("parallel"),
    )(y_rows, y_rows, y_rows, y_rows, gate_rows, x, ln)


def _routing_plan(top_idx, counts):
    T = top_idx.shape[0]
    TK = T * TOP_K
    order = jnp.argsort(top_idx.reshape(TK)).astype(jnp.int32)
    padded = (counts + EXPERT_BLOCK - 1) // EXPERT_BLOCK * EXPERT_BLOCK
    pad_end = jnp.cumsum(padded)
    pad_start = pad_end - padded
    grp_start = jnp.cumsum(counts) - counts
    spare_start = TK + jnp.cumsum(padded - counts) - (padded - counts)
    n_blocks = -(-TK // EXPERT_BLOCK) + N_EXPERTS
    n_rows = n_blocks * EXPERT_BLOCK
    block_start = jnp.arange(n_blocks, dtype=jnp.int32) * EXPERT_BLOCK
    block_exp = jnp.minimum(jnp.sum(pad_end[None, :] <= block_start[:, None], axis=1),
                            N_EXPERTS - 1).astype(jnp.int32)
    off = (block_start - pad_start[block_exp])[:, None] + jnp.arange(EXPERT_BLOCK, dtype=jnp.int32)[None, :]
    cnt = counts[block_exp][:, None]
    valid = off < cnt
    pair = order[jnp.clip(grp_start[block_exp][:, None] + off, 0, TK - 1)]
    tok, slot = pair // TOP_K, pair % TOP_K
    spare = spare_start[block_exp][:, None] + (off - cnt)
    scale = D_MODEL // LANES
    gather_tok = jnp.where(valid, tok, 0).astype(jnp.int32) * scale
    scatter_row = jnp.where(valid, slot * T + tok, spare).astype(jnp.int32) * scale
    fill = jnp.arange(EXPERT_BLOCK, dtype=jnp.int32)[None, :]
    gather_tok = jnp.concatenate([gather_tok, jnp.zeros_like(fill)])
    scatter_row = jnp.concatenate([(n_rows + fill) * scale, scatter_row])
    shape = (n_blocks + 1, 1, EXPERT_BLOCK)
    n_used = jnp.maximum(pad_end[-1] // EXPERT_BLOCK, 1)
    last_step = ((n_used - 1) // 2).astype(jnp.int32).reshape(1)
    return (gather_tok.reshape(shape), scatter_row.reshape(shape), block_exp, last_step,
            n_rows + 2 * EXPERT_BLOCK)


def _moe_ffn_ln(x, x_tiles, router_w, router_b, w_gu, b_gu, w_dn, b_dn, ln):
    T, D = x.shape
    w_pad = jnp.zeros((D, LANES), F32).at[:, :N_EXPERTS].set(router_w)
    w3 = jnp.concatenate(_split2(w_pad), axis=1)
    b_row = jnp.full((1, LANES), NEG_BIG, F32).at[0, :N_EXPERTS].set(router_b)
    idx_rows, gate_rows, counts = _router(x, w3, b_row)
    gather_tok, scatter_row, block_exp, last_step, n_out_rows = _routing_plan(idx_rows[:, :TOP_K], counts)
    y_rows = _moe_experts(x_tiles, gather_tok, scatter_row, block_exp, last_step, n_out_rows,
                          w_gu, b_gu, w_dn, b_dn)
    return _combine_ln(y_rows, gate_rows, x, ln)


def _cast_kernel(w_ref, o_ref):
    o_ref[...] = w_ref[...].astype(BF16)


def _cast_bf16(w):
    lead = w.shape[:-2]
    w3 = w.reshape((-1,) + w.shape[-2:])
    spec = pl.BlockSpec((1,) + w3.shape[1:], lambda i: (i, 0, 0))
    out = pl.pallas_call(
        _cast_kernel,
        grid=(w3.shape[0],),
        in_specs=[spec],
        out_specs=spec,
        out_shape=jax.ShapeDtypeStruct(w3.shape, BF16),
        compiler_params=_params("parallel"),
    )(w3)
    return out.reshape(lead + w.shape[-2:])


def _pad_rows(rows, n=SUBLANES):
    out = jnp.zeros((n, rows[0].shape[-1]), F32)
    for j, r in enumerate(rows):
        out = out.at[j].set(r.reshape(-1))
    return out


def kernel(x, ln_g, ln_b, rwkv_mix, rwkv_w_rkv, rwkv_w_o, rwkv_w0, rwkv_w1, rwkv_w2, rwkv_a0, rwkv_a1, rwkv_a2, rwkv_g1, rwkv_g2, rwkv_k_k, rwkv_k_a, rwkv_r_k, rwkv_lnx_g, rwkv_lnx_b, rwkv_v0, rwkv_v1, rwkv_v2, kv_w, da_w_q, da_w_o, da_lambda, da_subln_g, moe_router_w, moe_router_b, moe_w_gu, moe_b_gu, moe_w_dn, moe_b_dn):
    B, S, D = x.shape
    T = B * S
    v_first = None
    kv = None
    slopes = 2.0 ** (-8.0 * jnp.arange(1, DA_HEADS + 1, dtype=F32) / DA_HEADS)
    slopes = jnp.repeat(slopes, LANES).reshape(1, DA_HEADS * LANES)
    w_gu_bf16 = _cast_bf16(moe_w_gu)
    w_dn_bf16 = _cast_bf16(moe_w_dn)
    for l in range(DEPTH):
        ln_mix = jnp.stack([ln_g[l, 0], ln_b[l, 0]])
        ln_ffn = jnp.stack([ln_g[l, 1], ln_b[l, 1]])
        xt = x.reshape(T, D)
        if l < N_A_LAYERS:
            zero = jnp.zeros((D,), F32)
            vecs = _pad_rows([rwkv_w0[l], rwkv_a0[l], rwkv_v0[l - 1] if l > 0 else zero])
            vres = None if l == 0 else (rwkv_v1[l - 1].astype(BF16), rwkv_v2[l - 1].astype(BF16), v_first)
            r, lw, k, v, a, g = _rwkv_proj(
                x, _pad_rows(list(rwkv_mix[l])), rwkv_w_rkv[l].astype(BF16),
                rwkv_w1[l].astype(BF16), rwkv_w2[l].astype(BF16), rwkv_a1[l].astype(BF16),
                rwkv_a2[l].astype(BF16), rwkv_g1[l].astype(BF16), rwkv_g2[l].astype(BF16), vecs, vres)
            if l == 0:
                v_first = v
            head_params = _pad_rows([rwkv_k_k[l], rwkv_k_a[l],
                                     jnp.tile(rwkv_r_k[l].reshape(-1), 1), rwkv_lnx_g[l], rwkv_lnx_b[l]])
            y = _wkv(r, lw, k, v, a, head_params)
            xt, xt_tiles = _proj_ln(y.reshape(T, D), g.reshape(T, D), rwkv_w_o[l].astype(BF16), xt, ln_mix)
        else:
            if l == N_A_LAYERS:
                kv = _matmul(xt, kv_w.astype(BF16)).reshape(B, S, 2 * D)
            j = l - N_A_LAYERS
            lambda_init = 0.8 - 0.6 * math.exp(-0.3 * l)
            lam = da_lambda[j].astype(F32)
            lam_full = (jnp.exp(jnp.sum(lam[0] * lam[1])) - jnp.exp(jnp.sum(lam[2] * lam[3]))
                        + lambda_init)
            q = _matmul(xt, da_w_q[j].astype(BF16)).reshape(B, S, D)
            o = _diff_attention(q, kv, slopes, jnp.full((1, LANES), lam_full, F32),
                                da_subln_g[j].reshape(1, LANES), lambda_init)
            xt, xt_tiles = _proj_ln(o.reshape(T, D), None, da_w_o[j].astype(BF16), xt, ln_mix)
        xt = _moe_ffn_ln(xt, xt_tiles, moe_router_w[l], moe_router_b[l], w_gu_bf16[l], moe_b_gu[l],
                         w_dn_bf16[l], moe_b_dn[l], ln_ffn)
        x = xt.reshape(B, S, D)
    return x
```

```python
import functools
import math

import jax
import jax.numpy as jnp
from jax import lax
from jax.experimental import pallas as pl
from jax.experimental.pallas import tpu as pltpu

F32 = jnp.float32
BF16 = jnp.bfloat16

D_MODEL = 1024
DEPTH = 4
N_A_LAYERS = DEPTH // 2
RWKV_HEAD_DIM = 64
RWKV_GN_EPS = 64e-5
L2_EPS = 1e-12
DA_HEAD_DIM = 64
DA_HEADS = D_MODEL // (2 * DA_HEAD_DIM)
SUBLN_EPS = 1e-5
N_EXPERTS = 32
TOP_K = 4
EXPERT_FF = D_MODEL
SWIGLU_LIMIT = 7.0
SWIGLU_ALPHA = 1.702
EXPERT_BLOCK = 512
DEEPNORM_ALPHA = (2 * DEPTH) ** 0.25
LN_EPS = 1e-5

LANES = 128
SUBLANES = 8
VMEM_LIMIT_BYTES = 56 * 1024 * 1024
DMA_PRIORITIES = 2

WKV_CHUNK = 64
WKV_TILES_PER_STEP = 8
HEADS_PER_TILE = LANES // RWKV_HEAD_DIM
NEG_BIG = -1e30
ATTN_ONES_ROWS = 16
ATTN_HEADS_PER_STEP = 4


def _params(*sem):
    return pltpu.CompilerParams(dimension_semantics=sem, vmem_limit_bytes=VMEM_LIMIT_BYTES)


def _dot(a, b):
    return jnp.dot(a.astype(BF16), b.astype(BF16), preferred_element_type=F32)


def _dot_nt(a, b):
    return lax.dot_general(a.astype(BF16), b.astype(BF16), (((1,), (1,)), ((), ())),
                           preferred_element_type=F32)


def _split2(x):
    hi = x.astype(BF16)
    return hi, (x - hi.astype(F32)).astype(BF16)


def _dot_exact_lhs(m_bf16, x):
    hi, lo = _split2(x)
    return (jnp.dot(m_bf16, hi, preferred_element_type=F32)
            + jnp.dot(m_bf16, lo, preferred_element_type=F32))


def _layer_norm(z, g, b):
    mu = jnp.mean(z, axis=-1, keepdims=True)
    d = z - mu
    var = jnp.mean(d * d, axis=-1, keepdims=True)
    return d * lax.rsqrt(var + LN_EPS) * g + b


def _row_tile(n, want):
    t = min(want, n)
    assert n % t == 0 and t % SUBLANES == 0
    return t


def _rwkv_proj_kernel(has_vres, *refs):
    if has_vres:
        (x_ref, xp_ref, mix_ref, wrkv_ref, w1_ref, w2_ref, a1_ref, a2_ref, g1_ref, g2_ref,
         vec_ref, v1_ref, v2_ref, vfirst_ref,
         r_out, lw_out, k_out, v_out, a_out, g_out) = refs
    else:
        (x_ref, xp_ref, mix_ref, wrkv_ref, w1_ref, w2_ref, a1_ref, a2_ref, g1_ref, g2_ref,
         vec_ref, r_out, lw_out, k_out, v_out, a_out, g_out) = refs
    i = pl.program_id(1)
    x = x_ref[0]
    prev = xp_ref[0][SUBLANES - 1:SUBLANES, :]
    prev = jnp.where(i == 0, 0.0, prev)
    row = lax.broadcasted_iota(jnp.int32, x.shape, 0)
    x_prev = jnp.where(row == 0, prev, pltpu.roll(x, 1, axis=0))
    xx = x_prev - x

    def xm(j):
        return x + xx * mix_ref[j:j + 1, :]

    xv = xm(2)
    r_out[0] = _dot(xm(0), wrkv_ref[0])
    k_out[0] = _dot(xm(1), wrkv_ref[1])
    v = _dot(xv, wrkv_ref[2])
    wl = vec_ref[0:1, :] + _dot(jnp.tanh(_dot(xm(3), w1_ref[...])), w2_ref[...])
    lw_out[0] = -jax.nn.sigmoid(wl) * math.exp(-0.5)
    a_out[0] = jax.nn.sigmoid(vec_ref[1:2, :] + _dot(_dot(xm(4), a1_ref[...]), a2_ref[...]))
    g_out[0] = _dot(jax.nn.sigmoid(_dot(xm(5), g1_ref[...])), g2_ref[...])
    if has_vres:
        mixv = jax.nn.sigmoid(vec_ref[2:3, :] + _dot(_dot(xv, v1_ref[...]), v2_ref[...]))
        v = v + (vfirst_ref[0] - v) * mixv
    v_out[0] = v


def _rwkv_proj(x, mix, w_rkv, w1, w2, a1, a2, g1, g2, vecs, vres):
    B, S, D = x.shape
    tm = _row_tile(S, 512)
    has_vres = vres is not None
    full = lambda a: pl.BlockSpec(a.shape, lambda b, i: (0,) * a.ndim)
    tile = pl.BlockSpec((1, tm, D), lambda b, i: (b, i, 0))
    prev = pl.BlockSpec((1, SUBLANES, D),
                        lambda b, i: (b, jnp.maximum(i * (tm // SUBLANES) - 1, 0), 0))
    ins = [x, x, mix, w_rkv, w1, w2, a1, a2, g1, g2, vecs]
    specs = [tile, prev] + [full(a) for a in ins[2:]]
    if has_vres:
        v1, v2, v_first = vres
        ins += [v1, v2, v_first]
        specs += [full(v1), full(v2), tile]
    out = jax.ShapeDtypeStruct((B, S, D), F32)
    return pl.pallas_call(
        functools.partial(_rwkv_proj_kernel, has_vres),
        grid=(B, S // tm),
        in_specs=specs,
        out_specs=[tile] * 6,
        out_shape=[out] * 6,
        compiler_params=_params("parallel", "arbitrary"),
    )(*ins)


def _wkv_kernel(r_ref, lw_ref, k_ref, v_ref, a_ref, hp_ref, y_out,
                state, avec_s, bvec_s, kmod_s, y_s):
    C = WKV_CHUNK
    ts = r_ref.shape[1]
    n_tiles = r_ref.shape[2] // LANES
    n_chunks = ts // C
    R2 = HEADS_PER_TILE * C
    inv_n = 1.0 / RWKV_HEAD_DIM

    @pl.when(pl.program_id(2) == 0)
    def _():
        state[...] = jnp.zeros_like(state)

    li = lax.broadcasted_iota(jnp.int32, (LANES, LANES), 0)
    lj = lax.broadcasted_iota(jnp.int32, (LANES, LANES), 1)
    head_ones = (li // RWKV_HEAD_DIM == lj // RWKV_HEAD_DIM).astype(BF16)

    for t in range(n_tiles):
        ls = slice(t * LANES, (t + 1) * LANES)
        k = k_ref[0, :, ls]
        a = a_ref[0, :, ls]
        kk = k * hp_ref[0:1, ls]
        n2 = _dot(kk * kk, head_ones)
        kk = kk / jnp.maximum(jnp.sqrt(n2), L2_EPS)
        avec_s[:, ls] = -kk
        bvec_s[:, ls] = kk * a
        kmod_s[:, ls] = k * (1.0 + (a - 1.0) * hp_ref[1:2, ls])

    ri = lax.broadcasted_iota(jnp.int32, (R2, R2), 0)
    rj = lax.broadcasted_iota(jnp.int32, (R2, R2), 1)
    same_head = (ri // C) == (rj // C)
    strict = same_head & ((ri % C) > (rj % C))
    incl = same_head & ((ri % C) >= (rj % C))
    ci = lax.broadcasted_iota(jnp.int32, (C, C), 0)
    cj = lax.broadcasted_iota(jnp.int32, (C, C), 1)
    tri = (ci >= cj).astype(BF16)
    lane = lax.broadcasted_iota(jnp.int32, (C, LANES), 1)
    head0 = lane < RWKV_HEAD_DIM
    n_doublings = int(math.log2(C))

    def stack(t):
        return jnp.concatenate([jnp.where(head0, t, 0.0), jnp.where(head0, 0.0, t)], axis=0)

    tiles = range(n_tiles)
    lanes_of = lambda t: slice(t * LANES, (t + 1) * LANES)

    def chunk(c, carry):
        sl = pl.ds(pl.multiple_of(c * C, C), C)
        each = lambda f: [f(t) for t in tiles]
        s0 = each(lambda t: state[t])
        lw = each(lambda t: lw_ref[0, sl, lanes_of(t)])
        cum = each(lambda t: _dot_exact_lhs(tri, lw[t]))
        tot = each(lambda t: cum[t][C - 1:C, :])
        e_neg = each(lambda t: jnp.exp(-cum[t]))
        av = each(lambda t: avec_s[sl, lanes_of(t)])
        bv = each(lambda t: bvec_s[sl, lanes_of(t)])
        km = each(lambda t: kmod_s[sl, lanes_of(t)])
        lhs = each(lambda t: jnp.concatenate(
            [stack(av[t] * jnp.exp(cum[t] - lw[t])),
             stack(r_ref[0, sl, lanes_of(t)] * jnp.exp(cum[t]))], axis=0))
        rhs = each(lambda t: jnp.concatenate([stack(bv[t] * e_neg[t]), stack(km[t] * e_neg[t])], axis=0))
        g = each(lambda t: _dot_nt(lhs[t], rhs[t]))
        a_s = each(lambda t: _dot_nt(lhs[t], s0[t]))
        vm = each(lambda t: stack(v_ref[0, sl, lanes_of(t)]))
        akv = each(lambda t: _dot(jnp.where(strict, g[t][:R2, R2:], 0.0), vm[t]))
        u = each(lambda t: a_s[t][:R2] + akv[t])
        p = each(lambda t: jnp.where(strict, g[t][:R2, :R2], 0.0))
        for it in range(n_doublings):
            pu = each(lambda t: _dot(p[t], u[t]))
            if it + 1 < n_doublings:
                p = each(lambda t: _dot(p[t], p[t]))
            u = each(lambda t: u[t] + pu[t])
        uv = each(lambda t: jnp.concatenate([u[t], vm[t]], axis=0))
        w_y = each(lambda t: jnp.concatenate([jnp.where(incl, g[t][R2:, :R2], 0.0),
                                              jnp.where(incl, g[t][R2:, R2:], 0.0)], axis=1))
        y_sm = each(lambda t: a_s[t][R2:] + _dot(w_y[t], uv[t]))
        bk_end = each(lambda t: jnp.concatenate(
            [stack(bv[t] * jnp.exp(tot[t] - cum[t])), stack(km[t] * jnp.exp(tot[t] - cum[t]))], axis=0))
        s1 = each(lambda t: s0[t] * jnp.exp(tot[t]) + _dot(uv[t].T, bk_end[t]))
        for t in tiles:
            y_s[sl, lanes_of(t)] = y_sm[t][:C] + y_sm[t][C:]
            state[t] = s1[t]
        return carry

    lax.fori_loop(0, n_chunks, chunk, 0)

    for t in range(n_tiles):
        ls = slice(t * LANES, (t + 1) * LANES)
        y = y_s[:, ls]
        mu = _dot(y, head_ones) * inv_n
        d = y - mu
        var = _dot(d * d, head_ones) * inv_n
        yn = d * lax.rsqrt(var + RWKV_GN_EPS) * hp_ref[3:4, ls] + hp_ref[4:5, ls]
        bonus = _dot(r_ref[0, :, ls] * kmod_s[:, ls] * hp_ref[2:3, ls], head_ones)
        y_out[0, :, ls] = yn + bonus * v_ref[0, :, ls]


def _wkv(r, lw, k, v, a, head_params):
    B, S, D = r.shape
    ts = _row_tile(S, 512)
    assert ts % WKV_CHUNK == 0
    width = WKV_TILES_PER_STEP * LANES
    tile = pl.BlockSpec((1, ts, width), lambda b, p, i: (b, i, p))
    hp = pl.BlockSpec((SUBLANES, width), lambda b, p, i: (0, p))
    return pl.pallas_call(
        _wkv_kernel,
        grid=(B, D // width, S // ts),
        in_specs=[tile] * 5 + [hp],
        out_specs=tile,
        out_shape=jax.ShapeDtypeStruct((B, S, D), F32),
        scratch_shapes=[pltpu.VMEM((WKV_TILES_PER_STEP, LANES, LANES), F32)]
        + [pltpu.VMEM((ts, width), F32)] * 4,
        compiler_params=_params("parallel", "parallel", "arbitrary"),
    )(r, lw, k, v, a, head_params)


def _matmul_kernel(a_ref, w_ref, o_ref):
    o_ref[...] = _dot(a_ref[...], w_ref[...])


def _matmul(a, w):
    T, K = a.shape
    N = w.shape[1]
    tm = _row_tile(T, 512)
    return pl.pallas_call(
        _matmul_kernel,
        grid=(T // tm,),
        in_specs=[pl.BlockSpec((tm, K), lambda i: (i, 0)), pl.BlockSpec((K, N), lambda i: (0, 0))],
        out_specs=pl.BlockSpec((tm, N), lambda i: (i, 0)),
        out_shape=jax.ShapeDtypeStruct((T, N), F32),
        compiler_params=_params("parallel"),
    )(a, w)


def _to_row_tiles(ref, value):
    n = value.shape[1] // LANES
    for c in range(n):
        ref[pl.ds(c, value.shape[0], stride=n), :] = value[:, c * LANES:(c + 1) * LANES]


def _from_row_tiles(ref, n_rows, n):
    return jnp.concatenate([ref[pl.ds(c, n_rows, stride=n), :] for c in range(n)], axis=1)


def _bf16_bits(v):
    return lax.bitcast_convert_type(v.astype(BF16).astype(F32), jnp.uint32)


def _to_packed_tiles(ref, value):
    m = value.shape[1] // (2 * LANES)
    for c in range(m):
        lo = _bf16_bits(value[:, c * LANES:(c + 1) * LANES]) >> 16
        hi = _bf16_bits(value[:, (c + m) * LANES:(c + m + 1) * LANES]) & jnp.uint32(0xFFFF0000)
        ref[pl.ds(c, value.shape[0], stride=m), :] = lo | hi


def _from_packed_tiles(ref, n_rows, m):
    words = [ref[pl.ds(c, n_rows, stride=m), :] for c in range(m)]
    as_f32 = lambda w: lax.bitcast_convert_type(w, F32)
    chunks = ([as_f32(w << 16) for w in words]
              + [as_f32(w & jnp.uint32(0xFFFF0000)) for w in words])
    return jnp.concatenate(chunks, axis=1).astype(BF16)


def _proj_ln_kernel(has_gate, *refs):
    if has_gate:
        a_ref, g_ref, w_ref, x_ref, ln_ref, o_ref, o_tiles_ref = refs
        a = a_ref[...] * g_ref[...]
    else:
        a_ref, w_ref, x_ref, ln_ref, o_ref, o_tiles_ref = refs
        a = a_ref[...]
    z = DEEPNORM_ALPHA * x_ref[...] + _dot(a, w_ref[...])
    out = _layer_norm(z, ln_ref[0:1, :], ln_ref[1:2, :])
    o_ref[...] = out
    _to_packed_tiles(o_tiles_ref, out)


def _proj_ln(a, gate, w, x, ln):
    T, D = x.shape
    tm = _row_tile(T, 512)
    n = D // (2 * LANES)
    tile = pl.BlockSpec((tm, D), lambda i: (i, 0))
    full = lambda t: pl.BlockSpec(t.shape, lambda i: (0, 0))
    has_gate = gate is not None
    ins = [a] + ([gate] if has_gate else []) + [w, x, ln]
    specs = [tile] + ([tile] if has_gate else []) + [full(w), tile, full(ln)]
    return pl.pallas_call(
        functools.partial(_proj_ln_kernel, has_gate),
        grid=(T // tm,),
        in_specs=specs,
        out_specs=[tile, pl.BlockSpec((tm * n, LANES), lambda i: (i, 0))],
        out_shape=[jax.ShapeDtypeStruct((T, D), F32), jax.ShapeDtypeStruct((T * n, LANES), jnp.uint32)],
        compiler_params=_params("parallel"),
    )(*ins)


def _max_rows(s, groups=4):
    step = s.shape[0] // groups
    part = s[0:step]
    for g in range(1, groups):
        part = jnp.maximum(part, s[g * step:(g + 1) * step])
    return jnp.max(part, axis=0, keepdims=True)


def _attn_kernel(out_scale, q_ref, k_ref, v_ref, slope_ref, lam_ref, g_ref, o_ref,
                 k_s, vt_s, acc_s):
    i = pl.program_id(2)
    tq = q_ref.shape[1]
    tk = tq
    n_heads = q_ref.shape[2] // LANES
    n_kv_blocks = k_ref.shape[1] // tk
    heads = range(n_heads)
    chains = [(h, c) for h in heads for c in range(2)]
    lanes_of = lambda h: slice(h * LANES, (h + 1) * LANES)

    @pl.when(i == 0)
    def _():
        lane = lax.broadcasted_iota(jnp.int32, (tk, LANES), 1)
        for h in heads:
            for blk in range(n_kv_blocks):
                sl = slice(blk * tk, (blk + 1) * tk)
                kb = k_ref[0, sl, lanes_of(h)]
                k_s[2 * h, sl, :] = jnp.where(lane < DA_HEAD_DIM, kb, 0.0).astype(BF16)
                k_s[2 * h + 1, sl, :] = jnp.where(lane < DA_HEAD_DIM, 0.0, kb).astype(BF16)
                vt_s[h, blk, 0:LANES, :] = v_ref[0, sl, lanes_of(h)].T.astype(BF16)
                vt_s[h, blk, LANES:, :] = jnp.ones((ATTN_ONES_ROWS, tk), BF16)

    q = [(q_ref[0, :, lanes_of(h)] * (DA_HEAD_DIM ** -0.5)).astype(BF16) for h in heads]
    slope = [slope_ref[0, :, h * LANES:h * LANES + 1] for h in heads]
    k_loc = lax.broadcasted_iota(jnp.int32, (tk, tq), 0)
    q_loc = lax.broadcasted_iota(jnp.int32, (tk, tq), 1)
    k_loc_f = k_loc.astype(F32)
    acc_s[...] = jnp.zeros_like(acc_s)

    def block(j, ms, diagonal):
        sl = pl.ds(pl.multiple_of(j * tk, tk), tk)
        rel = ((j - i) * tk).astype(F32)
        each = lambda f: [f(n, h) for n, (h, _) in enumerate(chains)]
        s = each(lambda n, h: _dot_nt(k_s[n, sl, :], q[h]) + slope[h] * k_loc_f)
        if diagonal:
            s = each(lambda n, h: jnp.where(k_loc > q_loc, NEG_BIG, s[n]))
        offset = [slope[h] * rel for h in heads]
        m_new = each(lambda n, h: jnp.maximum(ms[n], _max_rows(s[n]) + offset[h]))
        p = each(lambda n, h: jnp.exp(s[n] - (m_new[n] - offset[h])).astype(BF16))
        acc_old = each(lambda n, h: acc_s[n])
        pv = each(lambda n, h: jnp.dot(vt_s[h, j], p[n], preferred_element_type=F32))
        for n in range(len(chains)):
            acc_s[n] = jnp.exp(ms[n] - m_new[n]) * acc_old[n] + pv[n]
        return tuple(m_new)

    init = tuple(jnp.full((1, tq), NEG_BIG, F32) for _ in chains)
    ms = lax.fori_loop(0, i, lambda j, ms: block(j, ms, False), init)
    block(i, ms, True)

    for h in heads:
        acc0, acc1 = acc_s[2 * h], acc_s[2 * h + 1]
        o_t = (acc0[:LANES] / acc0[LANES:LANES + 1]
               - lam_ref[:, 0:1] * (acc1[:LANES] / acc1[LANES:LANES + 1]))
        o = o_t.T
        o = o * lax.rsqrt(jnp.mean(o * o, axis=-1, keepdims=True) + SUBLN_EPS) * g_ref[...]
        o_ref[0, :, lanes_of(h)] = o * out_scale


def _diff_attention(q, kv, slopes, lam_row, subln_g, lambda_init):
    B, S, D = q.shape
    tq = _row_tile(S, 256)
    nh = ATTN_HEADS_PER_STEP
    width = nh * LANES
    n_groups = DA_HEADS // nh
    return pl.pallas_call(
        functools.partial(_attn_kernel, 1.0 - lambda_init),
        grid=(B, n_groups, S // tq),
        in_specs=[pl.BlockSpec((1, tq, width), lambda b, h, i: (b, i, h)),
                  pl.BlockSpec((1, S, width), lambda b, h, i: (b, 0, h)),
                  pl.BlockSpec((1, S, width), lambda b, h, i: (b, 0, n_groups + h)),
                  pl.BlockSpec((1, 1, width), lambda b, h, i: (h, 0, 0)),
                  pl.BlockSpec((1, LANES), lambda b, h, i: (0, 0)),
                  pl.BlockSpec((1, LANES), lambda b, h, i: (0, 0))],
        out_specs=pl.BlockSpec((1, tq, width), lambda b, h, i: (b, i, h)),
        out_shape=jax.ShapeDtypeStruct((B, S, D), F32),
        scratch_shapes=[pltpu.VMEM((2 * nh, S, LANES), BF16),
                        pltpu.VMEM((nh, S // tq, LANES + ATTN_ONES_ROWS, tq), BF16),
                        pltpu.VMEM((2 * nh, LANES + ATTN_ONES_ROWS, tq), F32)],
        compiler_params=_params("parallel", "parallel", "arbitrary"),
    )(q, kv, kv, slopes.reshape(n_groups, 1, width), lam_row, subln_g)


def _router_kernel(x_ref, w_ref, b_ref, idx_out, gate_out, count_out):
    @pl.when(pl.program_id(0) == 0)
    def _():
        count_out[...] = jnp.zeros_like(count_out)

    x = x_ref[...]
    xh, xl = _split2(x)
    w = w_ref[...]
    hh = jnp.dot(xh, w, preferred_element_type=F32)
    lh = jnp.dot(xl, w[:, :LANES], preferred_element_type=F32)
    logits = hh[:, :LANES] + (hh[:, LANES:] + lh) + b_ref[...]
    lane = lax.broadcasted_iota(jnp.int32, logits.shape, 1)
    vals, idxs = [], []
    cur = logits
    for _ in range(TOP_K):
        m = jnp.max(cur, axis=-1, keepdims=True)
        idx = jnp.min(jnp.where(cur == m, lane, LANES), axis=-1, keepdims=True)
        cur = jnp.where(lane == idx, -jnp.inf, cur)
        vals.append(m)
        idxs.append(idx)
    es = [jnp.exp(v - vals[0]) for v in vals]
    denom = es[0] + es[1] + es[2] + es[3]
    idx_row = jnp.zeros(logits.shape, jnp.int32)
    gate_row = jnp.zeros(logits.shape, F32)
    for kx in range(TOP_K):
        idx_row = jnp.where(lane == kx, idxs[kx], idx_row)
        gate_row = jnp.where(lane == kx, es[kx] / denom, gate_row)
    idx_out[...] = idx_row
    gate_out[...] = gate_row
    picked = sum((lane == idx).astype(jnp.int32) for idx in idxs)
    count_out[0:1, :] += jnp.sum(picked, axis=0, keepdims=True)


def _router(x, w3, b_row):
    T, D = x.shape
    tm = _row_tile(T, 512)
    tile = pl.BlockSpec((tm, D), lambda i: (i, 0))
    out_tile = pl.BlockSpec((tm, LANES), lambda i: (i, 0))
    idx_rows, gate_rows, counts = pl.pallas_call(
        _router_kernel,
        grid=(T // tm,),
        in_specs=[tile, pl.BlockSpec(w3.shape, lambda i: (0, 0)),
                  pl.BlockSpec(b_row.shape, lambda i: (0, 0))],
        out_specs=[out_tile, out_tile, pl.BlockSpec((SUBLANES, LANES), lambda i: (0, 0))],
        out_shape=[jax.ShapeDtypeStruct((T, LANES), jnp.int32),
                   jax.ShapeDtypeStruct((T, LANES), F32),
                   jax.ShapeDtypeStruct((SUBLANES, LANES), jnp.int32)],
        compiler_params=_params("arbitrary"),
    )(x, w3, b_row)
    return idx_rows, gate_rows, counts[0, :N_EXPERTS]


def _clamped_swiglu(gate, up):
    gate = jnp.minimum(gate, SWIGLU_LIMIT)
    up = jnp.clip(up, -SWIGLU_LIMIT, SWIGLU_LIMIT)
    return (up + 1.0) * (gate * jax.nn.sigmoid(SWIGLU_ALPHA * gate))


def _moe_kernel(bexp_ref, glast_ref, g_first, g_b, g_next, s_prev, s_a, s_b, x_hbm,
                wgu_a, bgu_a, wdn_a, bdn_a, wgu_b, bgu_b, wdn_b, bdn_b, y_hbm,
                xbuf_a, xbuf_b, ybuf_a, ybuf_b, zbuf, sem_ga, sem_gb, sem_sa, sem_sb, sem_z):
    del bexp_ref
    g = pl.program_id(0)
    last = glast_ref[0]

    n = wdn_a.shape[2] // LANES
    m = wgu_a.shape[1] // (2 * LANES)

    def gather(idx_ref, xbuf, sem, j):
        return pltpu.make_async_copy(x_hbm.at[pl.ds(pl.multiple_of(idx_ref[0, 0, j], m), m)],
                                     xbuf.at[pl.ds(j * m, m)], sem)

    def scatter(idx_ref, ybuf, sem, j):
        return pltpu.make_async_copy(ybuf.at[pl.ds(j * n, n)],
                                     y_hbm.at[pl.ds(pl.multiple_of(idx_ref[0, 0, j], n), n)], sem)

    def wait_gather(xbuf, sem):
        pltpu.make_async_copy(x_hbm.at[pl.ds(0, EXPERT_BLOCK * m)], xbuf, sem).wait()

    def wait_scatter(ybuf, sem):
        pltpu.make_async_copy(ybuf, y_hbm.at[pl.ds(0, EXPERT_BLOCK * n)], sem).wait()

    def expert_phase(xbuf, ybuf, wgu, bgu, wdn, bdn, sem_x, sem_y, gather_next, scatter_done):
        ff = wdn.shape[1]
        for j in range(EXPERT_BLOCK):
            gather_next(j).start(priority=j % DMA_PRIORITIES)
        wait_gather(xbuf, sem_x)
        h = _dot(_from_packed_tiles(xbuf, EXPERT_BLOCK, m), wgu[0]) + bgu[0]
        act = _clamped_swiglu(h[:, :ff], h[:, ff:])
        for j in range(EXPERT_BLOCK):
            scatter_done(j).start(priority=j % DMA_PRIORITIES)
        wait_scatter(ybuf, sem_y)
        _to_row_tiles(ybuf, _dot(act, wdn[0]) + bdn[0])

    @pl.when(g == 0)
    def _():
        ybuf_a[...] = jnp.zeros_like(ybuf_a)
        ybuf_b[...] = jnp.zeros_like(ybuf_b)
        zbuf[...] = jnp.zeros_like(zbuf)
        spare0 = y_hbm.shape[0] - EXPERT_BLOCK * n

        def first(j, c):
            gather(g_first, xbuf_a, sem_ga, j).start()
            pltpu.make_async_copy(ybuf_a.at[pl.ds(j * n, n)],
                                  y_hbm.at[pl.ds(pl.multiple_of(spare0 + j * n, n), n)], sem_sa).start()
            return c
        lax.fori_loop(0, EXPERT_BLOCK, first, 0)

    @pl.when(g <= last)
    def _():
        expert_phase(xbuf_a, ybuf_a, wgu_a, bgu_a, wdn_a, bdn_a, sem_ga, sem_sa,
                     lambda j: gather(g_b, xbuf_b, sem_gb, j),
                     lambda j: scatter(s_prev, ybuf_b, sem_sb, j))
        expert_phase(xbuf_b, ybuf_b, wgu_b, bgu_b, wdn_b, bdn_b, sem_gb, sem_sb,
                     lambda j: gather(g_next, xbuf_a, sem_ga, j),
                     lambda j: scatter(s_a, ybuf_a, sem_sa, j))

    @pl.when(g > last)
    def _():
        fills = [pltpu.make_async_copy(zbuf, y_hbm.at[pl.ds(pl.multiple_of(plan[0, 0, 0], n),
                                                           EXPERT_BLOCK * n)], sem_z)
                 for plan in (s_a, s_b)]
        for f in fills:
            f.start()
        for f in fills:
            f.wait()

    @pl.when(g == last)
    def _():
        def final(j, c):
            scatter(s_b, ybuf_b, sem_sb, j).start()
            return c
        lax.fori_loop(0, EXPERT_BLOCK, final, 0)
        wait_gather(xbuf_a, sem_ga)
        wait_scatter(ybuf_a, sem_sa)
        wait_scatter(ybuf_b, sem_sb)


def _moe_experts(x_tiles, gather_tok, scatter_row, block_exp, last_step, n_out_rows, w_gu, b_gu, w_dn, b_dn):
    D = w_gu.shape[1]
    n = D // LANES
    n_blocks = block_exp.shape[0]
    assert n_blocks % 2 == 0
    FF2 = w_gu.shape[2]
    idx_spec = lambda fn: pl.BlockSpec((1, 1, EXPERT_BLOCK), fn, memory_space=pltpu.SMEM)
    weights = lambda off: [
        pl.BlockSpec((1, D, FF2), lambda g, be, gl: (be[2 * g + off], 0, 0)),
        pl.BlockSpec((1, 1, FF2), lambda g, be, gl: (be[2 * g + off], 0, 0)),
        pl.BlockSpec((1, EXPERT_FF, D), lambda g, be, gl: (be[2 * g + off], 0, 0)),
        pl.BlockSpec((1, 1, D), lambda g, be, gl: (be[2 * g + off], 0, 0)),
    ]
    grid_spec = pltpu.PrefetchScalarGridSpec(
        num_scalar_prefetch=2,
        grid=(n_blocks // 2,),
        in_specs=[
            idx_spec(lambda g, be, gl: (0, 0, 0)),
            idx_spec(lambda g, be, gl: (2 * g + 1, 0, 0)),
            idx_spec(lambda g, be, gl: (2 * g + 2, 0, 0)),
            idx_spec(lambda g, be, gl: (2 * g, 0, 0)),
            idx_spec(lambda g, be, gl: (2 * g + 1, 0, 0)),
            idx_spec(lambda g, be, gl: (2 * g + 2, 0, 0)),
            pl.BlockSpec(memory_space=pl.ANY),
        ] + weights(0) + weights(1),
        out_specs=pl.BlockSpec(memory_space=pl.ANY),
        scratch_shapes=[pltpu.VMEM((EXPERT_BLOCK * n // 2, LANES), jnp.uint32)] * 2
        + [pltpu.VMEM((EXPERT_BLOCK * n, LANES), F32)] * 3
        + [pltpu.SemaphoreType.DMA(())] * 5,
    )
    b_gu = b_gu.reshape(N_EXPERTS, 1, FF2)
    b_dn = b_dn.reshape(N_EXPERTS, 1, D)
    return pl.pallas_call(
        _moe_kernel,
        grid_spec=grid_spec,
        out_shape=jax.ShapeDtypeStruct((n_out_rows * n, LANES), F32),
        compiler_params=_params("arbitrary"),
    )(block_exp, last_step, gather_tok, gather_tok, gather_tok, scatter_row, scatter_row, scatter_row, x_tiles,
      w_gu, b_gu, w_dn, b_dn, w_gu, b_gu, w_dn, b_dn)


def _combine_ln_kernel(y0_ref, y1_ref, y2_ref, y3_ref, gate_ref, x_ref, ln_ref, o_ref):
    gate = gate_ref[...]
    tm, D = x_ref.shape
    ffn = jnp.zeros(x_ref.shape, F32)
    for kx, y_ref in enumerate((y0_ref, y1_ref, y2_ref, y3_ref)):
        ffn = ffn + _from_row_tiles(y_ref, tm, D // LANES) * gate[:, kx:kx + 1]
    z = DEEPNORM_ALPHA * x_ref[...] + ffn
    o_ref[...] = _layer_norm(z, ln_ref[0:1, :], ln_ref[1:2, :])


def _combine_ln(y_rows, gate_rows, x, ln):
    T, D = x.shape
    tm = _row_tile(T, 512)
    tile = pl.BlockSpec((tm, D), lambda i: (i, 0))
    slot = lambda kx: pl.BlockSpec((tm * (D // LANES), LANES), lambda i: (kx * (T // tm) + i, 0))
    return pl.pallas_call(
        _combine_ln_kernel,
        grid=(T // tm,),
        in_specs=[slot(kx) for kx in range(TOP_K)]
        + [pl.BlockSpec((tm, LANES), lambda i: (i, 0)), tile, pl.BlockSpec(ln.shape, lambda i: (0, 0))],
        out_specs=tile,
        out_shape=jax.ShapeDtypeStruct((T, D), F32),
        compiler_params=_params("parallel"),
    )(y_rows, y_rows, y_rows, y_rows, gate_rows, x, ln)


def _routing_plan(top_idx, counts):
    T = top_idx.shape[0]
    TK = T * TOP_K
    order = jnp.argsort(top_idx.reshape(TK)).astype(jnp.int32)
    padded = (counts + EXPERT_BLOCK - 1) // EXPERT_BLOCK * EXPERT_BLOCK
    pad_end = jnp.cumsum(padded)
    pad_start = pad_end - padded
    grp_start = jnp.cumsum(counts) - counts
    spare_start = TK + jnp.cumsum(padded - counts) - (padded - counts)
    n_blocks = -(-TK // EXPERT_BLOCK) + N_EXPERTS
    n_rows = n_blocks * EXPERT_BLOCK
    block_start = jnp.arange(n_blocks, dtype=jnp.int32) * EXPERT_BLOCK
    block_exp = jnp.minimum(jnp.sum(pad_end[None, :] <= block_start[:, None], axis=1),
                            N_EXPERTS - 1).astype(jnp.int32)
    off = (block_start - pad_start[block_exp])[:, None] + jnp.arange(EXPERT_BLOCK, dtype=jnp.int32)[None, :]
    cnt = counts[block_exp][:, None]
    valid = off < cnt
    pair = order[jnp.clip(grp_start[block_exp][:, None] + off, 0, TK - 1)]
    tok, slot = pair // TOP_K, pair % TOP_K
    spare = spare_start[block_exp][:, None] + (off - cnt)
    scale = D_MODEL // LANES
    gather_tok = jnp.where(valid, tok, 0).astype(jnp.int32) * (scale // 2)
    scatter_row = jnp.where(valid, slot * T + tok, spare).astype(jnp.int32) * scale
    fill = jnp.arange(EXPERT_BLOCK, dtype=jnp.int32)[None, :]
    gather_tok = jnp.concatenate([gather_tok, jnp.zeros_like(fill)])
    scatter_row = jnp.concatenate([(n_rows + fill) * scale, scatter_row])
    shape = (n_blocks + 1, 1, EXPERT_BLOCK)
    n_used = jnp.maximum(pad_end[-1] // EXPERT_BLOCK, 1)
    last_step = ((n_used - 1) // 2).astype(jnp.int32).reshape(1)
    return (gather_tok.reshape(shape), scatter_row.reshape(shape), block_exp, last_step,
            n_rows + 2 * EXPERT_BLOCK)


def _moe_ffn_ln(x, x_tiles, router_w, router_b, w_gu, b_gu, w_dn, b_dn, ln):
    T, D = x.shape
    w_pad = jnp.zeros((D, LANES), F32).at[:, :N_EXPERTS].set(router_w)
    w3 = jnp.concatenate(_split2(w_pad), axis=1)
    b_row = jnp.full((1, LANES), NEG_BIG, F32).at[0, :N_EXPERTS].set(router_b)
    idx_rows, gate_rows, counts = _router(x, w3, b_row)
    gather_tok, scatter_row, block_exp, last_step, n_out_rows = _routing_plan(idx_rows[:, :TOP_K], counts)
    y_rows = _moe_experts(x_tiles, gather_tok, scatter_row, block_exp, last_step, n_out_rows,
                          w_gu, b_gu, w_dn, b_dn)
    return _combine_ln(y_rows, gate_rows, x, ln)


def _cast_kernel(w_ref, o_ref):
    o_ref[...] = w_ref[...].astype(BF16)


def _cast_bf16(w):
    lead = w.shape[:-2]
    w3 = w.reshape((-1,) + w.shape[-2:])
    spec = pl.BlockSpec((1,) + w3.shape[1:], lambda i: (i, 0, 0))
    out = pl.pallas_call(
        _cast_kernel,
        grid=(w3.shape[0],),
        in_specs=[spec],
        out_specs=spec,
        out_shape=jax.ShapeDtypeStruct(w3.shape, BF16),
        compiler_params=_params("parallel"),
    )(w3)
    return out.reshape(lead + w.shape[-2:])


def _pad_rows(rows, n=SUBLANES):
    out = jnp.zeros((n, rows[0].shape[-1]), F32)
    for j, r in enumerate(rows):
        out = out.at[j].set(r.reshape(-1))
    return out


def kernel(x, ln_g, ln_b, rwkv_mix, rwkv_w_rkv, rwkv_w_o, rwkv_w0, rwkv_w1, rwkv_w2, rwkv_a0, rwkv_a1, rwkv_a2, rwkv_g1, rwkv_g2, rwkv_k_k, rwkv_k_a, rwkv_r_k, rwkv_lnx_g, rwkv_lnx_b, rwkv_v0, rwkv_v1, rwkv_v2, kv_w, da_w_q, da_w_o, da_lambda, da_subln_g, moe_router_w, moe_router_b, moe_w_gu, moe_b_gu, moe_w_dn, moe_b_dn):
    B, S, D = x.shape
    T = B * S
    v_first = None
    kv = None
    slopes = 2.0 ** (-8.0 * jnp.arange(1, DA_HEADS + 1, dtype=F32) / DA_HEADS)
    slopes = jnp.repeat(slopes, LANES).reshape(1, DA_HEADS * LANES)
    w_gu_bf16 = _cast_bf16(moe_w_gu)
    w_dn_bf16 = _cast_bf16(moe_w_dn)
    for l in range(DEPTH):
        ln_mix = jnp.stack([ln_g[l, 0], ln_b[l, 0]])
        ln_ffn = jnp.stack([ln_g[l, 1], ln_b[l, 1]])
        xt = x.reshape(T, D)
        if l < N_A_LAYERS:
            zero = jnp.zeros((D,), F32)
            vecs = _pad_rows([rwkv_w0[l], rwkv_a0[l], rwkv_v0[l - 1] if l > 0 else zero])
            vres = None if l == 0 else (rwkv_v1[l - 1].astype(BF16), rwkv_v2[l - 1].astype(BF16), v_first)
            r, lw, k, v, a, g = _rwkv_proj(
                x, _pad_rows(list(rwkv_mix[l])), rwkv_w_rkv[l].astype(BF16),
                rwkv_w1[l].astype(BF16), rwkv_w2[l].astype(BF16), rwkv_a1[l].astype(BF16),
                rwkv_a2[l].astype(BF16), rwkv_g1[l].astype(BF16), rwkv_g2[l].astype(BF16), vecs, vres)
            if l == 0:
                v_first = v
            head_params = _pad_rows([rwkv_k_k[l], rwkv_k_a[l],
                                     jnp.tile(rwkv_r_k[l].reshape(-1), 1), rwkv_lnx_g[l], rwkv_lnx_b[l]])
            y = _wkv(r, lw, k, v, a, head_params)
            xt, xt_tiles = _proj_ln(y.reshape(T, D), g.reshape(T, D), rwkv_w_o[l].astype(BF16), xt, ln_mix)
        else:
            if l == N_A_LAYERS:
                kv = _matmul(xt, kv_w.astype(BF16)).reshape(B, S, 2 * D)
            j = l - N_A_LAYERS
            lambda_init = 0.8 - 0.6 * math.exp(-0.3 * l)
            lam = da_lambda[j].astype(F32)
            lam_full = (jnp.exp(jnp.sum(lam[0] * lam[1])) - jnp.exp(jnp.sum(lam[2] * lam[3]))
                        + lambda_init)
            q = _matmul(xt, da_w_q[j].astype(BF16)).reshape(B, S, D)
            o = _diff_attention(q, kv, slopes, jnp.full((1, LANES), lam_full, F32),
                                da_subln_g[j].reshape(1, LANES), lambda_init)
            xt, xt_tiles = _proj_ln(o.reshape(T, D), None, da_w_o[j].astype(BF16), xt, ln_mix)
        xt = _moe_ffn_ln(xt, xt_tiles, moe_router_w[l], moe_router_b[l], w_gu_bf16[l], moe_b_gu[l],
                         w_dn_bf16[l], moe_b_dn[l], ln_ffn)
        x = xt.reshape(B, S, D)
    return x
```

```python
import functools
import math

import jax
import jax.numpy as jnp
from jax import lax
from jax.experimental import pallas as pl
from jax.experimental.pallas import tpu as pltpu

F32 = jnp.float32
BF16 = jnp.bfloat16

D_MODEL = 1024
DEPTH = 4
N_A_LAYERS = DEPTH // 2
RWKV_HEAD_DIM = 64
RWKV_GN_EPS = 64e-5
L2_EPS = 1e-12
DA_HEAD_DIM = 64
DA_HEADS = D_MODEL // (2 * DA_HEAD_DIM)
SUBLN_EPS = 1e-5
N_EXPERTS = 32
TOP_K = 4
EXPERT_FF = D_MODEL
SWIGLU_LIMIT = 7.0
SWIGLU_ALPHA = 1.702
EXPERT_BLOCK = 512
DEEPNORM_ALPHA = (2 * DEPTH) ** 0.25
LN_EPS = 1e-5

LANES = 128
SUBLANES = 8
VMEM_LIMIT_BYTES = 56 * 1024 * 1024
DMA_PRIORITIES = 2

WKV_CHUNK = 64
WKV_TILES_PER_STEP = 8
HEADS_PER_TILE = LANES // RWKV_HEAD_DIM
NEG_BIG = -1e30
ATTN_ONES_ROWS = 16
ATTN_HEADS_PER_STEP = 4


def _params(*sem):
    return pltpu.CompilerParams(dimension_semantics=sem, vmem_limit_bytes=VMEM_LIMIT_BYTES)


def _dot(a, b):
    return jnp.dot(a.astype(BF16), b.astype(BF16), preferred_element_type=F32)


def _dot_nt(a, b):
    return lax.dot_general(a.astype(BF16), b.astype(BF16), (((1,), (1,)), ((), ())),
                           preferred_element_type=F32)


def _split2(x):
    hi = x.astype(BF16)
    return hi, (x - hi.astype(F32)).astype(BF16)


def _dot_exact_lhs(m_bf16, x):
    hi, lo = _split2(x)
    return (jnp.dot(m_bf16, hi, preferred_element_type=F32)
            + jnp.dot(m_bf16, lo, preferred_element_type=F32))


def _layer_norm(z, g, b):
    mu = jnp.mean(z, axis=-1, keepdims=True)
    d = z - mu
    var = jnp.mean(d * d, axis=-1, keepdims=True)
    return d * lax.rsqrt(var + LN_EPS) * g + b


def _row_tile(n, want):
    t = min(want, n)
    assert n % t == 0 and t % SUBLANES == 0
    return t


def _rwkv_proj_kernel(has_vres, *refs):
    if has_vres:
        (x_ref, xp_ref, mix_ref, wrkv_ref, w1_ref, w2_ref, a1_ref, a2_ref, g1_ref, g2_ref,
         vec_ref, v1_ref, v2_ref, vfirst_ref,
         r_out, lw_out, k_out, v_out, a_out, g_out) = refs
    else:
        (x_ref, xp_ref, mix_ref, wrkv_ref, w1_ref, w2_ref, a1_ref, a2_ref, g1_ref, g2_ref,
         vec_ref, r_out, lw_out, k_out, v_out, a_out, g_out) = refs
    i = pl.program_id(1)
    x = x_ref[0]
    prev = xp_ref[0][SUBLANES - 1:SUBLANES, :]
    prev = jnp.where(i == 0, 0.0, prev)
    row = lax.broadcasted_iota(jnp.int32, x.shape, 0)
    x_prev = jnp.where(row == 0, prev, pltpu.roll(x, 1, axis=0))
    xx = x_prev - x

    def xm(j):
        return x + xx * mix_ref[j:j + 1, :]

    xv = xm(2)
    r_out[0] = _dot(xm(0), wrkv_ref[0])
    k_out[0] = _dot(xm(1), wrkv_ref[1])
    v = _dot(xv, wrkv_ref[2])
    wl = vec_ref[0:1, :] + _dot(jnp.tanh(_dot(xm(3), w1_ref[...])), w2_ref[...])
    lw_out[0] = -jax.nn.sigmoid(wl) * math.exp(-0.5)
    a_out[0] = jax.nn.sigmoid(vec_ref[1:2, :] + _dot(_dot(xm(4), a1_ref[...]), a2_ref[...]))
    g_out[0] = _dot(jax.nn.sigmoid(_dot(xm(5), g1_ref[...])), g2_ref[...])
    if has_vres:
        mixv = jax.nn.sigmoid(vec_ref[2:3, :] + _dot(_dot(xv, v1_ref[...]), v2_ref[...]))
        v = v + (vfirst_ref[0] - v) * mixv
    v_out[0] = v


def _rwkv_proj(x, mix, w_rkv, w1, w2, a1, a2, g1, g2, vecs, vres):
    B, S, D = x.shape
    tm = _row_tile(S, 512)
    has_vres = vres is not None
    full = lambda a: pl.BlockSpec(a.shape, lambda b, i: (0,) * a.ndim)
    tile = pl.BlockSpec((1, tm, D), lambda b, i: (b, i, 0))
    prev = pl.BlockSpec((1, SUBLANES, D),
                        lambda b, i: (b, jnp.maximum(i * (tm // SUBLANES) - 1, 0), 0))
    ins = [x, x, mix, w_rkv, w1, w2, a1, a2, g1, g2, vecs]
    specs = [tile, prev] + [full(a) for a in ins[2:]]
    if has_vres:
        v1, v2, v_first = vres
        ins += [v1, v2, v_first]
        specs += [full(v1), full(v2), tile]
    out = jax.ShapeDtypeStruct((B, S, D), F32)
    return pl.pallas_call(
        functools.partial(_rwkv_proj_kernel, has_vres),
        grid=(B, S // tm),
        in_specs=specs,
        out_specs=[tile] * 6,
        out_shape=[out] * 6,
        compiler_params=_params("parallel", "arbitrary"),
    )(*ins)


def _wkv_kernel(r_ref, lw_ref, k_ref, v_ref, a_ref, hp_ref, y_out,
                state, avec_s, bvec_s, kmod_s, y_s):
    C = WKV_CHUNK
    ts = r_ref.shape[1]
    n_tiles = r_ref.shape[2] // LANES
    n_chunks = ts // C
    R2 = HEADS_PER_TILE * C
    inv_n = 1.0 / RWKV_HEAD_DIM

    @pl.when(pl.program_id(2) == 0)
    def _():
        state[...] = jnp.zeros_like(state)

    li = lax.broadcasted_iota(jnp.int32, (LANES, LANES), 0)
    lj = lax.broadcasted_iota(jnp.int32, (LANES, LANES), 1)
    head_ones = (li // RWKV_HEAD_DIM == lj // RWKV_HEAD_DIM).astype(BF16)

    for t in range(n_tiles):
        ls = slice(t * LANES, (t + 1) * LANES)
        k = k_ref[0, :, ls]
        a = a_ref[0, :, ls]
        kk = k * hp_ref[0:1, ls]
        n2 = _dot(kk * kk, head_ones)
        kk = kk / jnp.maximum(jnp.sqrt(n2), L2_EPS)
        avec_s[:, ls] = -kk
        bvec_s[:, ls] = kk * a
        kmod_s[:, ls] = k * (1.0 + (a - 1.0) * hp_ref[1:2, ls])

    ri = lax.broadcasted_iota(jnp.int32, (R2, R2), 0)
    rj = lax.broadcasted_iota(jnp.int32, (R2, R2), 1)
    same_head = (ri // C) == (rj // C)
    strict = same_head & ((ri % C) > (rj % C))
    incl = same_head & ((ri % C) >= (rj % C))
    ci = lax.broadcasted_iota(jnp.int32, (C, C), 0)
    cj = lax.broadcasted_iota(jnp.int32, (C, C), 1)
    tri = (ci >= cj).astype(BF16)
    lane = lax.broadcasted_iota(jnp.int32, (C, LANES), 1)
    head0 = lane < RWKV_HEAD_DIM
    n_doublings = int(math.log2(C))

    def stack(t):
        return jnp.concatenate([jnp.where(head0, t, 0.0), jnp.where(head0, 0.0, t)], axis=0)

    tiles = range(n_tiles)
    lanes_of = lambda t: slice(t * LANES, (t + 1) * LANES)

    def chunk(c, carry):
        sl = pl.ds(pl.multiple_of(c * C, C), C)
        each = lambda f: [f(t) for t in tiles]
        s0 = each(lambda t: state[t])
        lw = each(lambda t: lw_ref[0, sl, lanes_of(t)])
        cum = each(lambda t: _dot_exact_lhs(tri, lw[t]))
        tot = each(lambda t: cum[t][C - 1:C, :])
        e_neg = each(lambda t: jnp.exp(-cum[t]))
        av = each(lambda t: avec_s[sl, lanes_of(t)])
        bv = each(lambda t: bvec_s[sl, lanes_of(t)])
        km = each(lambda t: kmod_s[sl, lanes_of(t)])
        lhs = each(lambda t: jnp.concatenate(
            [stack(av[t] * jnp.exp(cum[t] - lw[t])),
             stack(r_ref[0, sl, lanes_of(t)] * jnp.exp(cum[t]))], axis=0))
        rhs = each(lambda t: jnp.concatenate([stack(bv[t] * e_neg[t]), stack(km[t] * e_neg[t])], axis=0))
        g = each(lambda t: _dot_nt(lhs[t], rhs[t]))
        a_s = each(lambda t: _dot_nt(lhs[t], s0[t]))
        vm = each(lambda t: stack(v_ref[0, sl, lanes_of(t)]))
        akv = each(lambda t: _dot(jnp.where(strict, g[t][:R2, R2:], 0.0), vm[t]))
        u = each(lambda t: a_s[t][:R2] + akv[t])
        p = each(lambda t: jnp.where(strict, g[t][:R2, :R2], 0.0))
        for it in range(n_doublings):
            pu = each(lambda t: _dot(p[t], u[t]))
            if it + 1 < n_doublings:
                p = each(lambda t: _dot(p[t], p[t]))
            u = each(lambda t: u[t] + pu[t])
        uv = each(lambda t: jnp.concatenate([u[t], vm[t]], axis=0))
        w_y = each(lambda t: jnp.concatenate([jnp.where(incl, g[t][R2:, :R2], 0.0),
                                              jnp.where(incl, g[t][R2:, R2:], 0.0)], axis=1))
        y_sm = each(lambda t: a_s[t][R2:] + _dot(w_y[t], uv[t]))
        bk_end = each(lambda t: jnp.concatenate(
            [stack(bv[t] * jnp.exp(tot[t] - cum[t])), stack(km[t] * jnp.exp(tot[t] - cum[t]))], axis=0))
        s1 = each(lambda t: s0[t] * jnp.exp(tot[t]) + _dot(uv[t].T, bk_end[t]))
        for t in tiles:
            y_s[sl, lanes_of(t)] = y_sm[t][:C] + y_sm[t][C:]
            state[t] = s1[t]
        return carry

    lax.fori_loop(0, n_chunks, chunk, 0)

    for t in range(n_tiles):
        ls = slice(t * LANES, (t + 1) * LANES)
        y = y_s[:, ls]
        mu = _dot(y, head_ones) * inv_n
        d = y - mu
        var = _dot(d * d, head_ones) * inv_n
        yn = d * lax.rsqrt(var + RWKV_GN_EPS) * hp_ref[3:4, ls] + hp_ref[4:5, ls]
        bonus = _dot(r_ref[0, :, ls] * kmod_s[:, ls] * hp_ref[2:3, ls], head_ones)
        y_out[0, :, ls] = yn + bonus * v_ref[0, :, ls]


def _wkv(r, lw, k, v, a, head_params):
    B, S, D = r.shape
    ts = _row_tile(S, 512)
    assert ts % WKV_CHUNK == 0
    width = WKV_TILES_PER_STEP * LANES
    tile = pl.BlockSpec((1, ts, width), lambda b, p, i: (b, i, p))
    hp = pl.BlockSpec((SUBLANES, width), lambda b, p, i: (0, p))
    return pl.pallas_call(
        _wkv_kernel,
        grid=(B, D // width, S // ts),
        in_specs=[tile] * 5 + [hp],
        out_specs=tile,
        out_shape=jax.ShapeDtypeStruct((B, S, D), F32),
        scratch_shapes=[pltpu.VMEM((WKV_TILES_PER_STEP, LANES, LANES), F32)]
        + [pltpu.VMEM((ts, width), F32)] * 4,
        compiler_params=_params("parallel", "parallel", "arbitrary"),
    )(r, lw, k, v, a, head_params)


def _matmul_kernel(a_ref, w_ref, o_ref):
    o_ref[...] = _dot(a_ref[...], w_ref[...])


def _matmul(a, w):
    T, K = a.shape
    N = w.shape[1]
    tm = _row_tile(T, 512)
    return pl.pallas_call(
        _matmul_kernel,
        grid=(T // tm,),
        in_specs=[pl.BlockSpec((tm, K), lambda i: (i, 0)), pl.BlockSpec((K, N), lambda i: (0, 0))],
        out_specs=pl.BlockSpec((tm, N), lambda i: (i, 0)),
        out_shape=jax.ShapeDtypeStruct((T, N), F32),
        compiler_params=_params("parallel"),
    )(a, w)


def _to_row_tiles(ref, value):
    n = value.shape[1] // LANES
    for c in range(n):
        ref[pl.ds(c, value.shape[0], stride=n), :] = value[:, c * LANES:(c + 1) * LANES]


def _from_row_tiles(ref, n_rows, n):
    return jnp.concatenate([ref[pl.ds(c, n_rows, stride=n), :] for c in range(n)], axis=1)


def _bf16_bits(v):
    return lax.bitcast_convert_type(v.astype(BF16).astype(F32), jnp.uint32)


def _to_packed_tiles(ref, value):
    m = value.shape[1] // (2 * LANES)
    for c in range(m):
        lo = _bf16_bits(value[:, c * LANES:(c + 1) * LANES]) >> 16
        hi = _bf16_bits(value[:, (c + m) * LANES:(c + m + 1) * LANES]) & jnp.uint32(0xFFFF0000)
        ref[pl.ds(c, value.shape[0], stride=m), :] = lo | hi


def _from_packed_tiles(ref, n_rows, m):
    words = [ref[pl.ds(c, n_rows, stride=m), :] for c in range(m)]
    as_f32 = lambda w: lax.bitcast_convert_type(w, F32)
    chunks = ([as_f32(w << 16) for w in words]
              + [as_f32(w & jnp.uint32(0xFFFF0000)) for w in words])
    return jnp.concatenate(chunks, axis=1).astype(BF16)


def _proj_ln_kernel(has_gate, *refs):
    if has_gate:
        a_ref, g_ref, w_ref, x_ref, ln_ref, o_ref, o_tiles_ref = refs
        a = a_ref[...] * g_ref[...]
    else:
        a_ref, w_ref, x_ref, ln_ref, o_ref, o_tiles_ref = refs
        a = a_ref[...]
    z = DEEPNORM_ALPHA * x_ref[...] + _dot(a, w_ref[...])
    out = _layer_norm(z, ln_ref[0:1, :], ln_ref[1:2, :])
    o_ref[...] = out
    _to_packed_tiles(o_tiles_ref, out)


def _proj_ln(a, gate, w, x, ln):
    T, D = x.shape
    tm = _row_tile(T, 512)
    n = D // (2 * LANES)
    tile = pl.BlockSpec((tm, D), lambda i: (i, 0))
    full = lambda t: pl.BlockSpec(t.shape, lambda i: (0, 0))
    has_gate = gate is not None
    ins = [a] + ([gate] if has_gate else []) + [w, x, ln]
    specs = [tile] + ([tile] if has_gate else []) + [full(w), tile, full(ln)]
    return pl.pallas_call(
        functools.partial(_proj_ln_kernel, has_gate),
        grid=(T // tm,),
        in_specs=specs,
        out_specs=[tile, pl.BlockSpec((tm * n, LANES), lambda i: (i, 0))],
        out_shape=[jax.ShapeDtypeStruct((T, D), F32), jax.ShapeDtypeStruct((T * n, LANES), jnp.uint32)],
        compiler_params=_params("parallel"),
    )(*ins)


def _max_rows(s, groups=4):
    step = s.shape[0] // groups
    part = s[0:step]
    for g in range(1, groups):
        part = jnp.maximum(part, s[g * step:(g + 1) * step])
    return jnp.max(part, axis=0, keepdims=True)


def _attn_kernel(out_scale, q_ref, k_ref, v_ref, slope_ref, lam_ref, g_ref, o_ref,
                 k_s, vt_s, acc_s):
    i = pl.program_id(2)
    tq = q_ref.shape[1]
    tk = tq
    n_heads = q_ref.shape[2] // LANES
    n_kv_blocks = k_ref.shape[1] // tk
    heads = range(n_heads)
    chains = [(h, c) for h in heads for c in range(2)]
    lanes_of = lambda h: slice(h * LANES, (h + 1) * LANES)

    @pl.when(i == 0)
    def _():
        lane = lax.broadcasted_iota(jnp.int32, (tk, LANES), 1)
        row = lax.broadcasted_iota(jnp.int32, (tk, LANES), 0).astype(F32)
        for h in heads:
            key_bias = slope_ref[0, :, h * LANES:h * LANES + 1] * row
            for blk in range(n_kv_blocks):
                sl = slice(blk * tk, (blk + 1) * tk)
                kb = k_ref[0, sl, lanes_of(h)]
                k_s[2 * h, sl, :] = jnp.where(
                    lane < DA_HEAD_DIM, kb, jnp.where(lane == DA_HEAD_DIM, key_bias, 0.0)).astype(BF16)
                k_s[2 * h + 1, sl, :] = jnp.where(
                    lane >= DA_HEAD_DIM, kb, jnp.where(lane == 0, key_bias, 0.0)).astype(BF16)
                vt_s[h, blk, 0:LANES, :] = v_ref[0, sl, lanes_of(h)].T.astype(BF16)
                vt_s[h, blk, LANES:, :] = jnp.ones((ATTN_ONES_ROWS, tk), BF16)

    q_lane = lax.broadcasted_iota(jnp.int32, (tq, LANES), 1)
    q = []
    for h in heads:
        qs = q_ref[0, :, lanes_of(h)] * (DA_HEAD_DIM ** -0.5)
        q.append(jnp.where(q_lane < DA_HEAD_DIM, qs,
                           jnp.where(q_lane == DA_HEAD_DIM, 1.0, 0.0)).astype(BF16))
        q.append(jnp.where(q_lane >= DA_HEAD_DIM, qs, jnp.where(q_lane == 0, 1.0, 0.0)).astype(BF16))
    slope = [slope_ref[0, :, h * LANES:h * LANES + 1] for h in heads]
    k_loc = lax.broadcasted_iota(jnp.int32, (tk, tq), 0)
    q_loc = lax.broadcasted_iota(jnp.int32, (tk, tq), 1)
    acc_s[...] = jnp.zeros_like(acc_s)

    def block(j, ms, diagonal):
        sl = pl.ds(pl.multiple_of(j * tk, tk), tk)
        rel = ((j - i) * tk).astype(F32)
        each = lambda f: [f(n, h) for n, (h, _) in enumerate(chains)]
        s = each(lambda n, h: _dot_nt(k_s[n, sl, :], q[n]))
        if diagonal:
            s = each(lambda n, h: jnp.where(k_loc > q_loc, NEG_BIG, s[n]))
        offset = [slope[h] * rel for h in heads]
        m_new = each(lambda n, h: jnp.maximum(ms[n], _max_rows(s[n]) + offset[h]))
        p = each(lambda n, h: jnp.exp(s[n] - (m_new[n] - offset[h])).astype(BF16))
        acc_old = each(lambda n, h: acc_s[n])
        pv = each(lambda n, h: jnp.dot(vt_s[h, j], p[n], preferred_element_type=F32))
        for n in range(len(chains)):
            acc_s[n] = jnp.exp(ms[n] - m_new[n]) * acc_old[n] + pv[n]
        return tuple(m_new)

    init = tuple(jnp.full((1, tq), NEG_BIG, F32) for _ in chains)
    ms = lax.fori_loop(0, i, lambda j, ms: block(j, ms, False), init)
    block(i, ms, True)

    for h in heads:
        acc0, acc1 = acc_s[2 * h], acc_s[2 * h + 1]
        o_t = (acc0[:LANES] / acc0[LANES:LANES + 1]
               - lam_ref[:, 0:1] * (acc1[:LANES] / acc1[LANES:LANES + 1]))
        o = o_t.T
        o = o * lax.rsqrt(jnp.mean(o * o, axis=-1, keepdims=True) + SUBLN_EPS) * g_ref[...]
        o_ref[0, :, lanes_of(h)] = o * out_scale


def _diff_attention(q, kv, slopes, lam_row, subln_g, lambda_init):
    B, S, D = q.shape
    tq = _row_tile(S, 256)
    nh = ATTN_HEADS_PER_STEP
    width = nh * LANES
    n_groups = DA_HEADS // nh
    return pl.pallas_call(
        functools.partial(_attn_kernel, 1.0 - lambda_init),
        grid=(B, n_groups, S // tq),
        in_specs=[pl.BlockSpec((1, tq, width), lambda b, h, i: (b, i, h)),
                  pl.BlockSpec((1, S, width), lambda b, h, i: (b, 0, h)),
                  pl.BlockSpec((1, S, width), lambda b, h, i: (b, 0, n_groups + h)),
                  pl.BlockSpec((1, 1, width), lambda b, h, i: (h, 0, 0)),
                  pl.BlockSpec((1, LANES), lambda b, h, i: (0, 0)),
                  pl.BlockSpec((1, LANES), lambda b, h, i: (0, 0))],
        out_specs=pl.BlockSpec((1, tq, width), lambda b, h, i: (b, i, h)),
        out_shape=jax.ShapeDtypeStruct((B, S, D), F32),
        scratch_shapes=[pltpu.VMEM((2 * nh, S, LANES), BF16),
                        pltpu.VMEM((nh, S // tq, LANES + ATTN_ONES_ROWS, tq), BF16),
                        pltpu.VMEM((2 * nh, LANES + ATTN_ONES_ROWS, tq), F32)],
        compiler_params=_params("parallel", "parallel", "arbitrary"),
    )(q, kv, kv, slopes.reshape(n_groups, 1, width), lam_row, subln_g)


def _router_kernel(x_ref, w_ref, b_ref, idx_out, gate_out, count_out):
    @pl.when(pl.program_id(0) == 0)
    def _():
        count_out[...] = jnp.zeros_like(count_out)

    x = x_ref[...]
    xh, xl = _split2(x)
    w = w_ref[...]
    hh = jnp.dot(xh, w, preferred_element_type=F32)
    lh = jnp.dot(xl, w[:, :LANES], preferred_element_type=F32)
    logits = hh[:, :LANES] + (hh[:, LANES:] + lh) + b_ref[...]
    lane = lax.broadcasted_iota(jnp.int32, logits.shape, 1)
    vals, idxs = [], []
    cur = logits
    for _ in range(TOP_K):
        m = jnp.max(cur, axis=-1, keepdims=True)
        idx = jnp.min(jnp.where(cur == m, lane, LANES), axis=-1, keepdims=True)
        cur = jnp.where(lane == idx, -jnp.inf, cur)
        vals.append(m)
        idxs.append(idx)
    es = [jnp.exp(v - vals[0]) for v in vals]
    denom = es[0] + es[1] + es[2] + es[3]
    idx_row = jnp.zeros(logits.shape, jnp.int32)
    gate_row = jnp.zeros(logits.shape, F32)
    for kx in range(TOP_K):
        idx_row = jnp.where(lane == kx, idxs[kx], idx_row)
        gate_row = jnp.where(lane == kx, es[kx] / denom, gate_row)
    idx_out[...] = idx_row
    gate_out[...] = gate_row
    picked = sum((lane == idx).astype(jnp.int32) for idx in idxs)
    count_out[0:1, :] += jnp.sum(picked, axis=0, keepdims=True)


def _router(x, w3, b_row):
    T, D = x.shape
    tm = _row_tile(T, 512)
    tile = pl.BlockSpec((tm, D), lambda i: (i, 0))
    out_tile = pl.BlockSpec((tm, LANES), lambda i: (i, 0))
    idx_rows, gate_rows, counts = pl.pallas_call(
        _router_kernel,
        grid=(T // tm,),
        in_specs=[tile, pl.BlockSpec(w3.shape, lambda i: (0, 0)),
                  pl.BlockSpec(b_row.shape, lambda i: (0, 0))],
        out_specs=[out_tile, out_tile, pl.BlockSpec((SUBLANES, LANES), lambda i: (0, 0))],
        out_shape=[jax.ShapeDtypeStruct((T, LANES), jnp.int32),
                   jax.ShapeDtypeStruct((T, LANES), F32),
                   jax.ShapeDtypeStruct((SUBLANES, LANES), jnp.int32)],
        compiler_params=_params("arbitrary"),
    )(x, w3, b_row)
    return idx_rows, gate_rows, counts[0, :N_EXPERTS]


def _clamped_swiglu(gate, up):
    gate = jnp.minimum(gate, SWIGLU_LIMIT)
    up = jnp.clip(up, -SWIGLU_LIMIT, SWIGLU_LIMIT)
    return (up + 1.0) * (gate * jax.nn.sigmoid(SWIGLU_ALPHA * gate))


def _moe_kernel(bexp_ref, glast_ref, g_first, g_b, g_next, s_prev, s_a, s_b, x_hbm,
                wgu_a, bgu_a, wdn_a, bdn_a, wgu_b, bgu_b, wdn_b, bdn_b, y_hbm,
                xbuf_a, xbuf_b, ybuf_a, ybuf_b, zbuf, sem_ga, sem_gb, sem_sa, sem_sb, sem_z):
    del bexp_ref
    g = pl.program_id(0)
    last = glast_ref[0]

    n = wdn_a.shape[2] // LANES
    m = wgu_a.shape[1] // (2 * LANES)

    def gather(idx_ref, xbuf, sem, j):
        return pltpu.make_async_copy(x_hbm.at[pl.ds(pl.multiple_of(idx_ref[0, 0, j], m), m)],
                                     xbuf.at[pl.ds(j * m, m)], sem)

    def scatter(idx_ref, ybuf, sem, j):
        return pltpu.make_async_copy(ybuf.at[pl.ds(j * n, n)],
                                     y_hbm.at[pl.ds(pl.multiple_of(idx_ref[0, 0, j], n), n)], sem)

    def wait_gather(xbuf, sem):
        pltpu.make_async_copy(x_hbm.at[pl.ds(0, EXPERT_BLOCK * m)], xbuf, sem).wait()

    def wait_scatter(ybuf, sem):
        pltpu.make_async_copy(ybuf, y_hbm.at[pl.ds(0, EXPERT_BLOCK * n)], sem).wait()

    def expert_phase(xbuf, ybuf, wgu, bgu, wdn, bdn, sem_x, sem_y, gather_next, scatter_done):
        ff = wdn.shape[1]
        for j in range(EXPERT_BLOCK):
            gather_next(j).start(priority=j % DMA_PRIORITIES)
        wait_gather(xbuf, sem_x)
        h = _dot(_from_packed_tiles(xbuf, EXPERT_BLOCK, m), wgu[0]) + bgu[0]
        act = _clamped_swiglu(h[:, :ff], h[:, ff:])
        for j in range(EXPERT_BLOCK):
            scatter_done(j).start(priority=j % DMA_PRIORITIES)
        wait_scatter(ybuf, sem_y)
        _to_row_tiles(ybuf, _dot(act, wdn[0]) + bdn[0])

    @pl.when(g == 0)
    def _():
        ybuf_a[...] = jnp.zeros_like(ybuf_a)
        ybuf_b[...] = jnp.zeros_like(ybuf_b)
        zbuf[...] = jnp.zeros_like(zbuf)
        spare0 = y_hbm.shape[0] - EXPERT_BLOCK * n

        def first(j, c):
            gather(g_first, xbuf_a, sem_ga, j).start()
            pltpu.make_async_copy(ybuf_a.at[pl.ds(j * n, n)],
                                  y_hbm.at[pl.ds(pl.multiple_of(spare0 + j * n, n), n)], sem_sa).start()
            return c
        lax.fori_loop(0, EXPERT_BLOCK, first, 0)

    @pl.when(g <= last)
    def _():
        expert_phase(xbuf_a, ybuf_a, wgu_a, bgu_a, wdn_a, bdn_a, sem_ga, sem_sa,
                     lambda j: gather(g_b, xbuf_b, sem_gb, j),
                     lambda j: scatter(s_prev, ybuf_b, sem_sb, j))
        expert_phase(xbuf_b, ybuf_b, wgu_b, bgu_b, wdn_b, bdn_b, sem_gb, sem_sb,
                     lambda j: gather(g_next, xbuf_a, sem_ga, j),
                     lambda j: scatter(s_a, ybuf_a, sem_sa, j))

    @pl.when(g > last)
    def _():
        fills = [pltpu.make_async_copy(zbuf, y_hbm.at[pl.ds(pl.multiple_of(plan[0, 0, 0], n),
                                                           EXPERT_BLOCK * n)], sem_z)
                 for plan in (s_a, s_b)]
        for f in fills:
            f.start()
        for f in fills:
            f.wait()

    @pl.when(g == last)
    def _():
        def final(j, c):
            scatter(s_b, ybuf_b, sem_sb, j).start()
            return c
        lax.fori_loop(0, EXPERT_BLOCK, final, 0)
        wait_gather(xbuf_a, sem_ga)
        wait_scatter(ybuf_a, sem_sa)
        wait_scatter(ybuf_b, sem_sb)


def _moe_experts(x_tiles, gather_tok, scatter_row, block_exp, last_step, n_out_rows, w_gu, b_gu, w_dn, b_dn):
    D = w_gu.shape[1]
    n = D // LANES
    n_blocks = block_exp.shape[0]
    assert n_blocks % 2 == 0
    FF2 = w_gu.shape[2]
    idx_spec = lambda fn: pl.BlockSpec((1, 1, EXPERT_BLOCK), fn, memory_space=pltpu.SMEM)
    weights = lambda off: [
        pl.BlockSpec((1, D, FF2), lambda g, be, gl: (be[2 * g + off], 0, 0)),
        pl.BlockSpec((1, 1, FF2), lambda g, be, gl: (be[2 * g + off], 0, 0)),
        pl.BlockSpec((1, EXPERT_FF, D), lambda g, be, gl: (be[2 * g + off], 0, 0)),
        pl.BlockSpec((1, 1, D), lambda g, be, gl: (be[2 * g + off], 0, 0)),
    ]
    grid_spec = pltpu.PrefetchScalarGridSpec(
        num_scalar_prefetch=2,
        grid=(n_blocks // 2,),
        in_specs=[
            idx_spec(lambda g, be, gl: (0, 0, 0)),
            idx_spec(lambda g, be, gl: (2 * g + 1, 0, 0)),
            idx_spec(lambda g, be, gl: (2 * g + 2, 0, 0)),
            idx_spec(lambda g, be, gl: (2 * g, 0, 0)),
            idx_spec(lambda g, be, gl: (2 * g + 1, 0, 0)),
            idx_spec(lambda g, be, gl: (2 * g + 2, 0, 0)),
            pl.BlockSpec(memory_space=pl.ANY),
        ] + weights(0) + weights(1),
        out_specs=pl.BlockSpec(memory_space=pl.ANY),
        scratch_shapes=[pltpu.VMEM((EXPERT_BLOCK * n // 2, LANES), jnp.uint32)] * 2
        + [pltpu.VMEM((EXPERT_BLOCK * n, LANES), F32)] * 3
        + [pltpu.SemaphoreType.DMA(())] * 5,
    )
    b_gu = b_gu.reshape(N_EXPERTS, 1, FF2)
    b_dn = b_dn.reshape(N_EXPERTS, 1, D)
    return pl.pallas_call(
        _moe_kernel,
        grid_spec=grid_spec,
        out_shape=jax.ShapeDtypeStruct((n_out_rows * n, LANES), F32),
        compiler_params=_params("arbitrary"),
    )(block_exp, last_step, gather_tok, gather_tok, gather_tok, scatter_row, scatter_row, scatter_row, x_tiles,
      w_gu, b_gu, w_dn, b_dn, w_gu, b_gu, w_dn, b_dn)


def _combine_ln_kernel(y0_ref, y1_ref, y2_ref, y3_ref, gate_ref, x_ref, ln_ref, o_ref):
    gate = gate_ref[...]
    tm, D = x_ref.shape
    ffn = jnp.zeros(x_ref.shape, F32)
    for kx, y_ref in enumerate((y0_ref, y1_ref, y2_ref, y3_ref)):
        ffn = ffn + _from_row_tiles(y_ref, tm, D // LANES) * gate[:, kx:kx + 1]
    z = DEEPNORM_ALPHA * x_ref[...] + ffn
    o_ref[...] = _layer_norm(z, ln_ref[0:1, :], ln_ref[1:2, :])


def _combine_ln(y_rows, gate_rows, x, ln):
    T, D = x.shape
    tm = _row_tile(T, 512)
    tile = pl.BlockSpec((tm, D), lambda i: (i, 0))
    slot = lambda kx: pl.BlockSpec((tm * (D // LANES), LANES), lambda i: (kx * (T // tm) + i, 0))
    return pl.pallas_call(
        _combine_ln_kernel,
        grid=(T // tm,),
        in_specs=[slot(kx) for kx in range(TOP_K)]
        + [pl.BlockSpec((tm, LANES), lambda i: (i, 0)), tile, pl.BlockSpec(ln.shape, lambda i: (0, 0))],
        out_specs=tile,
        out_shape=jax.ShapeDtypeStruct((T, D), F32),
        compiler_params=_params("parallel"),
    )(y_rows, y_rows, y_rows, y_rows, gate_rows, x, ln)


def _routing_plan(top_idx, counts):
    T = top_idx.shape[0]
    TK = T * TOP_K
    order = jnp.argsort(top_idx.reshape(TK)).astype(jnp.int32)
    padded = (counts + EXPERT_BLOCK - 1) // EXPERT_BLOCK * EXPERT_BLOCK
    pad_end = jnp.cumsum(padded)
    pad_start = pad_end - padded
    grp_start = jnp.cumsum(counts) - counts
    spare_start = TK + jnp.cumsum(padded - counts) - (padded - counts)
    n_blocks = -(-TK // EXPERT_BLOCK) + N_EXPERTS
    n_rows = n_blocks * EXPERT_BLOCK
    block_start = jnp.arange(n_blocks, dtype=jnp.int32) * EXPERT_BLOCK
    block_exp = jnp.minimum(jnp.sum(pad_end[None, :] <= block_start[:, None], axis=1),
                            N_EXPERTS - 1).astype(jnp.int32)
    off = (block_start - pad_start[block_exp])[:, None] + jnp.arange(EXPERT_BLOCK, dtype=jnp.int32)[None, :]
    cnt = counts[block_exp][:, None]
    valid = off < cnt
    pair = order[jnp.clip(grp_start[block_exp][:, None] + off, 0, TK - 1)]
    tok, slot = pair // TOP_K, pair % TOP_K
    spare = spare_start[block_exp][:, None] + (off - cnt)
    scale = D_MODEL // LANES
    gather_tok = jnp.where(valid, tok, 0).astype(jnp.int32) * (scale // 2)
    scatter_row = jnp.where(valid, slot * T + tok, spare).astype(jnp.int32) * scale
    fill = jnp.arange(EXPERT_BLOCK, dtype=jnp.int32)[None, :]
    gather_tok = jnp.concatenate([gather_tok, jnp.zeros_like(fill)])
    scatter_row = jnp.concatenate([(n_rows + fill) * scale, scatter_row])
    shape = (n_blocks + 1, 1, EXPERT_BLOCK)
    n_used = jnp.maximum(pad_end[-1] // EXPERT_BLOCK, 1)
    last_step = ((n_used - 1) // 2).astype(jnp.int32).reshape(1)
    return (gather_tok.reshape(shape), scatter_row.reshape(shape), block_exp, last_step,
            n_rows + 2 * EXPERT_BLOCK)


def _moe_ffn_ln(x, x_tiles, router_w, router_b, w_gu, b_gu, w_dn, b_dn, ln):
    T, D = x.shape
    w_pad = jnp.zeros((D, LANES), F32).at[:, :N_EXPERTS].set(router_w)
    w3 = jnp.concatenate(_split2(w_pad), axis=1)
    b_row = jnp.full((1, LANES), NEG_BIG, F32).at[0, :N_EXPERTS].set(router_b)
    idx_rows, gate_rows, counts = _router(x, w3, b_row)
    gather_tok, scatter_row, block_exp, last_step, n_out_rows = _routing_plan(idx_rows[:, :TOP_K], counts)
    y_rows = _moe_experts(x_tiles, gather_tok, scatter_row, block_exp, last_step, n_out_rows,
                          w_gu, b_gu, w_dn, b_dn)
    return _combine_ln(y_rows, gate_rows, x, ln)


def _cast_kernel(w_ref, o_ref):
    o_ref[...] = w_ref[...].astype(BF16)


def _cast_bf16(w):
    lead = w.shape[:-2]
    w3 = w.reshape((-1,) + w.shape[-2:])
    spec = pl.BlockSpec((1,) + w3.shape[1:], lambda i: (i, 0, 0))
    out = pl.pallas_call(
        _cast_kernel,
        grid=(w3.shape[0],),
        in_specs=[spec],
        out_specs=spec,
        out_shape=jax.ShapeDtypeStruct(w3.shape, BF16),
        compiler_params=_params("parallel"),
    )(w3)
    return out.reshape(lead + w.shape[-2:])


def _pad_rows(rows, n=SUBLANES):
    out = jnp.zeros((n, rows[0].shape[-1]), F32)
    for j, r in enumerate(rows):
        out = out.at[j].set(r.reshape(-1))
    return out


def kernel(x, ln_g, ln_b, rwkv_mix, rwkv_w_rkv, rwkv_w_o, rwkv_w0, rwkv_w1, rwkv_w2, rwkv_a0, rwkv_a1, rwkv_a2, rwkv_g1, rwkv_g2, rwkv_k_k, rwkv_k_a, rwkv_r_k, rwkv_lnx_g, rwkv_lnx_b, rwkv_v0, rwkv_v1, rwkv_v2, kv_w, da_w_q, da_w_o, da_lambda, da_subln_g, moe_router_w, moe_router_b, moe_w_gu, moe_b_gu, moe_w_dn, moe_b_dn):
    B, S, D = x.shape
    T = B * S
    v_first = None
    kv = None
    slopes = 2.0 ** (-8.0 * jnp.arange(1, DA_HEADS + 1, dtype=F32) / DA_HEADS)
    slopes = jnp.repeat(slopes, LANES).reshape(1, DA_HEADS * LANES)
    w_gu_bf16 = _cast_bf16(moe_w_gu)
    w_dn_bf16 = _cast_bf16(moe_w_dn)
    for l in range(DEPTH):
        ln_mix = jnp.stack([ln_g[l, 0], ln_b[l, 0]])
        ln_ffn = jnp.stack([ln_g[l, 1], ln_b[l, 1]])
        xt = x.reshape(T, D)
        if l < N_A_LAYERS:
            zero = jnp.zeros((D,), F32)
            vecs = _pad_rows([rwkv_w0[l], rwkv_a0[l], rwkv_v0[l - 1] if l > 0 else zero])
            vres = None if l == 0 else (rwkv_v1[l - 1].astype(BF16), rwkv_v2[l - 1].astype(BF16), v_first)
            r, lw, k, v, a, g = _rwkv_proj(
                x, _pad_rows(list(rwkv_mix[l])), rwkv_w_rkv[l].astype(BF16),
                rwkv_w1[l].astype(BF16), rwkv_w2[l].astype(BF16), rwkv_a1[l].astype(BF16),
                rwkv_a2[l].astype(BF16), rwkv_g1[l].astype(BF16), rwkv_g2[l].astype(BF16), vecs, vres)
            if l == 0:
                v_first = v
            head_params = _pad_rows([rwkv_k_k[l], rwkv_k_a[l],
                                     jnp.tile(rwkv_r_k[l].reshape(-1), 1), rwkv_lnx_g[l], rwkv_lnx_b[l]])
            y = _wkv(r, lw, k, v, a, head_params)
            xt, xt_tiles = _proj_ln(y.reshape(T, D), g.reshape(T, D), rwkv_w_o[l].astype(BF16), xt, ln_mix)
        else:
            if l == N_A_LAYERS:
                kv = _matmul(xt, kv_w.astype(BF16)).reshape(B, S, 2 * D)
            j = l - N_A_LAYERS
            lambda_init = 0.8 - 0.6 * math.exp(-0.3 * l)
            lam = da_lambda[j].astype(F32)
            lam_full = (jnp.exp(jnp.sum(lam[0] * lam[1])) - jnp.exp(jnp.sum(lam[2] * lam[3]))
                        + lambda_init)
            q = _matmul(xt, da_w_q[j].astype(BF16)).reshape(B, S, D)
            o = _diff_attention(q, kv, slopes, jnp.full((1, LANES), lam_full, F32),
                                da_subln_g[j].reshape(1, LANES), lambda_init)
            xt, xt_tiles = _proj_ln(o.reshape(T, D), None, da_w_o[j].astype(BF16), xt, ln_mix)
        xt = _moe_ffn_ln(xt, xt_tiles, moe_router_w[l], moe_router_b[l], w_gu_bf16[l], moe_b_gu[l],
                         w_dn_bf16[l], moe_b_dn[l], ln_ffn)
        x = xt.reshape(B, S, D)
    return x
```

```python
import functools
import math

import jax
import jax.numpy as jnp
from jax import lax
from jax.experimental import pallas as pl
from jax.experimental.pallas import tpu as pltpu

F32 = jnp.float32
BF16 = jnp.bfloat16

D_MODEL = 1024
DEPTH = 4
N_A_LAYERS = DEPTH // 2
RWKV_HEAD_DIM = 64
RWKV_GN_EPS = 64e-5
L2_EPS = 1e-12
DA_HEAD_DIM = 64
DA_HEADS = D_MODEL // (2 * DA_HEAD_DIM)
SUBLN_EPS = 1e-5
N_EXPERTS = 32
TOP_K = 4
EXPERT_FF = D_MODEL
SWIGLU_LIMIT = 7.0
SWIGLU_ALPHA = 1.702
EXPERT_BLOCK = 512
DEEPNORM_ALPHA = (2 * DEPTH) ** 0.25
LN_EPS = 1e-5

LANES = 128
SUBLANES = 8
VMEM_LIMIT_BYTES = 56 * 1024 * 1024
DMA_PRIORITIES = 2

WKV_CHUNK = 64
WKV_TILES_PER_STEP = 8
HEADS_PER_TILE = LANES // RWKV_HEAD_DIM
NEG_BIG = -1e30
ATTN_ONES_ROWS = 16
ATTN_HEADS_PER_STEP = 4


def _params(*sem):
    return pltpu.CompilerParams(dimension_semantics=sem, vmem_limit_bytes=VMEM_LIMIT_BYTES)


def _dot(a, b):
    return jnp.dot(a.astype(BF16), b.astype(BF16), preferred_element_type=F32)


def _dot_nt(a, b):
    return lax.dot_general(a.astype(BF16), b.astype(BF16), (((1,), (1,)), ((), ())),
                           preferred_element_type=F32)


def _split2(x):
    hi = x.astype(BF16)
    return hi, (x - hi.astype(F32)).astype(BF16)


def _dot_exact_lhs(m_bf16, x):
    hi, lo = _split2(x)
    return (jnp.dot(m_bf16, hi, preferred_element_type=F32)
            + jnp.dot(m_bf16, lo, preferred_element_type=F32))


def _layer_norm(z, g, b):
    mu = jnp.mean(z, axis=-1, keepdims=True)
    d = z - mu
    var = jnp.mean(d * d, axis=-1, keepdims=True)
    return d * lax.rsqrt(var + LN_EPS) * g + b


def _row_tile(n, want):
    t = min(want, n)
    assert n % t == 0 and t % SUBLANES == 0
    return t


def _rwkv_proj_kernel(has_vres, *refs):
    if has_vres:
        (x_ref, xp_ref, mix_ref, wrkv_ref, w1_ref, w2_ref, a1_ref, a2_ref, g1_ref, g2_ref,
         vec_ref, v1_ref, v2_ref, vfirst_ref,
         r_out, lw_out, k_out, v_out, a_out, g_out) = refs
    else:
        (x_ref, xp_ref, mix_ref, wrkv_ref, w1_ref, w2_ref, a1_ref, a2_ref, g1_ref, g2_ref,
         vec_ref, r_out, lw_out, k_out, v_out, a_out, g_out) = refs
    i = pl.program_id(1)
    x = x_ref[0]
    prev = xp_ref[0][SUBLANES - 1:SUBLANES, :]
    prev = jnp.where(i == 0, 0.0, prev)
    row = lax.broadcasted_iota(jnp.int32, x.shape, 0)
    x_prev = jnp.where(row == 0, prev, pltpu.roll(x, 1, axis=0))
    xx = x_prev - x

    def xm(j):
        return x + xx * mix_ref[j:j + 1, :]

    xv = xm(2)
    r_out[0] = _dot(xm(0), wrkv_ref[0])
    k_out[0] = _dot(xm(1), wrkv_ref[1])
    v = _dot(xv, wrkv_ref[2])
    wl = vec_ref[0:1, :] + _dot(jnp.tanh(_dot(xm(3), w1_ref[...])), w2_ref[...])
    lw_out[0] = -jax.nn.sigmoid(wl) * math.exp(-0.5)
    a_out[0] = jax.nn.sigmoid(vec_ref[1:2, :] + _dot(_dot(xm(4), a1_ref[...]), a2_ref[...]))
    g_out[0] = _dot(jax.nn.sigmoid(_dot(xm(5), g1_ref[...])), g2_ref[...])
    if has_vres:
        mixv = jax.nn.sigmoid(vec_ref[2:3, :] + _dot(_dot(xv, v1_ref[...]), v2_ref[...]))
        v = v + (vfirst_ref[0] - v) * mixv
    v_out[0] = v


def _rwkv_proj(x, mix, w_rkv, w1, w2, a1, a2, g1, g2, vecs, vres):
    B, S, D = x.shape
    tm = _row_tile(S, 512)
    has_vres = vres is not None
    full = lambda a: pl.BlockSpec(a.shape, lambda b, i: (0,) * a.ndim)
    tile = pl.BlockSpec((1, tm, D), lambda b, i: (b, i, 0))
    prev = pl.BlockSpec((1, SUBLANES, D),
                        lambda b, i: (b, jnp.maximum(i * (tm // SUBLANES) - 1, 0), 0))
    ins = [x, x, mix, w_rkv, w1, w2, a1, a2, g1, g2, vecs]
    specs = [tile, prev] + [full(a) for a in ins[2:]]
    if has_vres:
        v1, v2, v_first = vres
        ins += [v1, v2, v_first]
        specs += [full(v1), full(v2), tile]
    out = jax.ShapeDtypeStruct((B, S, D), F32)
    return pl.pallas_call(
        functools.partial(_rwkv_proj_kernel, has_vres),
        grid=(B, S // tm),
        in_specs=specs,
        out_specs=[tile] * 6,
        out_shape=[out] * 6,
        compiler_params=_params("parallel", "arbitrary"),
    )(*ins)


def _wkv_kernel(r_ref, lw_ref, k_ref, v_ref, a_ref, hp_ref, y_out,
                state, avec_s, bvec_s, kmod_s, y_s):
    C = WKV_CHUNK
    ts = r_ref.shape[1]
    n_tiles = r_ref.shape[2] // LANES
    n_chunks = ts // C
    R2 = HEADS_PER_TILE * C
    inv_n = 1.0 / RWKV_HEAD_DIM

    @pl.when(pl.program_id(2) == 0)
    def _():
        state[...] = jnp.zeros_like(state)

    li = lax.broadcasted_iota(jnp.int32, (LANES, LANES), 0)
    lj = lax.broadcasted_iota(jnp.int32, (LANES, LANES), 1)
    head_ones = (li // RWKV_HEAD_DIM == lj // RWKV_HEAD_DIM).astype(BF16)

    for t in range(n_tiles):
        ls = slice(t * LANES, (t + 1) * LANES)
        k = k_ref[0, :, ls]
        a = a_ref[0, :, ls]
        kk = k * hp_ref[0:1, ls]
        n2 = _dot(kk * kk, head_ones)
        kk = kk / jnp.maximum(jnp.sqrt(n2), L2_EPS)
        avec_s[:, ls] = -kk
        bvec_s[:, ls] = kk * a
        kmod_s[:, ls] = k * (1.0 + (a - 1.0) * hp_ref[1:2, ls])

    ri = lax.broadcasted_iota(jnp.int32, (R2, R2), 0)
    rj = lax.broadcasted_iota(jnp.int32, (R2, R2), 1)
    same_head = (ri // C) == (rj // C)
    strict = same_head & ((ri % C) > (rj % C))
    incl = same_head & ((ri % C) >= (rj % C))
    ci = lax.broadcasted_iota(jnp.int32, (C, C), 0)
    cj = lax.broadcasted_iota(jnp.int32, (C, C), 1)
    tri = (ci >= cj).astype(BF16)
    lane = lax.broadcasted_iota(jnp.int32, (C, LANES), 1)
    head0 = lane < RWKV_HEAD_DIM
    n_doublings = int(math.log2(C))

    def stack(t):
        return jnp.concatenate([jnp.where(head0, t, 0.0), jnp.where(head0, 0.0, t)], axis=0)

    tiles = range(n_tiles)
    lanes_of = lambda t: slice(t * LANES, (t + 1) * LANES)

    def chunk(c, carry):
        sl = pl.ds(pl.multiple_of(c * C, C), C)
        each = lambda f: [f(t) for t in tiles]
        s0 = each(lambda t: state[t])
        lw = each(lambda t: lw_ref[0, sl, lanes_of(t)])
        cum = each(lambda t: _dot_exact_lhs(tri, lw[t]))
        tot = each(lambda t: cum[t][C - 1:C, :])
        e_neg = each(lambda t: jnp.exp(-cum[t]))
        av = each(lambda t: avec_s[sl, lanes_of(t)])
        bv = each(lambda t: bvec_s[sl, lanes_of(t)])
        km = each(lambda t: kmod_s[sl, lanes_of(t)])
        lhs = each(lambda t: jnp.concatenate(
            [stack(av[t] * jnp.exp(cum[t] - lw[t])),
             stack(r_ref[0, sl, lanes_of(t)] * jnp.exp(cum[t]))], axis=0))
        rhs = each(lambda t: jnp.concatenate([stack(bv[t] * e_neg[t]), stack(km[t] * e_neg[t])], axis=0))
        g = each(lambda t: _dot_nt(lhs[t], rhs[t]))
        a_s = each(lambda t: _dot_nt(lhs[t], s0[t]))
        vm = each(lambda t: stack(v_ref[0, sl, lanes_of(t)]))
        akv = each(lambda t: _dot(jnp.where(strict, g[t][:R2, R2:], 0.0), vm[t]))
        u = each(lambda t: a_s[t][:R2] + akv[t])
        p = each(lambda t: jnp.where(strict, g[t][:R2, :R2], 0.0))
        for it in range(n_doublings):
            pu = each(lambda t: _dot(p[t], u[t]))
            if it + 1 < n_doublings:
                p = each(lambda t: _dot(p[t], p[t]))
            u = each(lambda t: u[t] + pu[t])
        uv = each(lambda t: jnp.concatenate([u[t], vm[t]], axis=0))
        w_y = each(lambda t: jnp.concatenate([jnp.where(incl, g[t][R2:, :R2], 0.0),
                                              jnp.where(incl, g[t][R2:, R2:], 0.0)], axis=1))
        y_sm = each(lambda t: a_s[t][R2:] + _dot(w_y[t], uv[t]))
        bk_end = each(lambda t: jnp.concatenate(
            [stack(bv[t] * jnp.exp(tot[t] - cum[t])), stack(km[t] * jnp.exp(tot[t] - cum[t]))], axis=0))
        s1 = each(lambda t: s0[t] * jnp.exp(tot[t]) + _dot(uv[t].T, bk_end[t]))
        for t in tiles:
            y_s[sl, lanes_of(t)] = y_sm[t][:C] + y_sm[t][C:]
            state[t] = s1[t]
        return carry

    lax.fori_loop(0, n_chunks, chunk, 0)

    for t in range(n_tiles):
        ls = slice(t * LANES, (t + 1) * LANES)
        y = y_s[:, ls]
        mu = _dot(y, head_ones) * inv_n
        d = y - mu
        var = _dot(d * d, head_ones) * inv_n
        yn = d * lax.rsqrt(var + RWKV_GN_EPS) * hp_ref[3:4, ls] + hp_ref[4:5, ls]
        bonus = _dot(r_ref[0, :, ls] * kmod_s[:, ls] * hp_ref[2:3, ls], head_ones)
        y_out[0, :, ls] = yn + bonus * v_ref[0, :, ls]


def _wkv(r, lw, k, v, a, head_params):
    B, S, D = r.shape
    ts = _row_tile(S, 512)
    assert ts % WKV_CHUNK == 0
    width = WKV_TILES_PER_STEP * LANES
    tile = pl.BlockSpec((1, ts, width), lambda b, p, i: (b, i, p))
    hp = pl.BlockSpec((SUBLANES, width), lambda b, p, i: (0, p))
    return pl.pallas_call(
        _wkv_kernel,
        grid=(B, D // width, S // ts),
        in_specs=[tile] * 5 + [hp],
        out_specs=tile,
        out_shape=jax.ShapeDtypeStruct((B, S, D), F32),
        scratch_shapes=[pltpu.VMEM((WKV_TILES_PER_STEP, LANES, LANES), F32)]
        + [pltpu.VMEM((ts, width), F32)] * 4,
        compiler_params=_params("parallel", "parallel", "arbitrary"),
    )(r, lw, k, v, a, head_params)


def _matmul_kernel(a_ref, w_ref, o_ref):
    o_ref[...] = _dot(a_ref[...], w_ref[...])


def _matmul(a, w):
    T, K = a.shape
    N = w.shape[1]
    tm = _row_tile(T, 512)
    return pl.pallas_call(
        _matmul_kernel,
        grid=(T // tm,),
        in_specs=[pl.BlockSpec((tm, K), lambda i: (i, 0)), pl.BlockSpec((K, N), lambda i: (0, 0))],
        out_specs=pl.BlockSpec((tm, N), lambda i: (i, 0)),
        out_shape=jax.ShapeDtypeStruct((T, N), F32),
        compiler_params=_params("parallel"),
    )(a, w)


def _to_row_tiles(ref, value):
    n = value.shape[1] // LANES
    for c in range(n):
        ref[pl.ds(c, value.shape[0], stride=n), :] = value[:, c * LANES:(c + 1) * LANES]


def _from_row_tiles(ref, n_rows, n):
    return jnp.concatenate([ref[pl.ds(c, n_rows, stride=n), :] for c in range(n)], axis=1)


def _bf16_bits(v):
    return lax.bitcast_convert_type(v.astype(BF16).astype(F32), jnp.uint32)


def _to_packed_tiles(ref, value):
    m = value.shape[1] // (2 * LANES)
    for c in range(m):
        lo = _bf16_bits(value[:, c * LANES:(c + 1) * LANES]) >> 16
        hi = _bf16_bits(value[:, (c + m) * LANES:(c + m + 1) * LANES]) & jnp.uint32(0xFFFF0000)
        ref[pl.ds(c, value.shape[0], stride=m), :] = lo | hi


def _from_packed_tiles(ref, n_rows, m):
    words = [ref[pl.ds(c, n_rows, stride=m), :] for c in range(m)]
    as_f32 = lambda w: lax.bitcast_convert_type(w, F32)
    chunks = ([as_f32(w << 16) for w in words]
              + [as_f32(w & jnp.uint32(0xFFFF0000)) for w in words])
    return jnp.concatenate(chunks, axis=1).astype(BF16)


def _proj_ln_kernel(has_gate, *refs):
    if has_gate:
        a_ref, g_ref, w_ref, x_ref, ln_ref, o_ref, o_tiles_ref = refs
        a = a_ref[...] * g_ref[...]
    else:
        a_ref, w_ref, x_ref, ln_ref, o_ref, o_tiles_ref = refs
        a = a_ref[...]
    z = DEEPNORM_ALPHA * x_ref[...] + _dot(a, w_ref[...])
    out = _layer_norm(z, ln_ref[0:1, :], ln_ref[1:2, :])
    o_ref[...] = out
    _to_packed_tiles(o_tiles_ref, out)


def _proj_ln(a, gate, w, x, ln):
    T, D = x.shape
    tm = _row_tile(T, 512)
    n = D // (2 * LANES)
    tile = pl.BlockSpec((tm, D), lambda i: (i, 0))
    full = lambda t: pl.BlockSpec(t.shape, lambda i: (0, 0))
    has_gate = gate is not None
    ins = [a] + ([gate] if has_gate else []) + [w, x, ln]
    specs = [tile] + ([tile] if has_gate else []) + [full(w), tile, full(ln)]
    return pl.pallas_call(
        functools.partial(_proj_ln_kernel, has_gate),
        grid=(T // tm,),
        in_specs=specs,
        out_specs=[tile, pl.BlockSpec((tm * n, LANES), lambda i: (i, 0))],
        out_shape=[jax.ShapeDtypeStruct((T, D), F32), jax.ShapeDtypeStruct((T * n, LANES), jnp.uint32)],
        compiler_params=_params("parallel"),
    )(*ins)


def _max_rows(s, groups=4):
    step = s.shape[0] // groups
    part = s[0:step]
    for g in range(1, groups):
        part = jnp.maximum(part, s[g * step:(g + 1) * step])
    return jnp.max(part, axis=0, keepdims=True)


def _attn_kernel(out_scale, q_ref, k_ref, v_ref, slope_ref, lam_ref, g_ref, o_ref,
                 k_s, vt_s, acc_s):
    i = pl.program_id(2)
    tq = q_ref.shape[1]
    tk = tq
    n_heads = q_ref.shape[2] // LANES
    n_kv_blocks = k_ref.shape[1] // tk
    heads = range(n_heads)
    chains = [(h, c) for h in heads for c in range(2)]
    lanes_of = lambda h: slice(h * LANES, (h + 1) * LANES)

    @pl.when(i == 0)
    def _():
        lane = lax.broadcasted_iota(jnp.int32, (tk, LANES), 1)
        row = lax.broadcasted_iota(jnp.int32, (tk, LANES), 0).astype(F32)
        for h in heads:
            key_bias = slope_ref[0, :, h * LANES:h * LANES + 1] * row
            for blk in range(n_kv_blocks):
                sl = slice(blk * tk, (blk + 1) * tk)
                kb = k_ref[0, sl, lanes_of(h)]
                k_s[2 * h, sl, :] = jnp.where(
                    lane < DA_HEAD_DIM, kb, jnp.where(lane == DA_HEAD_DIM, key_bias, 0.0)).astype(BF16)
                k_s[2 * h + 1, sl, :] = jnp.where(
                    lane >= DA_HEAD_DIM, kb, jnp.where(lane == 0, key_bias, 0.0)).astype(BF16)
                vt_s[h, blk, 0:LANES, :] = v_ref[0, sl, lanes_of(h)].T.astype(BF16)
                vt_s[h, blk, LANES:, :] = jnp.ones((ATTN_ONES_ROWS, tk), BF16)

    q_lane = lax.broadcasted_iota(jnp.int32, (tq, LANES), 1)
    q = []
    for h in heads:
        qs = q_ref[0, :, lanes_of(h)] * (DA_HEAD_DIM ** -0.5)
        q.append(jnp.where(q_lane < DA_HEAD_DIM, qs,
                           jnp.where(q_lane == DA_HEAD_DIM, 1.0, 0.0)).astype(BF16))
        q.append(jnp.where(q_lane >= DA_HEAD_DIM, qs, jnp.where(q_lane == 0, 1.0, 0.0)).astype(BF16))
    slope = [slope_ref[0, :, h * LANES:h * LANES + 1] for h in heads]
    k_loc = lax.broadcasted_iota(jnp.int32, (tk, tq), 0)
    q_loc = lax.broadcasted_iota(jnp.int32, (tk, tq), 1)
    acc_s[...] = jnp.zeros_like(acc_s)

    def block(j, ms, diagonal):
        sl = pl.ds(pl.multiple_of(j * tk, tk), tk)
        rel = ((j - i) * tk).astype(F32)
        each = lambda f: [f(n, h) for n, (h, _) in enumerate(chains)]
        s = each(lambda n, h: _dot_nt(k_s[n, sl, :], q[n]))
        if diagonal:
            s = each(lambda n, h: jnp.where(k_loc > q_loc, NEG_BIG, s[n]))
        offset = [slope[h] * rel for h in heads]
        m_new = each(lambda n, h: jnp.maximum(ms[n], _max_rows(s[n]) + offset[h]))
        p = each(lambda n, h: jnp.exp(s[n] - (m_new[n] - offset[h])).astype(BF16))
        acc_old = each(lambda n, h: acc_s[n])
        pv = each(lambda n, h: jnp.dot(vt_s[h, j], p[n], preferred_element_type=F32))
        for n in range(len(chains)):
            acc_s[n] = jnp.exp(ms[n] - m_new[n]) * acc_old[n] + pv[n]
        return tuple(m_new)

    init = tuple(jnp.full((1, tq), NEG_BIG, F32) for _ in chains)
    ms = lax.fori_loop(0, i, lambda j, ms: block(j, ms, False), init)
    block(i, ms, True)

    for h in heads:
        acc0, acc1 = acc_s[2 * h], acc_s[2 * h + 1]
        o_t = (acc0[:LANES] / acc0[LANES:LANES + 1]
               - lam_ref[:, 0:1] * (acc1[:LANES] / acc1[LANES:LANES + 1]))
        o = o_t.T
        o = o * lax.rsqrt(jnp.mean(o * o, axis=-1, keepdims=True) + SUBLN_EPS) * g_ref[...]
        o_ref[0, :, lanes_of(h)] = o * out_scale


def _diff_attention(q, kv, slopes, lam_row, subln_g, lambda_init):
    B, S, D = q.shape
    tq = _row_tile(S, 256)
    nh = ATTN_HEADS_PER_STEP
    width = nh * LANES
    n_groups = DA_HEADS // nh
    return pl.pallas_call(
        functools.partial(_attn_kernel, 1.0 - lambda_init),
        grid=(B, n_groups, S // tq),
        in_specs=[pl.BlockSpec((1, tq, width), lambda b, h, i: (b, i, h)),
                  pl.BlockSpec((1, S, width), lambda b, h, i: (b, 0, h)),
                  pl.BlockSpec((1, S, width), lambda b, h, i: (b, 0, n_groups + h)),
                  pl.BlockSpec((1, 1, width), lambda b, h, i: (h, 0, 0)),
                  pl.BlockSpec((1, LANES), lambda b, h, i: (0, 0)),
                  pl.BlockSpec((1, LANES), lambda b, h, i: (0, 0))],
        out_specs=pl.BlockSpec((1, tq, width), lambda b, h, i: (b, i, h)),
        out_shape=jax.ShapeDtypeStruct((B, S, D), F32),
        scratch_shapes=[pltpu.VMEM((2 * nh, S, LANES), BF16),
                        pltpu.VMEM((nh, S // tq, LANES + ATTN_ONES_ROWS, tq), BF16),
                        pltpu.VMEM((2 * nh, LANES + ATTN_ONES_ROWS, tq), F32)],
        compiler_params=_params("parallel", "parallel", "arbitrary"),
    )(q, kv, kv, slopes.reshape(n_groups, 1, width), lam_row, subln_g)


def _router_kernel(x_ref, w_ref, b_ref, idx_out, gate_out, count_out):
    @pl.when(pl.program_id(0) == 0)
    def _():
        count_out[...] = jnp.zeros_like(count_out)

    x = x_ref[...]
    xh, xl = _split2(x)
    w = w_ref[...]
    hh = jnp.dot(xh, w, preferred_element_type=F32)
    lh = jnp.dot(xl, w[:, :LANES], preferred_element_type=F32)
    logits = hh[:, :LANES] + (hh[:, LANES:] + lh) + b_ref[...]
    lane = lax.broadcasted_iota(jnp.int32, logits.shape, 1)
    vals, idxs = [], []
    cur = logits
    for _ in range(TOP_K):
        m = jnp.max(cur, axis=-1, keepdims=True)
        idx = jnp.min(jnp.where(cur == m, lane, LANES), axis=-1, keepdims=True)
        cur = jnp.where(lane == idx, -jnp.inf, cur)
        vals.append(m)
        idxs.append(idx)
    es = [jnp.exp(v - vals[0]) for v in vals]
    denom = es[0] + es[1] + es[2] + es[3]
    idx_row = jnp.zeros(logits.shape, jnp.int32)
    gate_row = jnp.zeros(logits.shape, F32)
    for kx in range(TOP_K):
        idx_row = jnp.where(lane == kx, idxs[kx], idx_row)
        gate_row = jnp.where(lane == kx, es[kx] / denom, gate_row)
    idx_out[...] = idx_row
    gate_out[...] = gate_row
    picked = sum((lane == idx).astype(jnp.int32) for idx in idxs)
    count_out[0:1, :] += jnp.sum(picked, axis=0, keepdims=True)


def _router(x, w3, b_row):
    T, D = x.shape
    tm = _row_tile(T, 512)
    tile = pl.BlockSpec((tm, D), lambda i: (i, 0))
    out_tile = pl.BlockSpec((tm, LANES), lambda i: (i, 0))
    idx_rows, gate_rows, counts = pl.pallas_call(
        _router_kernel,
        grid=(T // tm,),
        in_specs=[tile, pl.BlockSpec(w3.shape, lambda i: (0, 0)),
                  pl.BlockSpec(b_row.shape, lambda i: (0, 0))],
        out_specs=[out_tile, out_tile, pl.BlockSpec((SUBLANES, LANES), lambda i: (0, 0))],
        out_shape=[jax.ShapeDtypeStruct((T, LANES), jnp.int32),
                   jax.ShapeDtypeStruct((T, LANES), F32),
                   jax.ShapeDtypeStruct((SUBLANES, LANES), jnp.int32)],
        compiler_params=_params("arbitrary"),
    )(x, w3, b_row)
    return idx_rows, gate_rows, counts[0, :N_EXPERTS]


def _clamped_swiglu(gate, up):
    gate = jnp.minimum(gate, SWIGLU_LIMIT)
    up = jnp.clip(up, -SWIGLU_LIMIT, SWIGLU_LIMIT)
    return (up + 1.0) * (gate * jax.nn.sigmoid(SWIGLU_ALPHA * gate))


def _moe_kernel(bexp_ref, glast_ref, g_first, g_b, g_next, s_prev, s_a, s_b, x_hbm,
                wgu_a, bgu_a, wdn_a, bdn_a, wgu_b, bgu_b, wdn_b, bdn_b, y_hbm,
                xbuf_a, xbuf_b, ybuf_a, ybuf_b, zbuf, sem_ga, sem_gb, sem_sa, sem_sb, sem_z):
    del bexp_ref
    g = pl.program_id(0)
    last = glast_ref[0]

    n = wdn_a.shape[2] // LANES
    m = wgu_a.shape[1] // (2 * LANES)

    def gather(idx_ref, xbuf, sem, j):
        return pltpu.make_async_copy(x_hbm.at[pl.ds(pl.multiple_of(idx_ref[0, 0, j], m), m)],
                                     xbuf.at[pl.ds(j * m, m)], sem)

    def scatter(idx_ref, ybuf, sem, j):
        return pltpu.make_async_copy(ybuf.at[pl.ds(j * n, n)],
                                     y_hbm.at[pl.ds(pl.multiple_of(idx_ref[0, 0, j], n), n)], sem)

    def wait_gather(xbuf, sem):
        pltpu.make_async_copy(x_hbm.at[pl.ds(0, EXPERT_BLOCK * m)], xbuf, sem).wait()

    def wait_scatter(ybuf, sem):
        pltpu.make_async_copy(ybuf, y_hbm.at[pl.ds(0, EXPERT_BLOCK * n)], sem).wait()

    def expert_phase(xbuf, ybuf, wgu, bgu, wdn, bdn, sem_x, sem_y, gather_next, scatter_done):
        ff = wdn.shape[1]
        for j in range(EXPERT_BLOCK):
            gather_next(j).start(priority=0)
        wait_gather(xbuf, sem_x)
        h = _dot(_from_packed_tiles(xbuf, EXPERT_BLOCK, m), wgu[0]) + bgu[0]
        act = _clamped_swiglu(h[:, :ff], h[:, ff:])
        for j in range(EXPERT_BLOCK):
            scatter_done(j).start(priority=DMA_PRIORITIES - 1)
        wait_scatter(ybuf, sem_y)
        _to_row_tiles(ybuf, _dot(act, wdn[0]) + bdn[0])

    @pl.when(g == 0)
    def _():
        ybuf_a[...] = jnp.zeros_like(ybuf_a)
        ybuf_b[...] = jnp.zeros_like(ybuf_b)
        zbuf[...] = jnp.zeros_like(zbuf)
        spare0 = y_hbm.shape[0] - EXPERT_BLOCK * n

        def first(j, c):
            gather(g_first, xbuf_a, sem_ga, j).start()
            pltpu.make_async_copy(ybuf_a.at[pl.ds(j * n, n)],
                                  y_hbm.at[pl.ds(pl.multiple_of(spare0 + j * n, n), n)], sem_sa).start()
            return c
        lax.fori_loop(0, EXPERT_BLOCK, first, 0)

    @pl.when(g <= last)
    def _():
        expert_phase(xbuf_a, ybuf_a, wgu_a, bgu_a, wdn_a, bdn_a, sem_ga, sem_sa,
                     lambda j: gather(g_b, xbuf_b, sem_gb, j),
                     lambda j: scatter(s_prev, ybuf_b, sem_sb, j))
        expert_phase(xbuf_b, ybuf_b, wgu_b, bgu_b, wdn_b, bdn_b, sem_gb, sem_sb,
                     lambda j: gather(g_next, xbuf_a, sem_ga, j),
                     lambda j: scatter(s_a, ybuf_a, sem_sa, j))

    @pl.when(g > last)
    def _():
        fills = [pltpu.make_async_copy(zbuf, y_hbm.at[pl.ds(pl.multiple_of(plan[0, 0, 0], n),
                                                           EXPERT_BLOCK * n)], sem_z)
                 for plan in (s_a, s_b)]
        for f in fills:
            f.start()
        for f in fills:
            f.wait()

    @pl.when(g == last)
    def _():
        def final(j, c):
            scatter(s_b, ybuf_b, sem_sb, j).start()
            return c
        lax.fori_loop(0, EXPERT_BLOCK, final, 0)
        wait_gather(xbuf_a, sem_ga)
        wait_scatter(ybuf_a, sem_sa)
        wait_scatter(ybuf_b, sem_sb)


def _moe_experts(x_tiles, gather_tok, scatter_row, block_exp, last_step, n_out_rows, w_gu, b_gu, w_dn, b_dn):
    D = w_gu.shape[1]
    n = D // LANES
    n_blocks = block_exp.shape[0]
    assert n_blocks % 2 == 0
    FF2 = w_gu.shape[2]
    idx_spec = lambda fn: pl.BlockSpec((1, 1, EXPERT_BLOCK), fn, memory_space=pltpu.SMEM)
    weights = lambda off: [
        pl.BlockSpec((1, D, FF2), lambda g, be, gl: (be[2 * g + off], 0, 0)),
        pl.BlockSpec((1, 1, FF2), lambda g, be, gl: (be[2 * g + off], 0, 0)),
        pl.BlockSpec((1, EXPERT_FF, D), lambda g, be, gl: (be[2 * g + off], 0, 0)),
        pl.BlockSpec((1, 1, D), lambda g, be, gl: (be[2 * g + off], 0, 0)),
    ]
    grid_spec = pltpu.PrefetchScalarGridSpec(
        num_scalar_prefetch=2,
        grid=(n_blocks // 2,),
        in_specs=[
            idx_spec(lambda g, be, gl: (0, 0, 0)),
            idx_spec(lambda g, be, gl: (2 * g + 1, 0, 0)),
            idx_spec(lambda g, be, gl: (2 * g + 2, 0, 0)),
            idx_spec(lambda g, be, gl: (2 * g, 0, 0)),
            idx_spec(lambda g, be, gl: (2 * g + 1, 0, 0)),
            idx_spec(lambda g, be, gl: (2 * g + 2, 0, 0)),
            pl.BlockSpec(memory_space=pl.ANY),
        ] + weights(0) + weights(1),
        out_specs=pl.BlockSpec(memory_space=pl.ANY),
        scratch_shapes=[pltpu.VMEM((EXPERT_BLOCK * n // 2, LANES), jnp.uint32)] * 2
        + [pltpu.VMEM((EXPERT_BLOCK * n, LANES), F32)] * 3
        + [pltpu.SemaphoreType.DMA(())] * 5,
    )
    b_gu = b_gu.reshape(N_EXPERTS, 1, FF2)
    b_dn = b_dn.reshape(N_EXPERTS, 1, D)
    return pl.pallas_call(
        _moe_kernel,
        grid_spec=grid_spec,
        out_shape=jax.ShapeDtypeStruct((n_out_rows * n, LANES), F32),
        compiler_params=_params("arbitrary"),
    )(block_exp, last_step, gather_tok, gather_tok, gather_tok, scatter_row, scatter_row, scatter_row, x_tiles,
      w_gu, b_gu, w_dn, b_dn, w_gu, b_gu, w_dn, b_dn)


def _combine_ln_kernel(y0_ref, y1_ref, y2_ref, y3_ref, gate_ref, x_ref, ln_ref, o_ref):
    gate = gate_ref[...]
    tm, D = x_ref.shape
    ffn = jnp.zeros(x_ref.shape, F32)
    for kx, y_ref in enumerate((y0_ref, y1_ref, y2_ref, y3_ref)):
        ffn = ffn + _from_row_tiles(y_ref, tm, D // LANES) * gate[:, kx:kx + 1]
    z = DEEPNORM_ALPHA * x_ref[...] + ffn
    o_ref[...] = _layer_norm(z, ln_ref[0:1, :], ln_ref[1:2, :])


def _combine_ln(y_rows, gate_rows, x, ln):
    T, D = x.shape
    tm = _row_tile(T, 512)
    tile = pl.BlockSpec((tm, D), lambda i: (i, 0))
    slot = lambda kx: pl.BlockSpec((tm * (D // LANES), LANES), lambda i: (kx * (T // tm) + i, 0))
    return pl.pallas_call(
        _combine_ln_kernel,
        grid=(T // tm,),
        in_specs=[slot(kx) for kx in range(TOP_K)]
        + [pl.BlockSpec((tm, LANES), lambda i: (i, 0)), tile, pl.BlockSpec(ln.shape, lambda i: (0, 0))],
        out_specs=tile,
        out_shape=jax.ShapeDtypeStruct((T, D), F32),
        compiler_params=_params("parallel"),
    )(y_rows, y_rows, y_rows, y_rows, gate_rows, x, ln)


def _routing_plan(top_idx, counts):
    T = top_idx.shape[0]
    TK = T * TOP_K
    order = jnp.argsort(top_idx.reshape(TK)).astype(jnp.int32)
    padded = (counts + EXPERT_BLOCK - 1) // EXPERT_BLOCK * EXPERT_BLOCK
    pad_end = jnp.cumsum(padded)
    pad_start = pad_end - padded
    grp_start = jnp.cumsum(counts) - counts
    spare_start = TK + jnp.cumsum(padded - counts) - (padded - counts)
    n_blocks = -(-TK // EXPERT_BLOCK) + N_EXPERTS
    n_rows = n_blocks * EXPERT_BLOCK
    block_start = jnp.arange(n_blocks, dtype=jnp.int32) * EXPERT_BLOCK
    block_exp = jnp.minimum(jnp.sum(pad_end[None, :] <= block_start[:, None], axis=1),
                            N_EXPERTS - 1).astype(jnp.int32)
    off = (block_start - pad_start[block_exp])[:, None] + jnp.arange(EXPERT_BLOCK, dtype=jnp.int32)[None, :]
    cnt = counts[block_exp][:, None]
    valid = off < cnt
    pair = order[jnp.clip(grp_start[block_exp][:, None] + off, 0, TK - 1)]
    tok, slot = pair // TOP_K, pair % TOP_K
    spare = spare_start[block_exp][:, None] + (off - cnt)
    scale = D_MODEL // LANES
    gather_tok = jnp.where(valid, tok, 0).astype(jnp.int32) * (scale // 2)
    scatter_row = jnp.where(valid, slot * T + tok, spare).astype(jnp.int32) * scale
    fill = jnp.arange(EXPERT_BLOCK, dtype=jnp.int32)[None, :]
    gather_tok = jnp.concatenate([gather_tok, jnp.zeros_like(fill)])
    scatter_row = jnp.concatenate([(n_rows + fill) * scale, scatter_row])
    shape = (n_blocks + 1, 1, EXPERT_BLOCK)
    n_used = jnp.maximum(pad_end[-1] // EXPERT_BLOCK, 1)
    last_step = ((n_used - 1) // 2).astype(jnp.int32).reshape(1)
    return (gather_tok.reshape(shape), scatter_row.reshape(shape), block_exp, last_step,
            n_rows + 2 * EXPERT_BLOCK)


def _moe_ffn_ln(x, x_tiles, router_w, router_b, w_gu, b_gu, w_dn, b_dn, ln):
    T, D = x.shape
    w_pad = jnp.zeros((D, LANES), F32).at[:, :N_EXPERTS].set(router_w)
    w3 = jnp.concatenate(_split2(w_pad), axis=1)
    b_row = jnp.full((1, LANES), NEG_BIG, F32).at[0, :N_EXPERTS].set(router_b)
    idx_rows, gate_rows, counts = _router(x, w3, b_row)
    gather_tok, scatter_row, block_exp, last_step, n_out_rows = _routing_plan(idx_rows[:, :TOP_K], counts)
    y_rows = _moe_experts(x_tiles, gather_tok, scatter_row, block_exp, last_step, n_out_rows,
                          w_gu, b_gu, w_dn, b_dn)
    return _combine_ln(y_rows, gate_rows, x, ln)


def _cast_kernel(w_ref, o_ref):
    o_ref[...] = w_ref[...].astype(BF16)


def _cast_bf16(w):
    lead = w.shape[:-2]
    w3 = w.reshape((-1,) + w.shape[-2:])
    spec = pl.BlockSpec((1,) + w3.shape[1:], lambda i: (i, 0, 0))
    out = pl.pallas_call(
        _cast_kernel,
        grid=(w3.shape[0],),
        in_specs=[spec],
        out_specs=spec,
        out_shape=jax.ShapeDtypeStruct(w3.shape, BF16),
        compiler_params=_params("parallel"),
    )(w3)
    return out.reshape(lead + w.shape[-2:])


def _pad_rows(rows, n=SUBLANES):
    out = jnp.zeros((n, rows[0].shape[-1]), F32)
    for j, r in enumerate(rows):
        out = out.at[j].set(r.reshape(-1))
    return out


def kernel(x, ln_g, ln_b, rwkv_mix, rwkv_w_rkv, rwkv_w_o, rwkv_w0, rwkv_w1, rwkv_w2, rwkv_a0, rwkv_a1, rwkv_a2, rwkv_g1, rwkv_g2, rwkv_k_k, rwkv_k_a, rwkv_r_k, rwkv_lnx_g, rwkv_lnx_b, rwkv_v0, rwkv_v1, rwkv_v2, kv_w, da_w_q, da_w_o, da_lambda, da_subln_g, moe_router_w, moe_router_b, moe_w_gu, moe_b_gu, moe_w_dn, moe_b_dn):
    B, S, D = x.shape
    T = B * S
    v_first = None
    kv = None
    slopes = 2.0 ** (-8.0 * jnp.arange(1, DA_HEADS + 1, dtype=F32) / DA_HEADS)
    slopes = jnp.repeat(slopes, LANES).reshape(1, DA_HEADS * LANES)
    w_gu_bf16 = _cast_bf16(moe_w_gu)
    w_dn_bf16 = _cast_bf16(moe_w_dn)
    for l in range(DEPTH):
        ln_mix = jnp.stack([ln_g[l, 0], ln_b[l, 0]])
        ln_ffn = jnp.stack([ln_g[l, 1], ln_b[l, 1]])
        xt = x.reshape(T, D)
        if l < N_A_LAYERS:
            zero = jnp.zeros((D,), F32)
            vecs = _pad_rows([rwkv_w0[l], rwkv_a0[l], rwkv_v0[l - 1] if l > 0 else zero])
            vres = None if l == 0 else (rwkv_v1[l - 1].astype(BF16), rwkv_v2[l - 1].astype(BF16), v_first)
            r, lw, k, v, a, g = _rwkv_proj(
                x, _pad_rows(list(rwkv_mix[l])), rwkv_w_rkv[l].astype(BF16),
                rwkv_w1[l].astype(BF16), rwkv_w2[l].astype(BF16), rwkv_a1[l].astype(BF16),
                rwkv_a2[l].astype(BF16), rwkv_g1[l].astype(BF16), rwkv_g2[l].astype(BF16), vecs, vres)
            if l == 0:
                v_first = v
            head_params = _pad_rows([rwkv_k_k[l], rwkv_k_a[l],
                                     jnp.tile(rwkv_r_k[l].reshape(-1), 1), rwkv_lnx_g[l], rwkv_lnx_b[l]])
            y = _wkv(r, lw, k, v, a, head_params)
            xt, xt_tiles = _proj_ln(y.reshape(T, D), g.reshape(T, D), rwkv_w_o[l].astype(BF16), xt, ln_mix)
        else:
            if l == N_A_LAYERS:
                kv = _matmul(xt, kv_w.astype(BF16)).reshape(B, S, 2 * D)
            j = l - N_A_LAYERS
            lambda_init = 0.8 - 0.6 * math.exp(-0.3 * l)
            lam = da_lambda[j].astype(F32)
            lam_full = (jnp.exp(jnp.sum(lam[0] * lam[1])) - jnp.exp(jnp.sum(lam[2] * lam[3]))
                        + lambda_init)
            q = _matmul(xt, da_w_q[j].astype(BF16)).reshape(B, S, D)
            o = _diff_attention(q, kv, slopes, jnp.full((1, LANES), lam_full, F32),
                                da_subln_g[j].reshape(1, LANES), lambda_init)
            xt, xt_tiles = _proj_ln(o.reshape(T, D), None, da_w_o[j].astype(BF16), xt, ln_mix)
        xt = _moe_ffn_ln(xt, xt_tiles, moe_router_w[l], moe_router_b[l], w_gu_bf16[l], moe_b_gu[l],
                         w_dn_bf16[l], moe_b_dn[l], ln_ffn)
        x = xt.reshape(B, S, D)
    return x
```

```python
import functools
import math

import jax
import jax.numpy as jnp
from jax import lax
from jax.experimental import pallas as pl
from jax.experimental.pallas import tpu as pltpu

F32 = jnp.float32
BF16 = jnp.bfloat16

D_MODEL = 1024
DEPTH = 4
N_A_LAYERS = DEPTH // 2
RWKV_HEAD_DIM = 64
RWKV_GN_EPS = 64e-5
L2_EPS = 1e-12
DA_HEAD_DIM = 64
DA_HEADS = D_MODEL // (2 * DA_HEAD_DIM)
SUBLN_EPS = 1e-5
N_EXPERTS = 32
TOP_K = 4
EXPERT_FF = D_MODEL
SWIGLU_LIMIT = 7.0
SWIGLU_ALPHA = 1.702
EXPERT_BLOCK = 512
DEEPNORM_ALPHA = (2 * DEPTH) ** 0.25
LN_EPS = 1e-5

LANES = 128
SUBLANES = 8
VMEM_LIMIT_BYTES = 56 * 1024 * 1024
DMA_PRIORITIES = 2

WKV_CHUNK = 64
WKV_TILES_PER_STEP = 8
HEADS_PER_TILE = LANES // RWKV_HEAD_DIM
NEG_BIG = -1e30
ATTN_ONES_ROWS = 16
ATTN_HEADS_PER_STEP = 4


def _params(*sem):
    return pltpu.CompilerParams(dimension_semantics=sem, vmem_limit_bytes=VMEM_LIMIT_BYTES)


def _dot(a, b):
    return jnp.dot(a.astype(BF16), b.astype(BF16), preferred_element_type=F32)


def _dot_nt(a, b):
    return lax.dot_general(a.astype(BF16), b.astype(BF16), (((1,), (1,)), ((), ())),
                           preferred_element_type=F32)


def _split2(x):
    hi = x.astype(BF16)
    return hi, (x - hi.astype(F32)).astype(BF16)


def _dot_exact_lhs(m_bf16, x):
    hi, lo = _split2(x)
    return (jnp.dot(m_bf16, hi, preferred_element_type=F32)
            + jnp.dot(m_bf16, lo, preferred_element_type=F32))


def _layer_norm(z, g, b):
    mu = jnp.mean(z, axis=-1, keepdims=True)
    d = z - mu
    var = jnp.mean(d * d, axis=-1, keepdims=True)
    return d * lax.rsqrt(var + LN_EPS) * g + b


def _row_tile(n, want):
    t = min(want, n)
    assert n % t == 0 and t % SUBLANES == 0
    return t


def _rwkv_proj_kernel(has_vres, *refs):
    if has_vres:
        (x_ref, xp_ref, mix_ref, wrkv_ref, w1_ref, w2_ref, a1_ref, a2_ref, g1_ref, g2_ref,
         vec_ref, v1_ref, v2_ref, vfirst_ref,
         r_out, lw_out, k_out, v_out, a_out, g_out) = refs
    else:
        (x_ref, xp_ref, mix_ref, wrkv_ref, w1_ref, w2_ref, a1_ref, a2_ref, g1_ref, g2_ref,
         vec_ref, r_out, lw_out, k_out, v_out, a_out, g_out) = refs
    i = pl.program_id(1)
    x = x_ref[0]
    prev = xp_ref[0][SUBLANES - 1:SUBLANES, :]
    prev = jnp.where(i == 0, 0.0, prev)
    row = lax.broadcasted_iota(jnp.int32, x.shape, 0)
    x_prev = jnp.where(row == 0, prev, pltpu.roll(x, 1, axis=0))
    xx = x_prev - x

    def xm(j):
        return x + xx * mix_ref[j:j + 1, :]

    xv = xm(2)
    r_out[0] = _dot(xm(0), wrkv_ref[0])
    k_out[0] = _dot(xm(1), wrkv_ref[1])
    v = _dot(xv, wrkv_ref[2])
    wl = vec_ref[0:1, :] + _dot(jnp.tanh(_dot(xm(3), w1_ref[...])), w2_ref[...])
    lw_out[0] = -jax.nn.sigmoid(wl) * math.exp(-0.5)
    a_out[0] = jax.nn.sigmoid(vec_ref[1:2, :] + _dot(_dot(xm(4), a1_ref[...]), a2_ref[...]))
    g_out[0] = _dot(jax.nn.sigmoid(_dot(xm(5), g1_ref[...])), g2_ref[...])
    if has_vres:
        mixv = jax.nn.sigmoid(vec_ref[2:3, :] + _dot(_dot(xv, v1_ref[...]), v2_ref[...]))
        v = v + (vfirst_ref[0] - v) * mixv
    v_out[0] = v


def _rwkv_proj(x, mix, w_rkv, w1, w2, a1, a2, g1, g2, vecs, vres):
    B, S, D = x.shape
    tm = _row_tile(S, 512)
    has_vres = vres is not None
    full = lambda a: pl.BlockSpec(a.shape, lambda b, i: (0,) * a.ndim)
    tile = pl.BlockSpec((1, tm, D), lambda b, i: (b, i, 0))
    prev = pl.BlockSpec((1, SUBLANES, D),
                        lambda b, i: (b, jnp.maximum(i * (tm // SUBLANES) - 1, 0), 0))
    ins = [x, x, mix, w_rkv, w1, w2, a1, a2, g1, g2, vecs]
    specs = [tile, prev] + [full(a) for a in ins[2:]]
    if has_vres:
        v1, v2, v_first = vres
        ins += [v1, v2, v_first]
        specs += [full(v1), full(v2), tile]
    out = jax.ShapeDtypeStruct((B, S, D), F32)
    return pl.pallas_call(
        functools.partial(_rwkv_proj_kernel, has_vres),
        grid=(B, S // tm),
        in_specs=specs,
        out_specs=[tile] * 6,
        out_shape=[out] * 6,
        compiler_params=_params("parallel", "arbitrary"),
    )(*ins)


def _wkv_kernel(r_ref, lw_ref, k_ref, v_ref, a_ref, hp_ref, y_out,
                state, avec_s, bvec_s, kmod_s, y_s):
    C = WKV_CHUNK
    ts = r_ref.shape[1]
    n_tiles = r_ref.shape[2] // LANES
    n_chunks = ts // C
    R2 = HEADS_PER_TILE * C
    inv_n = 1.0 / RWKV_HEAD_DIM

    @pl.when(pl.program_id(2) == 0)
    def _():
        state[...] = jnp.zeros_like(state)

    li = lax.broadcasted_iota(jnp.int32, (LANES, LANES), 0)
    lj = lax.broadcasted_iota(jnp.int32, (LANES, LANES), 1)
    head_ones = (li // RWKV_HEAD_DIM == lj // RWKV_HEAD_DIM).astype(BF16)

    for t in range(n_tiles):
        ls = slice(t * LANES, (t + 1) * LANES)
        k = k_ref[0, :, ls]
        a = a_ref[0, :, ls]
        kk = k * hp_ref[0:1, ls]
        n2 = _dot(kk * kk, head_ones)
        kk = kk / jnp.maximum(jnp.sqrt(n2), L2_EPS)
        avec_s[:, ls] = -kk
        bvec_s[:, ls] = kk * a
        kmod_s[:, ls] = k * (1.0 + (a - 1.0) * hp_ref[1:2, ls])

    ri = lax.broadcasted_iota(jnp.int32, (R2, R2), 0)
    rj = lax.broadcasted_iota(jnp.int32, (R2, R2), 1)
    same_head = (ri // C) == (rj // C)
    strict = same_head & ((ri % C) > (rj % C))
    incl = same_head & ((ri % C) >= (rj % C))
    ci = lax.broadcasted_iota(jnp.int32, (C, C), 0)
    cj = lax.broadcasted_iota(jnp.int32, (C, C), 1)
    tri = (ci >= cj).astype(BF16)
    lane = lax.broadcasted_iota(jnp.int32, (C, LANES), 1)
    head0 = lane < RWKV_HEAD_DIM
    n_doublings = int(math.log2(C))

    def stack(t):
        return jnp.concatenate([jnp.where(head0, t, 0.0), jnp.where(head0, 0.0, t)], axis=0)

    tiles = range(n_tiles)
    lanes_of = lambda t: slice(t * LANES, (t + 1) * LANES)

    def chunk(c, carry):
        sl = pl.ds(pl.multiple_of(c * C, C), C)
        each = lambda f: [f(t) for t in tiles]
        s0 = each(lambda t: state[t])
        lw = each(lambda t: lw_ref[0, sl, lanes_of(t)])
        cum = each(lambda t: _dot_exact_lhs(tri, lw[t]))
        tot = each(lambda t: cum[t][C - 1:C, :])
        e_neg = each(lambda t: jnp.exp(-cum[t]))
        av = each(lambda t: avec_s[sl, lanes_of(t)])
        bv = each(lambda t: bvec_s[sl, lanes_of(t)])
        km = each(lambda t: kmod_s[sl, lanes_of(t)])
        lhs = each(lambda t: jnp.concatenate(
            [stack(av[t] * jnp.exp(cum[t] - lw[t])),
             stack(r_ref[0, sl, lanes_of(t)] * jnp.exp(cum[t]))], axis=0))
        rhs = each(lambda t: jnp.concatenate([stack(bv[t] * e_neg[t]), stack(km[t] * e_neg[t])], axis=0))
        g = each(lambda t: _dot_nt(lhs[t], rhs[t]))
        a_s = each(lambda t: _dot_nt(lhs[t], s0[t]))
        vm = each(lambda t: stack(v_ref[0, sl, lanes_of(t)]))
        akv = each(lambda t: _dot(jnp.where(strict, g[t][:R2, R2:], 0.0), vm[t]))
        u = each(lambda t: a_s[t][:R2] + akv[t])
        p = each(lambda t: jnp.where(strict, g[t][:R2, :R2], 0.0))
        for it in range(n_doublings):
            pu = each(lambda t: _dot(p[t], u[t]))
            if it + 1 < n_doublings:
                p = each(lambda t: _dot(p[t], p[t]))
            u = each(lambda t: u[t] + pu[t])
        uv = each(lambda t: jnp.concatenate([u[t], vm[t]], axis=0))
        w_y = each(lambda t: jnp.concatenate([jnp.where(incl, g[t][R2:, :R2], 0.0),
                                              jnp.where(incl, g[t][R2:, R2:], 0.0)], axis=1))
        y_sm = each(lambda t: a_s[t][R2:] + _dot(w_y[t], uv[t]))
        bk_end = each(lambda t: jnp.concatenate(
            [stack(bv[t] * jnp.exp(tot[t] - cum[t])), stack(km[t] * jnp.exp(tot[t] - cum[t]))], axis=0))
        s1 = each(lambda t: s0[t] * jnp.exp(tot[t]) + _dot(uv[t].T, bk_end[t]))
        for t in tiles:
            y_s[sl, lanes_of(t)] = y_sm[t][:C] + y_sm[t][C:]
            state[t] = s1[t]
        return carry

    lax.fori_loop(0, n_chunks, chunk, 0)

    for t in range(n_tiles):
        ls = slice(t * LANES, (t + 1) * LANES)
        y = y_s[:, ls]
        mu = _dot(y, head_ones) * inv_n
        d = y - mu
        var = _dot(d * d, head_ones) * inv_n
        yn = d * lax.rsqrt(var + RWKV_GN_EPS) * hp_ref[3:4, ls] + hp_ref[4:5, ls]
        bonus = _dot(r_ref[0, :, ls] * kmod_s[:, ls] * hp_ref[2:3, ls], head_ones)
        y_out[0, :, ls] = yn + bonus * v_ref[0, :, ls]


def _wkv(r, lw, k, v, a, head_params):
    B, S, D = r.shape
    ts = _row_tile(S, 512)
    assert ts % WKV_CHUNK == 0
    width = WKV_TILES_PER_STEP * LANES
    tile = pl.BlockSpec((1, ts, width), lambda b, p, i: (b, i, p))
    hp = pl.BlockSpec((SUBLANES, width), lambda b, p, i: (0, p))
    return pl.pallas_call(
        _wkv_kernel,
        grid=(B, D // width, S // ts),
        in_specs=[tile] * 5 + [hp],
        out_specs=tile,
        out_shape=jax.ShapeDtypeStruct((B, S, D), F32),
        scratch_shapes=[pltpu.VMEM((WKV_TILES_PER_STEP, LANES, LANES), F32)]
        + [pltpu.VMEM((ts, width), F32)] * 4,
        compiler_params=_params("parallel", "parallel", "arbitrary"),
    )(r, lw, k, v, a, head_params)


def _matmul_kernel(a_ref, w_ref, o_ref):
    o_ref[...] = _dot(a_ref[...], w_ref[...])


def _matmul(a, w):
    T, K = a.shape
    N = w.shape[1]
    tm = _row_tile(T, 512)
    return pl.pallas_call(
        _matmul_kernel,
        grid=(T // tm,),
        in_specs=[pl.BlockSpec((tm, K), lambda i: (i, 0)), pl.BlockSpec((K, N), lambda i: (0, 0))],
        out_specs=pl.BlockSpec((tm, N), lambda i: (i, 0)),
        out_shape=jax.ShapeDtypeStruct((T, N), F32),
        compiler_params=_params("parallel"),
    )(a, w)


def _to_row_tiles(ref, value):
    n = value.shape[1] // LANES
    for c in range(n):
        ref[pl.ds(c, value.shape[0], stride=n), :] = value[:, c * LANES:(c + 1) * LANES]


def _from_row_tiles(ref, n_rows, n):
    return jnp.concatenate([ref[pl.ds(c, n_rows, stride=n), :] for c in range(n)], axis=1)


def _bf16_bits(v):
    return lax.bitcast_convert_type(v.astype(BF16).astype(F32), jnp.uint32)


def _to_packed_tiles(ref, value):
    m = value.shape[1] // (2 * LANES)
    for c in range(m):
        lo = _bf16_bits(value[:, c * LANES:(c + 1) * LANES]) >> 16
        hi = _bf16_bits(value[:, (c + m) * LANES:(c + m + 1) * LANES]) & jnp.uint32(0xFFFF0000)
        ref[pl.ds(c, value.shape[0], stride=m), :] = lo | hi


def _from_packed_tiles(ref, n_rows, m):
    words = [ref[pl.ds(c, n_rows, stride=m), :] for c in range(m)]
    as_f32 = lambda w: lax.bitcast_convert_type(w, F32)
    chunks = ([as_f32(w << 16) for w in words]
              + [as_f32(w & jnp.uint32(0xFFFF0000)) for w in words])
    return jnp.concatenate(chunks, axis=1).astype(BF16)


def _proj_ln_kernel(has_gate, *refs):
    if has_gate:
        a_ref, g_ref, w_ref, x_ref, ln_ref, o_ref, o_tiles_ref = refs
        a = a_ref[...] * g_ref[...]
    else:
        a_ref, w_ref, x_ref, ln_ref, o_ref, o_tiles_ref = refs
        a = a_ref[...]
    z = DEEPNORM_ALPHA * x_ref[...] + _dot(a, w_ref[...])
    out = _layer_norm(z, ln_ref[0:1, :], ln_ref[1:2, :])
    o_ref[...] = out
    _to_packed_tiles(o_tiles_ref, out)


def _proj_ln(a, gate, w, x, ln):
    T, D = x.shape
    tm = _row_tile(T, 512)
    n = D // (2 * LANES)
    tile = pl.BlockSpec((tm, D), lambda i: (i, 0))
    full = lambda t: pl.BlockSpec(t.shape, lambda i: (0, 0))
    has_gate = gate is not None
    ins = [a] + ([gate] if has_gate else []) + [w, x, ln]
    specs = [tile] + ([tile] if has_gate else []) + [full(w), tile, full(ln)]
    return pl.pallas_call(
        functools.partial(_proj_ln_kernel, has_gate),
        grid=(T // tm,),
        in_specs=specs,
        out_specs=[tile, pl.BlockSpec((tm * n, LANES), lambda i: (i, 0))],
        out_shape=[jax.ShapeDtypeStruct((T, D), F32), jax.ShapeDtypeStruct((T * n, LANES), jnp.uint32)],
        compiler_params=_params("parallel"),
    )(*ins)


def _max_rows(s, groups=4):
    step = s.shape[0] // groups
    part = s[0:step]
    for g in range(1, groups):
        part = jnp.maximum(part, s[g * step:(g + 1) * step])
    return jnp.max(part, axis=0, keepdims=True)


def _attn_kernel(out_scale, q_ref, k_ref, v_ref, slope_ref, lam_ref, g_ref, o_ref,
                 k_s, vt_s, acc_s):
    i = pl.program_id(2)
    tq = q_ref.shape[1]
    tk = tq
    n_heads = q_ref.shape[2] // LANES
    n_kv_blocks = k_ref.shape[1] // tk
    heads = range(n_heads)
    chains = [(h, c) for h in heads for c in range(2)]
    lanes_of = lambda h: slice(h * LANES, (h + 1) * LANES)

    @pl.when(i == 0)
    def _():
        lane = lax.broadcasted_iota(jnp.int32, (tk, LANES), 1)
        row = lax.broadcasted_iota(jnp.int32, (tk, LANES), 0).astype(F32)
        for h in heads:
            key_bias = slope_ref[0, :, h * LANES:h * LANES + 1] * row
            for blk in range(n_kv_blocks):
                sl = slice(blk * tk, (blk + 1) * tk)
                kb = k_ref[0, sl, lanes_of(h)]
                k_s[2 * h, sl, :] = jnp.where(
                    lane < DA_HEAD_DIM, kb, jnp.where(lane == DA_HEAD_DIM, key_bias, 0.0)).astype(BF16)
                k_s[2 * h + 1, sl, :] = jnp.where(
                    lane >= DA_HEAD_DIM, kb, jnp.where(lane == 0, key_bias, 0.0)).astype(BF16)
                vt_s[h, blk, 0:LANES, :] = v_ref[0, sl, lanes_of(h)].T.astype(BF16)
                vt_s[h, blk, LANES:, :] = jnp.ones((ATTN_ONES_ROWS, tk), BF16)

    q_lane = lax.broadcasted_iota(jnp.int32, (tq, LANES), 1)
    q = []
    for h in heads:
        qs = q_ref[0, :, lanes_of(h)] * (DA_HEAD_DIM ** -0.5)
        q.append(jnp.where(q_lane < DA_HEAD_DIM, qs,
                           jnp.where(q_lane == DA_HEAD_DIM, 1.0, 0.0)).astype(BF16))
        q.append(jnp.where(q_lane >= DA_HEAD_DIM, qs, jnp.where(q_lane == 0, 1.0, 0.0)).astype(BF16))
    slope = [slope_ref[0, :, h * LANES:h * LANES + 1] for h in heads]
    k_loc = lax.broadcasted_iota(jnp.int32, (tk, tq), 0)
    q_loc = lax.broadcasted_iota(jnp.int32, (tk, tq), 1)
    acc_s[...] = jnp.zeros_like(acc_s)

    def block(j, ms, diagonal):
        sl = pl.ds(pl.multiple_of(j * tk, tk), tk)
        rel = ((j - i) * tk).astype(F32)
        each = lambda f: [f(n, h) for n, (h, _) in enumerate(chains)]
        s = each(lambda n, h: _dot_nt(k_s[n, sl, :], q[n]))
        if diagonal:
            s = each(lambda n, h: jnp.where(k_loc > q_loc, NEG_BIG, s[n]))
        offset = [slope[h] * rel for h in heads]
        m_new = each(lambda n, h: jnp.maximum(ms[n], _max_rows(s[n]) + offset[h]))
        p = each(lambda n, h: jnp.exp(s[n] - (m_new[n] - offset[h])).astype(BF16))
        acc_old = each(lambda n, h: acc_s[n])
        pv = each(lambda n, h: jnp.dot(vt_s[h, j], p[n], preferred_element_type=F32))
        for n in range(len(chains)):
            acc_s[n] = jnp.exp(ms[n] - m_new[n]) * acc_old[n] + pv[n]
        return tuple(m_new)

    init = tuple(jnp.full((1, tq), NEG_BIG, F32) for _ in chains)
    ms = lax.fori_loop(0, i, lambda j, ms: block(j, ms, False), init)
    block(i, ms, True)

    for h in heads:
        acc0, acc1 = acc_s[2 * h], acc_s[2 * h + 1]
        o_t = (acc0[:LANES] / acc0[LANES:LANES + 1]
               - lam_ref[:, 0:1] * (acc1[:LANES] / acc1[LANES:LANES + 1]))
        o = o_t.T
        o = o * lax.rsqrt(jnp.mean(o * o, axis=-1, keepdims=True) + SUBLN_EPS) * g_ref[...]
        o_ref[0, :, lanes_of(h)] = o * out_scale


def _diff_attention(q, q_col0, kv, slopes, lam_row, subln_g, lambda_init):
    B, S, _ = q.shape
    D = DA_HEADS * 2 * DA_HEAD_DIM
    tq = _row_tile(S, 256)
    nh = ATTN_HEADS_PER_STEP
    width = nh * LANES
    n_groups = DA_HEADS // nh
    q_group0 = q_col0 // width
    return pl.pallas_call(
        functools.partial(_attn_kernel, 1.0 - lambda_init),
        grid=(B, n_groups, S // tq),
        in_specs=[pl.BlockSpec((1, tq, width), lambda b, h, i: (b, i, q_group0 + h)),
                  pl.BlockSpec((1, S, width), lambda b, h, i: (b, 0, h)),
                  pl.BlockSpec((1, S, width), lambda b, h, i: (b, 0, n_groups + h)),
                  pl.BlockSpec((1, 1, width), lambda b, h, i: (h, 0, 0)),
                  pl.BlockSpec((1, LANES), lambda b, h, i: (0, 0)),
                  pl.BlockSpec((1, LANES), lambda b, h, i: (0, 0))],
        out_specs=pl.BlockSpec((1, tq, width), lambda b, h, i: (b, i, h)),
        out_shape=jax.ShapeDtypeStruct((B, S, D), F32),
        scratch_shapes=[pltpu.VMEM((2 * nh, S, LANES), BF16),
                        pltpu.VMEM((nh, S // tq, LANES + ATTN_ONES_ROWS, tq), BF16),
                        pltpu.VMEM((2 * nh, LANES + ATTN_ONES_ROWS, tq), F32)],
        compiler_params=_params("parallel", "parallel", "arbitrary"),
    )(q, kv, kv, slopes.reshape(n_groups, 1, width), lam_row, subln_g)


def _router_kernel(x_ref, w_ref, b_ref, idx_out, gate_out, count_out):
    @pl.when(pl.program_id(0) == 0)
    def _():
        count_out[...] = jnp.zeros_like(count_out)

    x = x_ref[...]
    xh, xl = _split2(x)
    w = w_ref[...]
    hh = jnp.dot(xh, w, preferred_element_type=F32)
    lh = jnp.dot(xl, w[:, :LANES], preferred_element_type=F32)
    logits = hh[:, :LANES] + (hh[:, LANES:] + lh) + b_ref[...]
    lane = lax.broadcasted_iota(jnp.int32, logits.shape, 1)
    vals, idxs = [], []
    cur = logits
    for _ in range(TOP_K):
        m = jnp.max(cur, axis=-1, keepdims=True)
        idx = jnp.min(jnp.where(cur == m, lane, LANES), axis=-1, keepdims=True)
        cur = jnp.where(lane == idx, -jnp.inf, cur)
        vals.append(m)
        idxs.append(idx)
    es = [jnp.exp(v - vals[0]) for v in vals]
    denom = es[0] + es[1] + es[2] + es[3]
    idx_row = jnp.zeros(logits.shape, jnp.int32)
    gate_row = jnp.zeros(logits.shape, F32)
    for kx in range(TOP_K):
        idx_row = jnp.where(lane == kx, idxs[kx], idx_row)
        gate_row = jnp.where(lane == kx, es[kx] / denom, gate_row)
    idx_out[...] = idx_row
    gate_out[...] = gate_row
    picked = sum((lane == idx).astype(jnp.int32) for idx in idxs)
    count_out[0:1, :] += jnp.sum(picked, axis=0, keepdims=True)


def _router(x, w3, b_row):
    T, D = x.shape
    tm = _row_tile(T, 512)
    tile = pl.BlockSpec((tm, D), lambda i: (i, 0))
    out_tile = pl.BlockSpec((tm, LANES), lambda i: (i, 0))
    idx_rows, gate_rows, counts = pl.pallas_call(
        _router_kernel,
        grid=(T // tm,),
        in_specs=[tile, pl.BlockSpec(w3.shape, lambda i: (0, 0)),
                  pl.BlockSpec(b_row.shape, lambda i: (0, 0))],
        out_specs=[out_tile, out_tile, pl.BlockSpec((SUBLANES, LANES), lambda i: (0, 0))],
        out_shape=[jax.ShapeDtypeStruct((T, LANES), jnp.int32),
                   jax.ShapeDtypeStruct((T, LANES), F32),
                   jax.ShapeDtypeStruct((SUBLANES, LANES), jnp.int32)],
        compiler_params=_params("arbitrary"),
    )(x, w3, b_row)
    return idx_rows, gate_rows, counts[0, :N_EXPERTS]


def _clamped_swiglu(gate, up):
    gate = jnp.minimum(gate, SWIGLU_LIMIT)
    up = jnp.clip(up, -SWIGLU_LIMIT, SWIGLU_LIMIT)
    return (up + 1.0) * (gate * jax.nn.sigmoid(SWIGLU_ALPHA * gate))


def _moe_kernel(bexp_ref, glast_ref, g_first, g_b, g_next, s_prev, s_a, s_b, x_hbm,
                wgu_a, bgu_a, wdn_a, bdn_a, wgu_b, bgu_b, wdn_b, bdn_b, y_hbm,
                xbuf_a, xbuf_b, ybuf_a, ybuf_b, zbuf, sem_ga, sem_gb, sem_sa, sem_sb, sem_z):
    del bexp_ref
    g = pl.program_id(0)
    last = glast_ref[0]

    n = wdn_a.shape[2] // LANES
    m = wgu_a.shape[1] // (2 * LANES)

    def gather(idx_ref, xbuf, sem, j):
        return pltpu.make_async_copy(x_hbm.at[pl.ds(pl.multiple_of(idx_ref[0, 0, j], m), m)],
                                     xbuf.at[pl.ds(j * m, m)], sem)

    def scatter(idx_ref, ybuf, sem, j):
        return pltpu.make_async_copy(ybuf.at[pl.ds(j * n, n)],
                                     y_hbm.at[pl.ds(pl.multiple_of(idx_ref[0, 0, j], n), n)], sem)

    def wait_gather(xbuf, sem):
        pltpu.make_async_copy(x_hbm.at[pl.ds(0, EXPERT_BLOCK * m)], xbuf, sem).wait()

    def wait_scatter(ybuf, sem):
        pltpu.make_async_copy(ybuf, y_hbm.at[pl.ds(0, EXPERT_BLOCK * n)], sem).wait()

    def expert_phase(xbuf, ybuf, wgu, bgu, wdn, bdn, sem_x, sem_y, gather_next, scatter_done):
        ff = wdn.shape[1]
        for j in range(EXPERT_BLOCK):
            gather_next(j).start(priority=j % DMA_PRIORITIES)
        wait_gather(xbuf, sem_x)
        h = _dot(_from_packed_tiles(xbuf, EXPERT_BLOCK, m), wgu[0]) + bgu[0]
        act = _clamped_swiglu(h[:, :ff], h[:, ff:])
        for j in range(EXPERT_BLOCK):
            scatter_done(j).start(priority=j % DMA_PRIORITIES)
        wait_scatter(ybuf, sem_y)
        _to_row_tiles(ybuf, _dot(act, wdn[0]) + bdn[0])

    @pl.when(g == 0)
    def _():
        ybuf_a[...] = jnp.zeros_like(ybuf_a)
        ybuf_b[...] = jnp.zeros_like(ybuf_b)
        zbuf[...] = jnp.zeros_like(zbuf)
        spare0 = y_hbm.shape[0] - EXPERT_BLOCK * n

        def first(j, c):
            gather(g_first, xbuf_a, sem_ga, j).start()
            pltpu.make_async_copy(ybuf_a.at[pl.ds(j * n, n)],
                                  y_hbm.at[pl.ds(pl.multiple_of(spare0 + j * n, n), n)], sem_sa).start()
            return c
        lax.fori_loop(0, EXPERT_BLOCK, first, 0)

    @pl.when(g <= last)
    def _():
        expert_phase(xbuf_a, ybuf_a, wgu_a, bgu_a, wdn_a, bdn_a, sem_ga, sem_sa,
                     lambda j: gather(g_b, xbuf_b, sem_gb, j),
                     lambda j: scatter(s_prev, ybuf_b, sem_sb, j))
        expert_phase(xbuf_b, ybuf_b, wgu_b, bgu_b, wdn_b, bdn_b, sem_gb, sem_sb,
                     lambda j: gather(g_next, xbuf_a, sem_ga, j),
                     lambda j: scatter(s_a, ybuf_a, sem_sa, j))

    @pl.when(g > last)
    def _():
        fills = [pltpu.make_async_copy(zbuf, y_hbm.at[pl.ds(pl.multiple_of(plan[0, 0, 0], n),
                                                           EXPERT_BLOCK * n)], sem_z)
                 for plan in (s_a, s_b)]
        for f in fills:
            f.start()
        for f in fills:
            f.wait()

    @pl.when(g == last)
    def _():
        def final(j, c):
            scatter(s_b, ybuf_b, sem_sb, j).start()
            return c
        lax.fori_loop(0, EXPERT_BLOCK, final, 0)
        wait_gather(xbuf_a, sem_ga)
        wait_scatter(ybuf_a, sem_sa)
        wait_scatter(ybuf_b, sem_sb)


def _moe_experts(x_tiles, gather_tok, scatter_row, block_exp, last_step, n_out_rows, w_gu, b_gu, w_dn, b_dn):
    D = w_gu.shape[1]
    n = D // LANES
    n_blocks = block_exp.shape[0]
    assert n_blocks % 2 == 0
    FF2 = w_gu.shape[2]
    idx_spec = lambda fn: pl.BlockSpec((1, 1, EXPERT_BLOCK), fn, memory_space=pltpu.SMEM)
    weights = lambda off: [
        pl.BlockSpec((1, D, FF2), lambda g, be, gl: (be[2 * g + off], 0, 0)),
        pl.BlockSpec((1, 1, FF2), lambda g, be, gl: (be[2 * g + off], 0, 0)),
        pl.BlockSpec((1, EXPERT_FF, D), lambda g, be, gl: (be[2 * g + off], 0, 0)),
        pl.BlockSpec((1, 1, D), lambda g, be, gl: (be[2 * g + off], 0, 0)),
    ]
    grid_spec = pltpu.PrefetchScalarGridSpec(
        num_scalar_prefetch=2,
        grid=(n_blocks // 2,),
        in_specs=[
            idx_spec(lambda g, be, gl: (0, 0, 0)),
            idx_spec(lambda g, be, gl: (2 * g + 1, 0, 0)),
            idx_spec(lambda g, be, gl: (2 * g + 2, 0, 0)),
            idx_spec(lambda g, be, gl: (2 * g, 0, 0)),
            idx_spec(lambda g, be, gl: (2 * g + 1, 0, 0)),
            idx_spec(lambda g, be, gl: (2 * g + 2, 0, 0)),
            pl.BlockSpec(memory_space=pl.ANY),
        ] + weights(0) + weights(1),
        out_specs=pl.BlockSpec(memory_space=pl.ANY),
        scratch_shapes=[pltpu.VMEM((EXPERT_BLOCK * n // 2, LANES), jnp.uint32)] * 2
        + [pltpu.VMEM((EXPERT_BLOCK * n, LANES), F32)] * 3
        + [pltpu.SemaphoreType.DMA(())] * 5,
    )
    b_gu = b_gu.reshape(N_EXPERTS, 1, FF2)
    b_dn = b_dn.reshape(N_EXPERTS, 1, D)
    return pl.pallas_call(
        _moe_kernel,
        grid_spec=grid_spec,
        out_shape=jax.ShapeDtypeStruct((n_out_rows * n, LANES), F32),
        compiler_params=_params("arbitrary"),
    )(block_exp, last_step, gather_tok, gather_tok, gather_tok, scatter_row, scatter_row, scatter_row, x_tiles,
      w_gu, b_gu, w_dn, b_dn, w_gu, b_gu, w_dn, b_dn)


def _combine_ln_kernel(y0_ref, y1_ref, y2_ref, y3_ref, gate_ref, x_ref, ln_ref, o_ref):
    gate = gate_ref[...]
    tm, D = x_ref.shape
    ffn = jnp.zeros(x_ref.shape, F32)
    for kx, y_ref in enumerate((y0_ref, y1_ref, y2_ref, y3_ref)):
        ffn = ffn + _from_row_tiles(y_ref, tm, D // LANES) * gate[:, kx:kx + 1]
    z = DEEPNORM_ALPHA * x_ref[...] + ffn
    o_ref[...] = _layer_norm(z, ln_ref[0:1, :], ln_ref[1:2, :])


def _combine_ln(y_rows, gate_rows, x, ln):
    T, D = x.shape
    tm = _row_tile(T, 512)
    tile = pl.BlockSpec((tm, D), lambda i: (i, 0))
    slot = lambda kx: pl.BlockSpec((tm * (D // LANES), LANES), lambda i: (kx * (T // tm) + i, 0))
    return pl.pallas_call(
        _combine_ln_kernel,
        grid=(T // tm,),
        in_specs=[slot(kx) for kx in range(TOP_K)]
        + [pl.BlockSpec((tm, LANES), lambda i: (i, 0)), tile, pl.BlockSpec(ln.shape, lambda i: (0, 0))],
        out_specs=tile,
        out_shape=jax.ShapeDtypeStruct((T, D), F32),
        compiler_params=_params("parallel"),
    )(y_rows, y_rows, y_rows, y_rows, gate_rows, x, ln)


def _routing_plan(top_idx, counts):
    T = top_idx.shape[0]
    TK = T * TOP_K
    order = jnp.argsort(top_idx.reshape(TK)).astype(jnp.int32)
    padded = (counts + EXPERT_BLOCK - 1) // EXPERT_BLOCK * EXPERT_BLOCK
    pad_end = jnp.cumsum(padded)
    pad_start = pad_end - padded
    grp_start = jnp.cumsum(counts) - counts
    spare_start = TK + jnp.cumsum(padded - counts) - (padded - counts)
    n_blocks = -(-TK // EXPERT_BLOCK) + N_EXPERTS
    n_rows = n_blocks * EXPERT_BLOCK
    block_start = jnp.arange(n_blocks, dtype=jnp.int32) * EXPERT_BLOCK
    block_exp = jnp.minimum(jnp.sum(pad_end[None, :] <= block_start[:, None], axis=1),
                            N_EXPERTS - 1).astype(jnp.int32)
    off = (block_start - pad_start[block_exp])[:, None] + jnp.arange(EXPERT_BLOCK, dtype=jnp.int32)[None, :]
    cnt = counts[block_exp][:, None]
    valid = off < cnt
    pair = order[jnp.clip(grp_start[block_exp][:, None] + off, 0, TK - 1)]
    tok, slot = pair // TOP_K, pair % TOP_K
    spare = spare_start[block_exp][:, None] + (off - cnt)
    scale = D_MODEL // LANES
    gather_tok = jnp.where(valid, tok, 0).astype(jnp.int32) * (scale // 2)
    scatter_row = jnp.where(valid, slot * T + tok, spare).astype(jnp.int32) * scale
    fill = jnp.arange(EXPERT_BLOCK, dtype=jnp.int32)[None, :]
    gather_tok = jnp.concatenate([gather_tok, jnp.zeros_like(fill)])
    scatter_row = jnp.concatenate([(n_rows + fill) * scale, scatter_row])
    shape = (n_blocks + 1, 1, EXPERT_BLOCK)
    n_used = jnp.maximum(pad_end[-1] // EXPERT_BLOCK, 1)
    last_step = ((n_used - 1) // 2).astype(jnp.int32).reshape(1)
    return (gather_tok.reshape(shape), scatter_row.reshape(shape), block_exp, last_step,
            n_rows + 2 * EXPERT_BLOCK)


def _moe_ffn_ln(x, x_tiles, router_w, router_b, w_gu, b_gu, w_dn, b_dn, ln):
    T, D = x.shape
    w_pad = jnp.zeros((D, LANES), F32).at[:, :N_EXPERTS].set(router_w)
    w3 = jnp.concatenate(_split2(w_pad), axis=1)
    b_row = jnp.full((1, LANES), NEG_BIG, F32).at[0, :N_EXPERTS].set(router_b)
    idx_rows, gate_rows, counts = _router(x, w3, b_row)
    gather_tok, scatter_row, block_exp, last_step, n_out_rows = _routing_plan(idx_rows[:, :TOP_K], counts)
    y_rows = _moe_experts(x_tiles, gather_tok, scatter_row, block_exp, last_step, n_out_rows,
                          w_gu, b_gu, w_dn, b_dn)
    return _combine_ln(y_rows, gate_rows, x, ln)


def _cast_kernel(w_ref, o_ref):
    o_ref[...] = w_ref[...].astype(BF16)


def _cast_bf16(w):
    lead = w.shape[:-2]
    w3 = w.reshape((-1,) + w.shape[-2:])
    spec = pl.BlockSpec((1,) + w3.shape[1:], lambda i: (i, 0, 0))
    out = pl.pallas_call(
        _cast_kernel,
        grid=(w3.shape[0],),
        in_specs=[spec],
        out_specs=spec,
        out_shape=jax.ShapeDtypeStruct(w3.shape, BF16),
        compiler_params=_params("parallel"),
    )(w3)
    return out.reshape(lead + w.shape[-2:])


def _pad_rows(rows, n=SUBLANES):
    out = jnp.zeros((n, rows[0].shape[-1]), F32)
    for j, r in enumerate(rows):
        out = out.at[j].set(r.reshape(-1))
    return out


def kernel(x, ln_g, ln_b, rwkv_mix, rwkv_w_rkv, rwkv_w_o, rwkv_w0, rwkv_w1, rwkv_w2, rwkv_a0, rwkv_a1, rwkv_a2, rwkv_g1, rwkv_g2, rwkv_k_k, rwkv_k_a, rwkv_r_k, rwkv_lnx_g, rwkv_lnx_b, rwkv_v0, rwkv_v1, rwkv_v2, kv_w, da_w_q, da_w_o, da_lambda, da_subln_g, moe_router_w, moe_router_b, moe_w_gu, moe_b_gu, moe_w_dn, moe_b_dn):
    B, S, D = x.shape
    T = B * S
    v_first = None
    kv = None
    slopes = 2.0 ** (-8.0 * jnp.arange(1, DA_HEADS + 1, dtype=F32) / DA_HEADS)
    slopes = jnp.repeat(slopes, LANES).reshape(1, DA_HEADS * LANES)
    w_gu_bf16 = _cast_bf16(moe_w_gu)
    w_dn_bf16 = _cast_bf16(moe_w_dn)
    for l in range(DEPTH):
        ln_mix = jnp.stack([ln_g[l, 0], ln_b[l, 0]])
        ln_ffn = jnp.stack([ln_g[l, 1], ln_b[l, 1]])
        xt = x.reshape(T, D)
        if l < N_A_LAYERS:
            zero = jnp.zeros((D,), F32)
            vecs = _pad_rows([rwkv_w0[l], rwkv_a0[l], rwkv_v0[l - 1] if l > 0 else zero])
            vres = None if l == 0 else (rwkv_v1[l - 1].astype(BF16), rwkv_v2[l - 1].astype(BF16), v_first)
            r, lw, k, v, a, g = _rwkv_proj(
                x, _pad_rows(list(rwkv_mix[l])), rwkv_w_rkv[l].astype(BF16),
                rwkv_w1[l].astype(BF16), rwkv_w2[l].astype(BF16), rwkv_a1[l].astype(BF16),
                rwkv_a2[l].astype(BF16), rwkv_g1[l].astype(BF16), rwkv_g2[l].astype(BF16), vecs, vres)
            if l == 0:
                v_first = v
            head_params = _pad_rows([rwkv_k_k[l], rwkv_k_a[l],
                                     jnp.tile(rwkv_r_k[l].reshape(-1), 1), rwkv_lnx_g[l], rwkv_lnx_b[l]])
            y = _wkv(r, lw, k, v, a, head_params)
            xt, xt_tiles = _proj_ln(y.reshape(T, D), g.reshape(T, D), rwkv_w_o[l].astype(BF16), xt, ln_mix)
        else:
            j = l - N_A_LAYERS
            if l == N_A_LAYERS:
                w_kvq = jnp.concatenate([kv_w, da_w_q[j]], axis=1).astype(BF16)
                kv = _matmul(xt, w_kvq).reshape(B, S, 3 * D)
                q, q_col0 = kv, 2 * D
            else:
                q, q_col0 = _matmul(xt, da_w_q[j].astype(BF16)).reshape(B, S, D), 0
            lambda_init = 0.8 - 0.6 * math.exp(-0.3 * l)
            lam = da_lambda[j].astype(F32)
            lam_full = (jnp.exp(jnp.sum(lam[0] * lam[1])) - jnp.exp(jnp.sum(lam[2] * lam[3]))
                        + lambda_init)
            o = _diff_attention(q, q_col0, kv, slopes, jnp.full((1, LANES), lam_full, F32),
                                da_subln_g[j].reshape(1, LANES), lambda_init)
            xt, xt_tiles = _proj_ln(o.reshape(T, D), None, da_w_o[j].astype(BF16), xt, ln_mix)
        xt = _moe_ffn_ln(xt, xt_tiles, moe_router_w[l], moe_router_b[l], w_gu_bf16[l], moe_b_gu[l],
                         w_dn_bf16[l], moe_b_dn[l], ln_ffn)
        x = xt.reshape(B, S, D)
    return x
```
